```python
import jax, jax.numpy as jnp
from jax import lax
import numpy as np

D_MODEL = 2048
BATCH = 4
SEQ = 8192
DEPTH = 2

PLE_DIM = 256
POOL_GROUPS = 4
POOL_GROUP_W = 256
POOL_W = POOL_GROUPS * POOL_GROUP_W
POOL_WINDOWS = (2, 4, 8, 16)
SB_HEADS = 8
SB_HEAD_DIM = 128
SB_W = SB_HEADS * SB_HEAD_DIM
SB_BLOCK = 128
GLA_HEADS = 4
GLA_DK = 128
GLA_DV = 256
GLA_K_W = GLA_HEADS * GLA_DK
GLA_V_W = GLA_HEADS * GLA_DV
GLA_RANK = 16
GLA_TAU = 16.0
GLA_CHUNK = 64
IN_COLS = POOL_W + 3 * SB_W + 2 * GLA_K_W + GLA_V_W + GLA_RANK + GLA_V_W
N_BRANCH = 3
D_FF = 5632
N_EXPERTS = 8
TOP_K = 2
N_DENSE = (DEPTH + 1) // 2
N_MOE = DEPTH // 2
EPS = 1e-6

kernel_name = 'hybrid_pool_stickbreak_gla_moe_block'


def rmsnorm(x, g):
    xf = x.astype(jnp.float32)
    y = xf * lax.rsqrt(jnp.mean(xf * xf, axis=-1, keepdims=True) + EPS)
    return (y * g.astype(jnp.float32)).astype(x.dtype)


def swiglu(h, w1, w3, w2):
    return (jax.nn.silu(h @ w1) * (h @ w3)) @ w2


def pool_mixer(u, w_groups, scale):
    B, S, _ = u.shape
    ug = u.astype(jnp.float32).reshape(B, S, POOL_GROUPS, POOL_GROUP_W)
    cs = jnp.concatenate([jnp.zeros_like(ug[:, :1]), jnp.cumsum(ug, axis=1)], axis=1)
    t = jnp.arange(S)
    outs = []
    for gi, w in enumerate(POOL_WINDOWS):
        lo = jnp.maximum(t + 1 - w, 0)
        csg = cs[:, :, gi]
        win_sum = csg[:, 1:] - jnp.take(csg, lo, axis=1)
        count = jnp.minimum(t + 1, w).astype(jnp.float32)
        outs.append(win_sum / count[None, :, None] - ug[:, :, gi])
    pooled = jnp.stack(outs, axis=2).astype(u.dtype)
    mixed = jnp.einsum('bsgc,gcd->bsgd', pooled, w_groups)
    return mixed.reshape(B, S, POOL_W) * scale


def stick_breaking_attention(q, k, v):
    B, S, H, hd = q.shape
    qf = (q.astype(jnp.float32) * (hd ** -0.5)).transpose(0, 2, 1, 3)
    kf = k.astype(jnp.float32).transpose(0, 2, 1, 3)
    vf = v.astype(jnp.float32).transpose(0, 2, 1, 3)
    nb = S // SB_BLOCK
    qb = qf.reshape(B, H, nb, SB_BLOCK, hd).transpose(2, 0, 1, 3, 4)
    s_idx = jnp.arange(S)

    def block(args):
        q_blk, bi = args
        z = jnp.einsum('bhqd,bhsd->bhqs', q_blk, kf)
        t_idx = bi * SB_BLOCK + jnp.arange(SB_BLOCK)
        mask = s_idx[None, :] < t_idx[:, None]
        log_beta = jax.nn.log_sigmoid(z)
        log_keep = jnp.where(mask, jax.nn.log_sigmoid(-z), 0.0)
        later = lax.cumsum(log_keep, axis=3, reverse=True) - log_keep
        a = jnp.where(mask, jnp.exp(log_beta + later), 0.0)
        return jnp.einsum('bhqs,bhsd->bhqd', a, vf)

    out = lax.map(block, (qb, jnp.arange(nb)))
    return out.transpose(1, 0, 3, 2, 4).reshape(B, S, H * hd).astype(q.dtype)


def gla_mixer(q, k, v, log_a):
    B, S, H, dk = q.shape
    dv = v.shape[-1]
    n = S // GLA_CHUNK

    def chunks(t):
        return t.astype(jnp.float32).reshape(B, n, GLA_CHUNK, H, t.shape[-1]).transpose(1, 0, 3, 2, 4)

    qc = chunks(q) * (dk ** -0.5)
    kc, vc, lc = chunks(k), chunks(v), chunks(log_a)
    causal = jnp.tril(jnp.ones((GLA_CHUNK, GLA_CHUNK), dtype=bool))[None, None, :, :, None]

    def step(state, inp):
        qi, ki, vi, li = inp
        b = jnp.cumsum(li, axis=2)
        inter = jnp.einsum('bhtk,bhkv->bhtv', qi * jnp.exp(b), state)
        diff = b[:, :, :, None, :] - b[:, :, None, :, :]
        decay = jnp.where(causal, jnp.exp(jnp.where(causal, diff, 0.0)), 0.0)
        scores = jnp.einsum('bhtk,bhsk,bhtsk->bhts', qi, ki, decay)
        intra = jnp.einsum('bhts,bhsv->bhtv', scores, vi)
        b_end = b[:, :, -1:, :]
        new_state = jnp.exp(b_end[:, :, 0, :])[..., None] * state + jnp.einsum(
            'bhsk,bhsv->bhkv', ki * jnp.exp(b_end - b), vi)
        return new_state, inter + intra

    s0 = jnp.zeros((B, H, dk, dv), jnp.float32)
    _, out = lax.scan(step, s0, (qc, kc, vc, lc))
    return out.transpose(1, 0, 3, 2, 4).reshape(B, S, H, dv)


def moe_swiglu(h, router, w1, w3, w2):
    logits = (h @ router).astype(jnp.float32)
    vals, idx = lax.top_k(logits, TOP_K)
    wts = jax.nn.softmax(vals, axis=-1)
    comb = jnp.sum(jax.nn.one_hot(idx, N_EXPERTS, dtype=jnp.float32) * wts[..., None], axis=-2)
    comb = comb.astype(h.dtype)
    y = jnp.zeros_like(h)
    for e in range(N_EXPERTS):
        y = y + comb[..., e:e + 1] * swiglu(h, w1[e], w3[e], w2[e])
    return y


def setup_inputs(seed: int = 0) -> dict:
    key = jax.random.key(seed)
    ks = jax.random.split(key, 32)
    f32 = jnp.float32

    def nrm(k, shape, fan_in):
        return jax.random.normal(k, shape, f32) * (fan_in ** -0.5)

    def gain(k, shape):
        return 1.0 + 0.02 * jax.random.normal(k, shape, f32)

    return {
        'x': jax.random.normal(ks[0], (BATCH, SEQ, D_MODEL), f32),
        'p': jax.random.normal(ks[1], (DEPTH, BATCH, SEQ, PLE_DIM), f32),
        'g_mix': gain(ks[2], (DEPTH, D_MODEL)),
        'w_in': nrm(ks[3], (DEPTH, D_MODEL, IN_COLS), D_MODEL),
        'w_branch_gate': nrm(ks[4], (DEPTH, D_MODEL, N_BRANCH * D_MODEL), D_MODEL),
        'b_branch_gate': 0.02 * jax.random.normal(ks[5], (DEPTH, N_BRANCH * D_MODEL), f32),
        'pool_w': nrm(ks[6], (DEPTH, POOL_GROUPS, POOL_GROUP_W, POOL_GROUP_W), POOL_GROUP_W),
        'pool_scale': gain(ks[7], (DEPTH, POOL_W)),
        'sb_gq': gain(ks[8], (DEPTH, SB_HEAD_DIM)),
        'sb_gk': gain(ks[9], (DEPTH, SB_HEAD_DIM)),
        'gla_w_lr': nrm(ks[10], (DEPTH, GLA_RANK, GLA_K_W), GLA_RANK),
        'gla_b_lr': 0.02 * jax.random.normal(ks[11], (DEPTH, GLA_K_W), f32),
        'gla_g_out': gain(ks[12], (DEPTH, GLA_DV)),
        'w_up_pool': nrm(ks[13], (DEPTH, POOL_W, D_MODEL), POOL_W),
        'w_up_sb': nrm(ks[14], (DEPTH, SB_W, D_MODEL), SB_W),
        'w_up_gla': nrm(ks[15], (DEPTH, GLA_V_W, D_MODEL), GLA_V_W),
        'w_o': nrm(ks[16], (DEPTH, D_MODEL, D_MODEL), D_MODEL),
        'g_ffn': gain(ks[17], (DEPTH, D_MODEL)),
        'ffn_w1': nrm(ks[18], (N_DENSE, D_MODEL, D_FF), D_MODEL),
        'ffn_w3': nrm(ks[19], (N_DENSE, D_MODEL, D_FF), D_MODEL),
        'ffn_w2': nrm(ks[20], (N_DENSE, D_FF, D_MODEL), D_FF),
        'moe_router': nrm(ks[21], (N_MOE, D_MODEL, N_EXPERTS), D_MODEL),
        'moe_w1': nrm(ks[22], (N_MOE, N_EXPERTS, D_MODEL, D_FF), D_MODEL),
        'moe_w3': nrm(ks[23], (N_MOE, N_EXPERTS, D_MODEL, D_FF), D_MODEL),
        'moe_w2': nrm(ks[24], (N_MOE, N_EXPERTS, D_FF, D_MODEL), D_FF),
        'ple_w_proj': nrm(ks[25], (DEPTH, PLE_DIM, D_MODEL), PLE_DIM),
        'ple_w_gate': nrm(ks[26], (DEPTH, D_MODEL, D_MODEL), D_MODEL),
    }


def reference(x, p, g_mix, w_in, w_branch_gate, b_branch_gate, pool_w, pool_scale, sb_gq, sb_gk,
              gla_w_lr, gla_b_lr, gla_g_out, w_up_pool, w_up_sb, w_up_gla, w_o, g_ffn,
              ffn_w1, ffn_w3, ffn_w2, moe_router, moe_w1, moe_w3, moe_w2, ple_w_proj, ple_w_gate):
    B, S, _ = x.shape
    split_at = list(np.cumsum([POOL_W, SB_W, SB_W, SB_W, GLA_K_W, GLA_K_W, GLA_V_W, GLA_RANK]))
    for i in range(DEPTH):
        h = rmsnorm(x, g_mix[i])
        z = h @ w_in[i]
        u_pool, sq, sk, sv, gq, gk, gv, g_lr, g_r = jnp.split(z, split_at, axis=-1)

        y_pool = pool_mixer(u_pool, pool_w[i], pool_scale[i])

        sq = rmsnorm(sq.reshape(B, S, SB_HEADS, SB_HEAD_DIM), sb_gq[i])
        sk = rmsnorm(sk.reshape(B, S, SB_HEADS, SB_HEAD_DIM), sb_gk[i])
        y_sb = stick_breaking_attention(sq, sk, sv.reshape(B, S, SB_HEADS, SB_HEAD_DIM))

        gate_pre = (g_lr @ gla_w_lr[i] + gla_b_lr[i]).astype(jnp.float32)
        log_a = jax.nn.log_sigmoid(gate_pre) / GLA_TAU
        o_gla = gla_mixer(gq.reshape(B, S, GLA_HEADS, GLA_DK), gk.reshape(B, S, GLA_HEADS, GLA_DK),
                          gv.reshape(B, S, GLA_HEADS, GLA_DV), log_a.reshape(B, S, GLA_HEADS, GLA_DK))
        o_gla = rmsnorm(o_gla, gla_g_out[i]).reshape(B, S, GLA_V_W).astype(x.dtype)
        y_gla = o_gla * jax.nn.silu(g_r)

        gates = jax.nn.sigmoid(h @ w_branch_gate[i] + b_branch_gate[i])
        ga, gb, gc = jnp.split(gates, N_BRANCH, axis=-1)
        merged = ga * (y_pool @ w_up_pool[i]) + gb * (y_sb @ w_up_sb[i]) + gc * (y_gla @ w_up_gla[i])
        x = x + merged @ w_o[i]

        h2 = rmsnorm(x, g_ffn[i])
        j = i // 2
        if i % 2 == 0:
            x = x + swiglu(h2, ffn_w1[j], ffn_w3[j], ffn_w2[j])
        else:
            x = x + moe_swiglu(h2, moe_router[j], moe_w1[j], moe_w3[j], moe_w2[j])

        x = x + jax.nn.sigmoid(x @ ple_w_gate[i]) * (p[i] @ ple_w_proj[i])
    return x
```

```python
import functools

import jax
import jax.numpy as jnp
from jax import lax
from jax.experimental import pallas as pl
from jax.experimental.pallas import tpu as pltpu

F32 = jnp.float32
BF16 = jnp.bfloat16

EPS = 1e-6
D_MODEL = 2048
PLE_DIM = 256
POOL_GROUPS = 4
POOL_GROUP_W = 256
POOL_W = POOL_GROUPS * POOL_GROUP_W
POOL_WINDOWS = (2, 4, 8, 16)
SB_HEADS = 8
SB_HEAD_DIM = 128
SB_W = SB_HEADS * SB_HEAD_DIM
GLA_HEADS = 4
GLA_DK = 128
GLA_DV = 256
GLA_K_W = GLA_HEADS * GLA_DK
GLA_V_W = GLA_HEADS * GLA_DV
GLA_RANK = 16
GLA_TAU = 16.0
D_FF = 5632
N_EXPERTS = 8
N_BRANCH = 3

LANES = 128
VMEM_LIMIT = 56 * 1024 * 1024

COL_POOL = 0
COL_SQ = COL_POOL + POOL_W
COL_SK = COL_SQ + SB_W
COL_SV = COL_SK + SB_W
COL_GQ = COL_SV + SB_W
COL_GK = COL_GQ + GLA_K_W
COL_GV = COL_GK + GLA_K_W
COL_GR = COL_GV + GLA_V_W
Z_COLS = COL_GR + GLA_V_W

GLA_CHUNK = 64
GLA_SUB = 16
GLA_EXP_CAP = 80.0
SB_F32_ZERO_LOG = -104.0


def _cparams(sem):
    return pltpu.CompilerParams(dimension_semantics=sem, vmem_limit_bytes=VMEM_LIMIT)


def _log_sigmoid(z):
    return jnp.minimum(z, 0.0) - jnp.log(1.0 + jnp.exp(-jnp.abs(z)))


def _sigmoid(z):
    return 1.0 / (1.0 + jnp.exp(-z))


def _split_bf16(x):
    hi = x.astype(BF16)
    lo = (x - hi.astype(F32)).astype(BF16)
    return hi, lo


def _rmsnorm_kernel(x_ref, g_ref, o_ref):
    x = x_ref[...]
    ms = jnp.mean(x * x, axis=-1, keepdims=True)
    o_ref[...] = (x * lax.rsqrt(ms + EPS) * g_ref[...]).astype(o_ref.dtype)


def rmsnorm(x, g, tm=512):
    n, d = x.shape
    return pl.pallas_call(
        _rmsnorm_kernel,
        out_shape=jax.ShapeDtypeStruct((n, d), BF16),
        grid=(n // tm,),
        in_specs=[pl.BlockSpec((tm, d), lambda i: (i, 0)),
                  pl.BlockSpec((1, d), lambda i: (0, 0))],
        out_specs=pl.BlockSpec((tm, d), lambda i: (i, 0)),
        compiler_params=_cparams(("parallel",)),
        name="rmsnorm",
    )(x, g.reshape(1, d))


def _rmsnorm_router_kernel(x_ref, g_ref, r_ref, o_ref, comb_ref):
    x = x_ref[...]
    ms = jnp.mean(x * x, axis=-1, keepdims=True)
    h = x * lax.rsqrt(ms + EPS) * g_ref[...]
    o_ref[...] = h.astype(o_ref.dtype)
    logits = jnp.dot(h, r_ref[...], preferred_element_type=F32, precision=lax.Precision.HIGHEST)
    lane = lax.broadcasted_iota(jnp.int32, logits.shape, 1).astype(F32)
    neg = jnp.float32(-jnp.inf)
    logits = jnp.where(lane < N_EXPERTS, logits, neg)
    m1 = jnp.max(logits, axis=-1, keepdims=True)
    i1 = jnp.min(jnp.where(logits == m1, lane, float(LANES)), axis=-1, keepdims=True)
    sel1 = lane == i1
    rest = jnp.where(sel1, neg, logits)
    m2 = jnp.max(rest, axis=-1, keepdims=True)
    i2 = jnp.min(jnp.where(rest == m2, lane, float(LANES)), axis=-1, keepdims=True)
    sel2 = lane == i2
    e2 = jnp.exp(m2 - m1)
    den = 1.0 + e2
    comb_ref[...] = jnp.where(sel1, 1.0 / den, 0.0) + jnp.where(sel2, e2 / den, 0.0)


def rmsnorm_router(x, g, router, tm=512):
    n, d = x.shape
    r_pad = jnp.zeros((d, LANES), F32).at[:, :N_EXPERTS].set(router)
    return pl.pallas_call(
        _rmsnorm_router_kernel,
        out_shape=(jax.ShapeDtypeStruct((n, d), BF16), jax.ShapeDtypeStruct((n, LANES), F32)),
        grid=(n // tm,),
        in_specs=[pl.BlockSpec((tm, d), lambda i: (i, 0)),
                  pl.BlockSpec((1, d), lambda i: (0, 0)),
                  pl.BlockSpec((d, LANES), lambda i: (0, 0))],
        out_specs=(pl.BlockSpec((tm, d), lambda i: (i, 0)),
                   pl.BlockSpec((tm, LANES), lambda i: (i, 0))),
        compiler_params=_cparams(("parallel",)),
        name="rmsnorm_router",
    )(x, g.reshape(1, d), r_pad)


def _mm_kernel(a_ref, w_ref, *rest, epilogue):
    acc = jnp.dot(a_ref[...], w_ref[...], preferred_element_type=F32)
    if epilogue == "plain":
        (o_ref,) = rest
    elif epilogue == "bias_sigmoid":
        b_ref, o_ref = rest
        acc = _sigmoid(acc + b_ref[...])
    elif epilogue == "residual":
        r_ref, o_ref = rest
        acc = acc + r_ref[...]
    o_ref[...] = acc.astype(o_ref.dtype)


def matmul(a, w, *, out_dtype, epilogue="plain", extra=None, tm=1024, tn=512, name="matmul"):
    n, k = a.shape
    m = w.shape[1]
    in_specs = [pl.BlockSpec((tm, k), lambda i, j: (i, 0)),
                pl.BlockSpec((k, tn), lambda i, j: (0, j))]
    args = [a, w]
    if epilogue == "bias_sigmoid":
        in_specs.append(pl.BlockSpec((1, tn), lambda i, j: (0, j)))
        args.append(extra.reshape(1, m))
    elif epilogue == "residual":
        in_specs.append(pl.BlockSpec((tm, tn), lambda i, j: (i, j)))
        args.append(extra)
    return pl.pallas_call(
        functools.partial(_mm_kernel, epilogue=epilogue),
        out_shape=jax.ShapeDtypeStruct((n, m), out_dtype),
        grid=(n // tm, m // tn),
        in_specs=in_specs,
        out_specs=pl.BlockSpec((tm, tn), lambda i, j: (i, j)),
        compiler_params=_cparams(("parallel", "arbitrary")),
        name=name,
    )(*args)


POOL_HALO = 128


def _pool_kernel(u_ref, w_ref, s_ref, o_ref, prev_ref, *, t):
    sb = pl.program_id(1)

    @pl.when(sb == 0)
    def _():
        prev_ref[...] = jnp.zeros_like(prev_ref)

    row = lax.broadcasted_iota(jnp.int32, (t, t), 0)
    col = lax.broadcasted_iota(jnp.int32, (t, t), 1)
    prow = lax.broadcasted_iota(jnp.int32, (t, POOL_HALO), 0)
    pcol = lax.broadcasted_iota(jnp.int32, (t, POOL_HALO), 1) - POOL_HALO
    tg = sb * t + lax.broadcasted_iota(jnp.int32, (t, 1), 0)
    for gi, w in enumerate(POOL_WINDOWS):
        cs = slice(gi * POOL_GROUP_W, (gi + 1) * POOL_GROUP_W)
        u = u_ref[:, cs]
        band_cur = ((col <= row) & (col > row - w)).astype(BF16)
        band_prev = ((pcol > prow - w) & (pcol + sb * t >= 0)).astype(BF16)
        win = jnp.dot(band_cur, u, preferred_element_type=F32)
        win = win + jnp.dot(band_prev, prev_ref[:, cs], preferred_element_type=F32)
        count = jnp.minimum(tg + 1, w).astype(F32)
        pooled = win / count - u.astype(F32)
        mixed = jnp.dot(pooled.astype(BF16), w_ref[gi], preferred_element_type=F32)
        o_ref[:, cs] = (mixed * s_ref[:, cs]).astype(o_ref.dtype)
    prev_ref[...] = u_ref[t - POOL_HALO:, :]


def pool_mixer(z3, pool_w, scale, t=256):
    b, s, _ = z3.shape
    return pl.pallas_call(
        functools.partial(_pool_kernel, t=t),
        out_shape=jax.ShapeDtypeStruct((b, s, POOL_W), BF16),
        grid=(b, s // t),
        in_specs=[pl.BlockSpec((None, t, POOL_W), lambda bi, si: (bi, si, COL_POOL // POOL_W)),
                  pl.BlockSpec((POOL_GROUPS, POOL_GROUP_W, POOL_GROUP_W), lambda bi, si: (0, 0, 0)),
                  pl.BlockSpec((1, POOL_W), lambda bi, si: (0, 0))],
        out_specs=pl.BlockSpec((None, t, POOL_W), lambda bi, si: (bi, si, 0)),
        scratch_shapes=[pltpu.VMEM((POOL_HALO, POOL_W), BF16)],
        compiler_params=_cparams(("parallel", "arbitrary")),
        name="pool_mixer",
    )(z3, pool_w, scale.reshape(1, POOL_W))


def _head_rmsnorm(x, g):
    ms = jnp.mean(x * x, axis=-1, keepdims=True)
    return x * lax.rsqrt(ms + EPS) * g


def _sb_kernel(q_ref, k_ref, v_ref, gq_ref, gk_ref, o_ref, acc_ref, c_ref, *, tq, tk):
    i = pl.program_id(2)
    q = _head_rmsnorm(q_ref[...].astype(F32), gq_ref[...]) * (SB_HEAD_DIM ** -0.5)
    q = q.astype(BF16)
    acc_ref[...] = jnp.zeros_like(acc_ref)
    c_ref[...] = jnp.zeros_like(c_ref)
    t_pos = i * tq + lax.broadcasted_iota(jnp.int32, (tq, tk), 0)
    s_off = lax.broadcasted_iota(jnp.int32, (tq, tk), 1)
    suffix = (lax.broadcasted_iota(jnp.int32, (tk, tk), 0)
              > lax.broadcasted_iota(jnp.int32, (tk, tk), 1)).astype(BF16)
    gk = gk_ref[...]

    def body(carry):
        j, _ = carry
        start = pl.multiple_of(j * tk, tk)
        k = _head_rmsnorm(k_ref[pl.ds(start, tk), :].astype(F32), gk).astype(BF16)
        z = lax.dot_general(q, k, (((1,), (1,)), ((), ())), preferred_element_type=F32)
        mask = (j * tk + s_off) < t_pos
        soft = jnp.log(1.0 + jnp.exp(-jnp.abs(z)))
        log_beta = jnp.minimum(z, 0.0) - soft
        log_keep = jnp.where(mask, jnp.minimum(-z, 0.0) - soft, 0.0)
        hi, lo = _split_bf16(log_keep)
        later = (jnp.dot(hi, suffix, preferred_element_type=F32)
                 + jnp.dot(lo, suffix, preferred_element_type=F32))
        c = c_ref[...]
        a = jnp.where(mask, jnp.exp(log_beta + later + c), 0.0)
        acc_ref[...] += jnp.dot(a.astype(BF16), v_ref[pl.ds(start, tk), :],
                                preferred_element_type=F32)
        c_new = c + later[:, 0:1] + log_keep[:, 0:1]
        c_ref[...] = c_new
        return j - 1, (jnp.max(c_new) > SB_F32_ZERO_LOG).astype(jnp.int32)

    def cond(carry):
        j, live = carry
        return jnp.logical_and(j >= 0, live > 0)

    j0 = ((i + 1) * tq) // tk - 1
    lax.while_loop(cond, body, (j0, jnp.int32(1)))
    o_ref[...] = acc_ref[...].astype(o_ref.dtype)


def sb_attention(z3, gq, gk, tq=128, tk=128):
    b, s, _ = z3.shape
    hd = SB_HEAD_DIM
    return pl.pallas_call(
        functools.partial(_sb_kernel, tq=tq, tk=tk),
        out_shape=jax.ShapeDtypeStruct((b, s, SB_W), BF16),
        grid=(b, SB_HEADS, s // tq),
        in_specs=[pl.BlockSpec((None, tq, hd), lambda bi, h, i: (bi, i, COL_SQ // hd + h)),
                  pl.BlockSpec((None, s, hd), lambda bi, h, i: (bi, 0, COL_SK // hd + h)),
                  pl.BlockSpec((None, s, hd), lambda bi, h, i: (bi, 0, COL_SV // hd + h)),
                  pl.BlockSpec((1, hd), lambda bi, h, i: (0, 0)),
                  pl.BlockSpec((1, hd), lambda bi, h, i: (0, 0))],
        out_specs=pl.BlockSpec((None, tq, hd), lambda bi, h, i: (bi, i, h)),
        scratch_shapes=[pltpu.VMEM((tq, hd), F32), pltpu.VMEM((tq, 1), F32)],
        compiler_params=_cparams(("parallel", "parallel", "arbitrary")),
        name="sb_attention",
    )(z3, z3, z3, gq.reshape(1, hd), gk.reshape(1, hd))


def _gla_gate_kernel(h_ref, wg_ref, wlr_ref, blr_ref, o_ref, *, tm):
    g = jnp.dot(h_ref[...], wg_ref[...], preferred_element_type=F32)
    pre = jnp.dot(g, wlr_ref[...], preferred_element_type=F32,
                  precision=lax.Precision.HIGHEST) + blr_ref[...]
    log_a = _log_sigmoid(pre) * (1.0 / GLA_TAU)
    row = lax.broadcasted_iota(jnp.int32, (tm, tm), 0)
    col = lax.broadcasted_iota(jnp.int32, (tm, tm), 1)
    shift = GLA_CHUNK.bit_length() - 1
    tri = ((col <= row) & ((col >> shift) == (row >> shift))).astype(BF16)
    hi, lo = _split_bf16(log_a)
    o_ref[...] = (jnp.dot(tri, hi, preferred_element_type=F32)
                  + jnp.dot(tri, lo, preferred_element_type=F32))


def gla_gate(h, w_glr, w_lr, b_lr, tm=256):
    n, d = h.shape
    return pl.pallas_call(
        functools.partial(_gla_gate_kernel, tm=tm),
        out_shape=jax.ShapeDtypeStruct((n, GLA_K_W), F32),
        grid=(n // tm,),
        in_specs=[pl.BlockSpec((tm, d), lambda i: (i, 0)),
                  pl.BlockSpec((d, LANES), lambda i: (0, 0)),
                  pl.BlockSpec((LANES, GLA_K_W), lambda i: (0, 0)),
                  pl.BlockSpec((1, GLA_K_W), lambda i: (0, 0))],
        out_specs=pl.BlockSpec((tm, GLA_K_W), lambda i: (i, 0)),
        compiler_params=_cparams(("parallel",)),
        name="gla_gate",
    )(h, w_glr, w_lr, b_lr.reshape(1, GLA_K_W))


def _gla_kernel(q_ref, k_ref, v_ref, r_ref, b_ref, g_ref, o_ref, st_ref, *, tc):
    @pl.when(pl.program_id(2) == 0)
    def _():
        st_ref[...] = jnp.zeros_like(st_ref)

    c = GLA_CHUNK
    sub = GLA_SUB
    causal = (lax.broadcasted_iota(jnp.int32, (c, c), 1)
              <= lax.broadcasted_iota(jnp.int32, (c, c), 0))
    g_out = g_ref[...]

    def chunk(ci, carry):
        base = pl.multiple_of(ci * c, c)
        rows = pl.ds(base, c)
        q = q_ref[rows, :].astype(F32) * (GLA_DK ** -0.5)
        k = k_ref[rows, :].astype(F32)
        v = v_ref[rows, :]
        b = b_ref[rows, :]
        b_end = b_ref[pl.ds(base + c - 1, 1), :]
        st = st_ref[...]
        inter = lax.dot_general((q * jnp.exp(b)).astype(BF16), st.astype(BF16),
                                (((1,), (1,)), ((), ())), preferred_element_type=F32)
        scores = []
        for si in range(c // sub):
            lo, hi = si * sub, (si + 1) * sub
            if si > 0:
                ref = b_ref[pl.ds(base + lo - 1, 1), :]
            else:
                ref = jnp.zeros((1, GLA_DK), F32)
            q_t = (q[lo:hi] * jnp.exp(b[lo:hi] - ref)).astype(BF16)
            k_t = (k * jnp.exp(jnp.minimum(ref - b, GLA_EXP_CAP))).astype(BF16)
            scores.append(lax.dot_general(q_t, k_t, (((1,), (1,)), ((), ())),
                                          preferred_element_type=F32))
        sc = jnp.where(causal, jnp.concatenate(scores, axis=0), 0.0)
        o = inter + jnp.dot(sc.astype(BF16), v, preferred_element_type=F32)
        k_e = (k * jnp.exp(b_end - b)).astype(BF16)
        v_t = v.astype(F32).T.astype(BF16)
        st_ref[...] = st * jnp.exp(b_end) + jnp.dot(v_t, k_e, preferred_element_type=F32)
        ms = jnp.mean(o * o, axis=-1, keepdims=True)
        o = o * lax.rsqrt(ms + EPS) * g_out
        r = r_ref[rows, :].astype(F32)
        o_ref[rows, :] = (o * (r * _sigmoid(r))).astype(o_ref.dtype)
        return carry

    lax.fori_loop(0, tc // c, chunk, 0)


def gla_mixer(z3, bcum3, g_out, tc=256):
    b, s, _ = z3.shape
    dk, dv = GLA_DK, GLA_DV
    return pl.pallas_call(
        functools.partial(_gla_kernel, tc=tc),
        out_shape=jax.ShapeDtypeStruct((b, s, GLA_V_W), BF16),
        grid=(b, GLA_HEADS, s // tc),
        in_specs=[pl.BlockSpec((None, tc, dk), lambda bi, h, i: (bi, i, COL_GQ // dk + h)),
                  pl.BlockSpec((None, tc, dk), lambda bi, h, i: (bi, i, COL_GK // dk + h)),
                  pl.BlockSpec((None, tc, dv), lambda bi, h, i: (bi, i, COL_GV // dv + h)),
                  pl.BlockSpec((None, tc, dv), lambda bi, h, i: (bi, i, COL_GR // dv + h)),
                  pl.BlockSpec((None, tc, dk), lambda bi, h, i: (bi, i, h)),
                  pl.BlockSpec((1, dv), lambda bi, h, i: (0, 0))],
        out_specs=pl.BlockSpec((None, tc, dv), lambda bi, h, i: (bi, i, h)),
        scratch_shapes=[pltpu.VMEM((dv, dk), F32)],
        compiler_params=_cparams(("parallel", "parallel", "arbitrary")),
        name="gla_mixer",
    )(z3, z3, z3, z3, bcum3, g_out.reshape(1, dv))


def _merge_kernel(ga_ref, gb_ref, gc_ref, yp_ref, ys_ref, yg_ref, wp_ref, ws_ref, wg_ref, o_ref):
    m = ga_ref[...].astype(F32) * jnp.dot(yp_ref[...], wp_ref[...], preferred_element_type=F32)
    m += gb_ref[...].astype(F32) * jnp.dot(ys_ref[...], ws_ref[...], preferred_element_type=F32)
    m += gc_ref[...].astype(F32) * jnp.dot(yg_ref[...], wg_ref[...], preferred_element_type=F32)
    o_ref[...] = m.astype(o_ref.dtype)


def merge_branches(gates, y_pool, y_sb, y_gla, w_p, w_s, w_g, tm=1024, tn=512):
    n = gates.shape[0]
    d = D_MODEL
    nj = d // tn
    gate_spec = lambda br: pl.BlockSpec((tm, tn), lambda i, j: (i, br * nj + j))
    y_spec = lambda w: pl.BlockSpec((tm, w), lambda i, j: (i, 0))
    w_spec = lambda w: pl.BlockSpec((w, tn), lambda i, j: (0, j))
    return pl.pallas_call(
        _merge_kernel,
        out_shape=jax.ShapeDtypeStruct((n, d), BF16),
        grid=(n // tm, nj),
        in_specs=[gate_spec(0), gate_spec(1), gate_spec(2),
                  y_spec(POOL_W), y_spec(SB_W), y_spec(GLA_V_W),
                  w_spec(POOL_W), w_spec(SB_W), w_spec(GLA_V_W)],
        out_specs=pl.BlockSpec((tm, tn), lambda i, j: (i, j)),
        compiler_params=_cparams(("parallel", "arbitrary")),
        name="merge_branches",
    )(gates, gates, gates, y_pool, y_sb, y_gla, w_p, w_s, w_g)


def _silu(a):
    return a * _sigmoid(a)


def _ffn_kernel(h_ref, w1_ref, w3_ref, w2_ref, x_ref, o_ref):
    @pl.when(pl.program_id(1) == 0)
    def _():
        o_ref[...] = x_ref[...]

    h = h_ref[...]
    a = jnp.dot(h, w1_ref[...], preferred_element_type=F32)
    b = jnp.dot(h, w3_ref[...], preferred_element_type=F32)
    g = (_silu(a) * b).astype(BF16)
    o_ref[...] += jnp.dot(g, w2_ref[...], preferred_element_type=F32)


def ffn_dense(h, w1, w3, w2, x, tm=512, tf=512):
    n, d = h.shape
    ff = w1.shape[1]
    return pl.pallas_call(
        _ffn_kernel,
        out_shape=jax.ShapeDtypeStruct((n, d), F32),
        grid=(n // tm, ff // tf),
        in_specs=[pl.BlockSpec((tm, d), lambda i, f: (i, 0)),
                  pl.BlockSpec((d, tf), lambda i, f: (0, f)),
                  pl.BlockSpec((d, tf), lambda i, f: (0, f)),
                  pl.BlockSpec((tf, d), lambda i, f: (f, 0)),
                  pl.BlockSpec((tm, d), lambda i, f: (i, 0))],
        out_specs=pl.BlockSpec((tm, d), lambda i, f: (i, 0)),
        compiler_params=_cparams(("parallel", "arbitrary")),
        name="ffn_dense",
    )(h, w1, w3, w2, x)


def _moe_dense_kernel(h_ref, comb_ref, w1_ref, w3_ref, w2_ref, x_ref, o_ref):
    e = pl.program_id(1)

    @pl.when((e == 0) & (pl.program_id(2) == 0))
    def _():
        o_ref[...] = x_ref[...]

    comb = comb_ref[...]
    lane = lax.broadcasted_iota(jnp.int32, comb.shape, 1)
    ce = jnp.sum(jnp.where(lane == e, comb, 0.0), axis=-1, keepdims=True)
    h = h_ref[...]
    a = jnp.dot(h, w1_ref[...], preferred_element_type=F32)
    b = jnp.dot(h, w3_ref[...], preferred_element_type=F32)
    g = (_silu(a) * b * ce).astype(BF16)
    o_ref[...] += jnp.dot(g, w2_ref[...], preferred_element_type=F32)


def moe_dense(h, comb, w1, w3, w2, x, tm=512, tf=512):
    n, d = h.shape
    ne, _, ff = w1.shape
    return pl.pallas_call(
        _moe_dense_kernel,
        out_shape=jax.ShapeDtypeStruct((n, d), F32),
        grid=(n // tm, ne, ff // tf),
        in_specs=[pl.BlockSpec((tm, d), lambda i, e, f: (i, 0)),
                  pl.BlockSpec((tm, LANES), lambda i, e, f: (i, 0)),
                  pl.BlockSpec((None, d, tf), lambda i, e, f: (e, 0, f)),
                  pl.BlockSpec((None, d, tf), lambda i, e, f: (e, 0, f)),
                  pl.BlockSpec((None, tf, d), lambda i, e, f: (e, f, 0)),
                  pl.BlockSpec((tm, d), lambda i, e, f: (i, 0))],
        out_specs=pl.BlockSpec((tm, d), lambda i, e, f: (i, 0)),
        compiler_params=_cparams(("parallel", "arbitrary", "arbitrary")),
        name="moe_dense",
    )(h, comb, w1, w3, w2, x)


def _ple_kernel(x_ref, xr_ref, p_ref, wg_ref, wp_ref, o_ref, xb_ref, pb_ref):
    @pl.when(pl.program_id(1) == 0)
    def _():
        xb_ref[...] = x_ref[...].astype(BF16)
        pb_ref[...] = p_ref[...].astype(BF16)

    gate = _sigmoid(jnp.dot(xb_ref[...], wg_ref[...], preferred_element_type=F32))
    proj = jnp.dot(pb_ref[...], wp_ref[...], preferred_element_type=F32)
    o_ref[...] = xr_ref[...] + gate * proj


def ple_update(x, p, w_gate, w_proj, tm=512, tn=512):
    n, d = x.shape
    pd = p.shape[1]
    return pl.pallas_call(
        _ple_kernel,
        out_shape=jax.ShapeDtypeStruct((n, d), F32),
        grid=(n // tm, d // tn),
        in_specs=[pl.BlockSpec((tm, d), lambda i, j: (i, 0)),
                  pl.BlockSpec((tm, tn), lambda i, j: (i, j)),
                  pl.BlockSpec((tm, pd), lambda i, j: (i, 0)),
                  pl.BlockSpec((d, tn), lambda i, j: (0, j)),
                  pl.BlockSpec((pd, tn), lambda i, j: (0, j))],
        out_specs=pl.BlockSpec((tm, tn), lambda i, j: (i, j)),
        scratch_shapes=[pltpu.VMEM((tm, d), BF16), pltpu.VMEM((tm, pd), BF16)],
        compiler_params=_cparams(("parallel", "arbitrary")),
        name="ple_update",
    )(x, x, p, w_gate, w_proj)


def _split_w_in(w_in_i):
    c_lr = COL_GR
    main = jnp.concatenate([w_in_i[:, :c_lr], w_in_i[:, c_lr + GLA_RANK:]], axis=1).astype(BF16)
    glr = jnp.zeros((D_MODEL, LANES), BF16).at[:, :GLA_RANK].set(
        w_in_i[:, c_lr:c_lr + GLA_RANK].astype(BF16))
    return main, glr


def kernel(x, p, g_mix, w_in, w_branch_gate, b_branch_gate, pool_w, pool_scale, sb_gq, sb_gk,
           gla_w_lr, gla_b_lr, gla_g_out, w_up_pool, w_up_sb, w_up_gla, w_o, g_ffn,
           ffn_w1, ffn_w3, ffn_w2, moe_router, moe_w1, moe_w3, moe_w2, ple_w_proj, ple_w_gate):
    bsz, seq, d = x.shape
    n = bsz * seq
    depth = w_in.shape[0]
    xf = x.reshape(n, d)
    for i in range(depth):
        w_main, w_glr = _split_w_in(w_in[i])
        w_lr_pad = jnp.zeros((LANES, GLA_K_W), F32).at[:GLA_RANK].set(gla_w_lr[i])

        h = rmsnorm(xf, g_mix[i])
        z = matmul(h, w_main, out_dtype=BF16, name="in_proj")
        gates = matmul(h, w_branch_gate[i].astype(BF16), out_dtype=BF16, epilogue="bias_sigmoid",
                       extra=b_branch_gate[i], name="branch_gates")
        bcum = gla_gate(h, w_glr, w_lr_pad, gla_b_lr[i])

        z3 = z.reshape(bsz, seq, Z_COLS)
        y_pool = pool_mixer(z3, pool_w[i].astype(BF16), pool_scale[i])
        y_sb = sb_attention(z3, sb_gq[i], sb_gk[i])
        y_gla = gla_mixer(z3, bcum.reshape(bsz, seq, GLA_K_W), gla_g_out[i])

        merged = merge_branches(gates, y_pool.reshape(n, POOL_W), y_sb.reshape(n, SB_W),
                                y_gla.reshape(n, GLA_V_W), w_up_pool[i].astype(BF16),
                                w_up_sb[i].astype(BF16), w_up_gla[i].astype(BF16))
        xf = matmul(merged, w_o[i].astype(BF16), out_dtype=F32, epilogue="residual", extra=xf,
                    name="out_proj")

        j = i // 2
        if i % 2 == 0:
            h2 = rmsnorm(xf, g_ffn[i])
            xf = ffn_dense(h2, ffn_w1[j].astype(BF16), ffn_w3[j].astype(BF16),
                           ffn_w2[j].astype(BF16), xf)
        else:
            h2, comb = rmsnorm_router(xf, g_ffn[i], moe_router[j])
            xf = moe_dense(h2, comb, moe_w1[j].astype(BF16), moe_w3[j].astype(BF16),
                           moe_w2[j].astype(BF16), xf)

        xf = ple_update(xf, p[i].reshape(n, PLE_DIM), ple_w_gate[i].astype(BF16),
                        ple_w_proj[i].astype(BF16))
    return xf.reshape(bsz, seq, d)
```

```python
import functools

import jax
import jax.numpy as jnp
from jax import lax
from jax.experimental import pallas as pl
from jax.experimental.pallas import tpu as pltpu

F32 = jnp.float32
BF16 = jnp.bfloat16

EPS = 1e-6
D_MODEL = 2048
PLE_DIM = 256
POOL_GROUPS = 4
POOL_GROUP_W = 256
POOL_W = POOL_GROUPS * POOL_GROUP_W
POOL_WINDOWS = (2, 4, 8, 16)
SB_HEADS = 8
SB_HEAD_DIM = 128
SB_W = SB_HEADS * SB_HEAD_DIM
GLA_HEADS = 4
GLA_DK = 128
GLA_DV = 256
GLA_K_W = GLA_HEADS * GLA_DK
GLA_V_W = GLA_HEADS * GLA_DV
GLA_RANK = 16
GLA_TAU = 16.0
D_FF = 5632
N_EXPERTS = 8
N_BRANCH = 3

LANES = 128
VMEM_LIMIT = 56 * 1024 * 1024

COL_POOL = 0
COL_SQ = COL_POOL + POOL_W
COL_SK = COL_SQ + SB_W
COL_SV = COL_SK + SB_W
COL_GQ = COL_SV + SB_W
COL_GK = COL_GQ + GLA_K_W
COL_GV = COL_GK + GLA_K_W
COL_GR = COL_GV + GLA_V_W
Z_COLS = COL_GR + GLA_V_W

GLA_CHUNK = 64
GLA_SUB = 16
GLA_EXP_CAP = 80.0
SB_F32_ZERO_LOG = -104.0


def _cparams(sem):
    return pltpu.CompilerParams(dimension_semantics=sem, vmem_limit_bytes=VMEM_LIMIT)


def _log_sigmoid(z):
    return jnp.minimum(z, 0.0) - jnp.log(1.0 + jnp.exp(-jnp.abs(z)))


def _sigmoid(z):
    return 1.0 / (1.0 + jnp.exp(-z))


def _split_bf16(x):
    hi = x.astype(BF16)
    lo = (x - hi.astype(F32)).astype(BF16)
    return hi, lo


def _rmsnorm_kernel(x_ref, g_ref, o_ref):
    x = x_ref[...]
    ms = jnp.mean(x * x, axis=-1, keepdims=True)
    o_ref[...] = (x * lax.rsqrt(ms + EPS) * g_ref[...]).astype(o_ref.dtype)


def rmsnorm(x, g, tm=512):
    n, d = x.shape
    return pl.pallas_call(
        _rmsnorm_kernel,
        out_shape=jax.ShapeDtypeStruct((n, d), BF16),
        grid=(n // tm,),
        in_specs=[pl.BlockSpec((tm, d), lambda i: (i, 0)),
                  pl.BlockSpec((1, d), lambda i: (0, 0))],
        out_specs=pl.BlockSpec((tm, d), lambda i: (i, 0)),
        compiler_params=_cparams(("parallel",)),
        name="rmsnorm",
    )(x, g.reshape(1, d))


def _rmsnorm_router_kernel(x_ref, g_ref, r_ref, o_ref, info_ref, cnt_ref, carry_ref, *, tm):
    @pl.when(pl.program_id(0) == 0)
    def _():
        carry_ref[...] = jnp.zeros_like(carry_ref)

    x = x_ref[...]
    ms = jnp.mean(x * x, axis=-1, keepdims=True)
    h = x * lax.rsqrt(ms + EPS) * g_ref[...]
    o_ref[...] = h.astype(o_ref.dtype)
    logits = jnp.dot(h, r_ref[...], preferred_element_type=F32, precision=lax.Precision.HIGHEST)
    lane = lax.broadcasted_iota(jnp.int32, logits.shape, 1).astype(F32)
    neg = jnp.float32(-jnp.inf)
    logits = jnp.where(lane < N_EXPERTS, logits, neg)
    m1 = jnp.max(logits, axis=-1, keepdims=True)
    i1 = jnp.min(jnp.where(logits == m1, lane, float(LANES)), axis=-1, keepdims=True)
    sel1 = lane == i1
    rest = jnp.where(sel1, neg, logits)
    m2 = jnp.max(rest, axis=-1, keepdims=True)
    i2 = jnp.min(jnp.where(rest == m2, lane, float(LANES)), axis=-1, keepdims=True)
    sel2 = lane == i2
    e2 = jnp.exp(m2 - m1)
    den = 1.0 + e2
    sel = (sel1 | sel2).astype(BF16)
    before = (lax.broadcasted_iota(jnp.int32, (tm, tm), 1)
              < lax.broadcasted_iota(jnp.int32, (tm, tm), 0)).astype(BF16)
    carry = carry_ref[...]
    rank = jnp.dot(before, sel, preferred_element_type=F32) + carry[0:1, :]
    r1 = jnp.sum(jnp.where(sel1, rank, 0.0), axis=-1, keepdims=True)
    r2 = jnp.sum(jnp.where(sel2, rank, 0.0), axis=-1, keepdims=True)
    fields = (i1, i2, 1.0 / den, e2 / den, r1, r2)
    info = jnp.zeros_like(logits)
    for li, val in enumerate(fields):
        info = jnp.where(lane == li, val, info)
    info_ref[...] = info
    carry = carry + jnp.sum(sel.astype(F32), axis=0, keepdims=True)
    carry_ref[...] = carry
    cnt_ref[...] = carry


INFO_E1, INFO_E2, INFO_W1, INFO_W2, INFO_R1, INFO_R2 = range(6)


def rmsnorm_router(x, g, router, tm=512):
    n, d = x.shape
    r_pad = jnp.zeros((d, LANES), F32).at[:, :N_EXPERTS].set(router)
    return pl.pallas_call(
        functools.partial(_rmsnorm_router_kernel, tm=tm),
        out_shape=(jax.ShapeDtypeStruct((n, d), BF16), jax.ShapeDtypeStruct((n, LANES), F32),
                   jax.ShapeDtypeStruct((8, LANES), F32)),
        grid=(n // tm,),
        in_specs=[pl.BlockSpec((tm, d), lambda i: (i, 0)),
                  pl.BlockSpec((1, d), lambda i: (0, 0)),
                  pl.BlockSpec((d, LANES), lambda i: (0, 0))],
        out_specs=(pl.BlockSpec((tm, d), lambda i: (i, 0)),
                   pl.BlockSpec((tm, LANES), lambda i: (i, 0)),
                   pl.BlockSpec((8, LANES), lambda i: (0, 0))),
        scratch_shapes=[pltpu.VMEM((8, LANES), F32)],
        compiler_params=_cparams(("arbitrary",)),
        name="rmsnorm_router",
    )(x, g.reshape(1, d), r_pad)


def _mm_kernel(a_ref, w_ref, *rest, epilogue):
    acc = jnp.dot(a_ref[...], w_ref[...], preferred_element_type=F32)
    if epilogue == "plain":
        (o_ref,) = rest
    elif epilogue == "bias_sigmoid":
        b_ref, o_ref = rest
        acc = _sigmoid(acc + b_ref[...])
    elif epilogue == "residual":
        r_ref, o_ref = rest
        acc = acc + r_ref[...]
    o_ref[...] = acc.astype(o_ref.dtype)


def matmul(a, w, *, out_dtype, epilogue="plain", extra=None, tm=1024, tn=512, name="matmul"):
    n, k = a.shape
    m = w.shape[1]
    in_specs = [pl.BlockSpec((tm, k), lambda i, j: (i, 0)),
                pl.BlockSpec((k, tn), lambda i, j: (0, j))]
    args = [a, w]
    if epilogue == "bias_sigmoid":
        in_specs.append(pl.BlockSpec((1, tn), lambda i, j: (0, j)))
        args.append(extra.reshape(1, m))
    elif epilogue == "residual":
        in_specs.append(pl.BlockSpec((tm, tn), lambda i, j: (i, j)))
        args.append(extra)
    return pl.pallas_call(
        functools.partial(_mm_kernel, epilogue=epilogue),
        out_shape=jax.ShapeDtypeStruct((n, m), out_dtype),
        grid=(n // tm, m // tn),
        in_specs=in_specs,
        out_specs=pl.BlockSpec((tm, tn), lambda i, j: (i, j)),
        compiler_params=_cparams(("parallel", "arbitrary")),
        name=name,
    )(*args)


POOL_HALO = 128


def _pool_kernel(u_ref, w_ref, s_ref, o_ref, prev_ref, *, t):
    sb = pl.program_id(1)

    @pl.when(sb == 0)
    def _():
        prev_ref[...] = jnp.zeros_like(prev_ref)

    row = lax.broadcasted_iota(jnp.int32, (t, t), 0)
    col = lax.broadcasted_iota(jnp.int32, (t, t), 1)
    prow = lax.broadcasted_iota(jnp.int32, (t, POOL_HALO), 0)
    pcol = lax.broadcasted_iota(jnp.int32, (t, POOL_HALO), 1) - POOL_HALO
    tg = sb * t + lax.broadcasted_iota(jnp.int32, (t, 1), 0)
    for gi, w in enumerate(POOL_WINDOWS):
        cs = slice(gi * POOL_GROUP_W, (gi + 1) * POOL_GROUP_W)
        u = u_ref[:, cs]
        band_cur = ((col <= row) & (col > row - w)).astype(BF16)
        band_prev = ((pcol > prow - w) & (pcol + sb * t >= 0)).astype(BF16)
        win = jnp.dot(band_cur, u, preferred_element_type=F32)
        win = win + jnp.dot(band_prev, prev_ref[:, cs], preferred_element_type=F32)
        count = jnp.minimum(tg + 1, w).astype(F32)
        pooled = win / count - u.astype(F32)
        mixed = jnp.dot(pooled.astype(BF16), w_ref[gi], preferred_element_type=F32)
        o_ref[:, cs] = (mixed * s_ref[:, cs]).astype(o_ref.dtype)
    prev_ref[...] = u_ref[t - POOL_HALO:, :]


def pool_mixer(z3, pool_w, scale, t=256):
    b, s, _ = z3.shape
    return pl.pallas_call(
        functools.partial(_pool_kernel, t=t),
        out_shape=jax.ShapeDtypeStruct((b, s, POOL_W), BF16),
        grid=(b, s // t),
        in_specs=[pl.BlockSpec((None, t, POOL_W), lambda bi, si: (bi, si, COL_POOL // POOL_W)),
                  pl.BlockSpec((POOL_GROUPS, POOL_GROUP_W, POOL_GROUP_W), lambda bi, si: (0, 0, 0)),
                  pl.BlockSpec((1, POOL_W), lambda bi, si: (0, 0))],
        out_specs=pl.BlockSpec((None, t, POOL_W), lambda bi, si: (bi, si, 0)),
        scratch_shapes=[pltpu.VMEM((POOL_HALO, POOL_W), BF16)],
        compiler_params=_cparams(("parallel", "arbitrary")),
        name="pool_mixer",
    )(z3, pool_w, scale.reshape(1, POOL_W))


def _head_rmsnorm(x, g):
    ms = jnp.mean(x * x, axis=-1, keepdims=True)
    return x * lax.rsqrt(ms + EPS) * g


def _sb_kernel(q_ref, k_ref, v_ref, gq_ref, gk_ref, o_ref, acc_ref, c_ref, *, tq, tk):
    i = pl.program_id(2)
    q = _head_rmsnorm(q_ref[...].astype(F32), gq_ref[...]) * (SB_HEAD_DIM ** -0.5)
    q = q.astype(BF16)
    acc_ref[...] = jnp.zeros_like(acc_ref)
    c_ref[...] = jnp.zeros_like(c_ref)
    t_pos = i * tq + lax.broadcasted_iota(jnp.int32, (tq, tk), 0)
    s_off = lax.broadcasted_iota(jnp.int32, (tq, tk), 1)
    suffix = (lax.broadcasted_iota(jnp.int32, (tk, tk), 0)
              > lax.broadcasted_iota(jnp.int32, (tk, tk), 1)).astype(BF16)
    gk = gk_ref[...]

    def body(carry):
        j, _ = carry
        start = pl.multiple_of(j * tk, tk)
        k = _head_rmsnorm(k_ref[pl.ds(start, tk), :].astype(F32), gk).astype(BF16)
        z = lax.dot_general(q, k, (((1,), (1,)), ((), ())), preferred_element_type=F32)
        mask = (j * tk + s_off) < t_pos
        soft = jnp.log(1.0 + jnp.exp(-jnp.abs(z)))
        log_beta = jnp.minimum(z, 0.0) - soft
        log_keep = jnp.where(mask, jnp.minimum(-z, 0.0) - soft, 0.0)
        hi, lo = _split_bf16(log_keep)
        later = (jnp.dot(hi, suffix, preferred_element_type=F32)
                 + jnp.dot(lo, suffix, preferred_element_type=F32))
        c = c_ref[...]
        a = jnp.where(mask, jnp.exp(log_beta + later + c), 0.0)
        acc_ref[...] += jnp.dot(a.astype(BF16), v_ref[pl.ds(start, tk), :],
                                preferred_element_type=F32)
        c_new = c + later[:, 0:1] + log_keep[:, 0:1]
        c_ref[...] = c_new
        return j - 1, (jnp.max(c_new) > SB_F32_ZERO_LOG).astype(jnp.int32)

    def cond(carry):
        j, live = carry
        return jnp.logical_and(j >= 0, live > 0)

    j0 = ((i + 1) * tq) // tk - 1
    lax.while_loop(cond, body, (j0, jnp.int32(1)))
    o_ref[...] = acc_ref[...].astype(o_ref.dtype)


def sb_attention(z3, gq, gk, tq=256, tk=256):
    b, s, _ = z3.shape
    hd = SB_HEAD_DIM
    return pl.pallas_call(
        functools.partial(_sb_kernel, tq=tq, tk=tk),
        out_shape=jax.ShapeDtypeStruct((b, s, SB_W), BF16),
        grid=(b, SB_HEADS, s // tq),
        in_specs=[pl.BlockSpec((None, tq, hd), lambda bi, h, i: (bi, i, COL_SQ // hd + h)),
                  pl.BlockSpec((None, s, hd), lambda bi, h, i: (bi, 0, COL_SK // hd + h)),
                  pl.BlockSpec((None, s, hd), lambda bi, h, i: (bi, 0, COL_SV // hd + h)),
                  pl.BlockSpec((1, hd), lambda bi, h, i: (0, 0)),
                  pl.BlockSpec((1, hd), lambda bi, h, i: (0, 0))],
        out_specs=pl.BlockSpec((None, tq, hd), lambda bi, h, i: (bi, i, h)),
        scratch_shapes=[pltpu.VMEM((tq, hd), F32), pltpu.VMEM((tq, 1), F32)],
        compiler_params=_cparams(("parallel", "parallel", "arbitrary")),
        name="sb_attention",
    )(z3, z3, z3, gq.reshape(1, hd), gk.reshape(1, hd))


def _gla_gate_kernel(h_ref, wg_ref, wlr_ref, blr_ref, o_ref, *, tm):
    g = jnp.dot(h_ref[...], wg_ref[...], preferred_element_type=F32)
    pre = jnp.dot(g, wlr_ref[...], preferred_element_type=F32,
                  precision=lax.Precision.HIGHEST) + blr_ref[...]
    log_a = _log_sigmoid(pre) * (1.0 / GLA_TAU)
    row = lax.broadcasted_iota(jnp.int32, (tm, tm), 0)
    col = lax.broadcasted_iota(jnp.int32, (tm, tm), 1)
    shift = GLA_CHUNK.bit_length() - 1
    tri = ((col <= row) & ((col >> shift) == (row >> shift))).astype(BF16)
    hi, lo = _split_bf16(log_a)
    o_ref[...] = (jnp.dot(tri, hi, preferred_element_type=F32)
                  + jnp.dot(tri, lo, preferred_element_type=F32))


def gla_gate(h, w_glr, w_lr, b_lr, tm=256):
    n, d = h.shape
    return pl.pallas_call(
        functools.partial(_gla_gate_kernel, tm=tm),
        out_shape=jax.ShapeDtypeStruct((n, GLA_K_W), F32),
        grid=(n // tm,),
        in_specs=[pl.BlockSpec((tm, d), lambda i: (i, 0)),
                  pl.BlockSpec((d, LANES), lambda i: (0, 0)),
                  pl.BlockSpec((LANES, GLA_K_W), lambda i: (0, 0)),
                  pl.BlockSpec((1, GLA_K_W), lambda i: (0, 0))],
        out_specs=pl.BlockSpec((tm, GLA_K_W), lambda i: (i, 0)),
        compiler_params=_cparams(("parallel",)),
        name="gla_gate",
    )(h, w_glr, w_lr, b_lr.reshape(1, GLA_K_W))


def _gla_kernel(q_ref, k_ref, v_ref, r_ref, b_ref, g_ref, o_ref, st_ref, *, tc):
    @pl.when(pl.program_id(2) == 0)
    def _():
        st_ref[...] = jnp.zeros_like(st_ref)

    c = GLA_CHUNK
    sub = GLA_SUB
    causal = (lax.broadcasted_iota(jnp.int32, (c, c), 1)
              <= lax.broadcasted_iota(jnp.int32, (c, c), 0))
    g_out = g_ref[...]

    def chunk(ci, carry):
        base = pl.multiple_of(ci * c, c)
        rows = pl.ds(base, c)
        q = q_ref[rows, :].astype(F32) * (GLA_DK ** -0.5)
        k = k_ref[rows, :].astype(F32)
        v = v_ref[rows, :]
        b = b_ref[rows, :]
        b_end = b_ref[pl.ds(base + c - 1, 1), :]
        st = st_ref[...]
        inter = lax.dot_general((q * jnp.exp(b)).astype(BF16), st.astype(BF16),
                                (((1,), (1,)), ((), ())), preferred_element_type=F32)
        scores = []
        for si in range(c // sub):
            lo, hi = si * sub, (si + 1) * sub
            if si > 0:
                ref = b_ref[pl.ds(base + lo - 1, 1), :]
            else:
                ref = jnp.zeros((1, GLA_DK), F32)
            q_t = (q[lo:hi] * jnp.exp(b[lo:hi] - ref)).astype(BF16)
            k_t = (k * jnp.exp(jnp.minimum(ref - b, GLA_EXP_CAP))).astype(BF16)
            scores.append(lax.dot_general(q_t, k_t, (((1,), (1,)), ((), ())),
                                          preferred_element_type=F32))
        sc = jnp.where(causal, jnp.concatenate(scores, axis=0), 0.0)
        o = inter + jnp.dot(sc.astype(BF16), v, preferred_element_type=F32)
        k_e = (k * jnp.exp(b_end - b)).astype(BF16)
        v_t = v.astype(F32).T.astype(BF16)
        st_ref[...] = st * jnp.exp(b_end) + jnp.dot(v_t, k_e, preferred_element_type=F32)
        ms = jnp.mean(o * o, axis=-1, keepdims=True)
        o = o * lax.rsqrt(ms + EPS) * g_out
        r = r_ref[rows, :].astype(F32)
        o_ref[rows, :] = (o * (r * _sigmoid(r))).astype(o_ref.dtype)
        return carry

    lax.fori_loop(0, tc // c, chunk, 0)


def gla_mixer(z3, bcum3, g_out, tc=256):
    b, s, _ = z3.shape
    dk, dv = GLA_DK, GLA_DV
    return pl.pallas_call(
        functools.partial(_gla_kernel, tc=tc),
        out_shape=jax.ShapeDtypeStruct((b, s, GLA_V_W), BF16),
        grid=(b, GLA_HEADS, s // tc),
        in_specs=[pl.BlockSpec((None, tc, dk), lambda bi, h, i: (bi, i, COL_GQ // dk + h)),
                  pl.BlockSpec((None, tc, dk), lambda bi, h, i: (bi, i, COL_GK // dk + h)),
                  pl.BlockSpec((None, tc, dv), lambda bi, h, i: (bi, i, COL_GV // dv + h)),
                  pl.BlockSpec((None, tc, dv), lambda bi, h, i: (bi, i, COL_GR // dv + h)),
                  pl.BlockSpec((None, tc, dk), lambda bi, h, i: (bi, i, h)),
                  pl.BlockSpec((1, dv), lambda bi, h, i: (0, 0))],
        out_specs=pl.BlockSpec((None, tc, dv), lambda bi, h, i: (bi, i, h)),
        scratch_shapes=[pltpu.VMEM((dv, dk), F32)],
        compiler_params=_cparams(("parallel", "parallel", "arbitrary")),
        name="gla_mixer",
    )(z3, z3, z3, z3, bcum3, g_out.reshape(1, dv))


def _merge_kernel(ga_ref, gb_ref, gc_ref, yp_ref, ys_ref, yg_ref, wp_ref, ws_ref, wg_ref, o_ref):
    m = ga_ref[...].astype(F32) * jnp.dot(yp_ref[...], wp_ref[...], preferred_element_type=F32)
    m += gb_ref[...].astype(F32) * jnp.dot(ys_ref[...], ws_ref[...], preferred_element_type=F32)
    m += gc_ref[...].astype(F32) * jnp.dot(yg_ref[...], wg_ref[...], preferred_element_type=F32)
    o_ref[...] = m.astype(o_ref.dtype)


def merge_branches(gates, y_pool, y_sb, y_gla, w_p, w_s, w_g, tm=1024, tn=512):
    n = gates.shape[0]
    d = D_MODEL
    nj = d // tn
    gate_spec = lambda br: pl.BlockSpec((tm, tn), lambda i, j: (i, br * nj + j))
    y_spec = lambda w: pl.BlockSpec((tm, w), lambda i, j: (i, 0))
    w_spec = lambda w: pl.BlockSpec((w, tn), lambda i, j: (0, j))
    return pl.pallas_call(
        _merge_kernel,
        out_shape=jax.ShapeDtypeStruct((n, d), BF16),
        grid=(n // tm, nj),
        in_specs=[gate_spec(0), gate_spec(1), gate_spec(2),
                  y_spec(POOL_W), y_spec(SB_W), y_spec(GLA_V_W),
                  w_spec(POOL_W), w_spec(SB_W), w_spec(GLA_V_W)],
        out_specs=pl.BlockSpec((tm, tn), lambda i, j: (i, j)),
        compiler_params=_cparams(("parallel", "arbitrary")),
        name="merge_branches",
    )(gates, gates, gates, y_pool, y_sb, y_gla, w_p, w_s, w_g)


def _silu(a):
    return a * _sigmoid(a)


def _ffn_kernel(h_ref, w1_ref, w3_ref, w2_ref, x_ref, o_ref):
    @pl.when(pl.program_id(1) == 0)
    def _():
        o_ref[...] = x_ref[...]

    h = h_ref[...]
    a = jnp.dot(h, w1_ref[...], preferred_element_type=F32)
    b = jnp.dot(h, w3_ref[...], preferred_element_type=F32)
    g = (_silu(a) * b).astype(BF16)
    o_ref[...] += jnp.dot(g, w2_ref[...], preferred_element_type=F32)


def ffn_dense(h, w1, w3, w2, x, tm=512, tf=512):
    n, d = h.shape
    ff = w1.shape[1]
    return pl.pallas_call(
        _ffn_kernel,
        out_shape=jax.ShapeDtypeStruct((n, d), F32),
        grid=(n // tm, ff // tf),
        in_specs=[pl.BlockSpec((tm, d), lambda i, f: (i, 0)),
                  pl.BlockSpec((d, tf), lambda i, f: (0, f)),
                  pl.BlockSpec((d, tf), lambda i, f: (0, f)),
                  pl.BlockSpec((tf, d), lambda i, f: (f, 0)),
                  pl.BlockSpec((tm, d), lambda i, f: (i, 0))],
        out_specs=pl.BlockSpec((tm, d), lambda i, f: (i, 0)),
        compiler_params=_cparams(("parallel", "arbitrary")),
        name="ffn_dense",
    )(h, w1, w3, w2, x)


ROW_SLABS = D_MODEL // LANES
MOE_TM = 512
TOP_K = 2


def _row_copy(src_ref, src_idx, dst_ref, dst_idx, sem):
    return pltpu.make_async_copy(src_ref.at[src_idx], dst_ref.at[dst_idx], sem)


def _dispatch_kernel(dest_ref, h_ref, init_ref, o_ref, sem, *, tm):
    del init_ref
    base = pl.program_id(0) * tm

    def start(t, carry):
        for kk in range(TOP_K):
            _row_copy(h_ref, t, o_ref, dest_ref[TOP_K * (base + t) + kk], sem).start()
        return carry

    def wait(t, carry):
        for kk in range(TOP_K):
            _row_copy(h_ref, t, o_ref, dest_ref[TOP_K * (base + t) + kk], sem).wait()
        return carry

    lax.fori_loop(0, tm, start, 0)
    lax.fori_loop(0, tm, wait, 0)


def moe_dispatch(dest, h_rows, n_rows, tm=256):
    n = h_rows.shape[0]
    grid_spec = pltpu.PrefetchScalarGridSpec(
        num_scalar_prefetch=1,
        grid=(n // tm,),
        in_specs=[pl.BlockSpec((tm, ROW_SLABS, LANES), lambda i, dest: (i, 0, 0)),
                  pl.BlockSpec(memory_space=pl.ANY)],
        out_specs=pl.BlockSpec(memory_space=pl.ANY),
        scratch_shapes=[pltpu.SemaphoreType.DMA],
    )
    return pl.pallas_call(
        functools.partial(_dispatch_kernel, tm=tm),
        out_shape=jax.ShapeDtypeStruct((n_rows, ROW_SLABS, LANES), BF16),
        grid_spec=grid_spec,
        input_output_aliases={2: 0},
        compiler_params=_cparams(("arbitrary",)),
        name="moe_dispatch",
    )(dest, h_rows, jnp.zeros((n_rows, ROW_SLABS, LANES), BF16))


def _moe_ffn_kernel(te_ref, nu_ref, h_ref, w1_ref, w3_ref, w2_ref, o_ref, acc_ref):
    del te_ref
    f = pl.program_id(1)
    last = pl.num_programs(1) - 1
    used = pl.program_id(0) < nu_ref[0]

    @pl.when(used)
    def _():
        h = h_ref[...]
        a = jnp.dot(h, w1_ref[...], preferred_element_type=F32)
        b = jnp.dot(h, w3_ref[...], preferred_element_type=F32)
        g = (_silu(a) * b).astype(BF16)
        y = jnp.dot(g, w2_ref[...], preferred_element_type=F32)

        @pl.when(f == 0)
        def _():
            acc_ref[...] = y

        @pl.when(f > 0)
        def _():
            acc_ref[...] += y

        @pl.when(f == last)
        def _():
            o_ref[...] = acc_ref[...].astype(o_ref.dtype)

    @pl.when(jnp.logical_not(used) & (f == last))
    def _():
        o_ref[...] = jnp.zeros_like(o_ref)


def moe_grouped_ffn(tile_expert, n_used, h_sorted, w1, w3, w2, tm=MOE_TM, tf=512):
    r, d = h_sorted.shape
    ff = w1.shape[2]
    nf = ff // tf

    def f_idx(i, f, nu):
        return jnp.where(i < nu[0], f, nf - 1)

    grid_spec = pltpu.PrefetchScalarGridSpec(
        num_scalar_prefetch=2,
        grid=(r // tm, nf),
        in_specs=[pl.BlockSpec((tm, d), lambda i, f, te, nu: (i, 0)),
                  pl.BlockSpec((None, d, tf), lambda i, f, te, nu: (te[i], 0, f_idx(i, f, nu))),
                  pl.BlockSpec((None, d, tf), lambda i, f, te, nu: (te[i], 0, f_idx(i, f, nu))),
                  pl.BlockSpec((None, tf, d), lambda i, f, te, nu: (te[i], f_idx(i, f, nu), 0))],
        out_specs=pl.BlockSpec((tm, d), lambda i, f, te, nu: (i, 0)),
        scratch_shapes=[pltpu.VMEM((tm, d), F32)],
    )
    return pl.pallas_call(
        _moe_ffn_kernel,
        out_shape=jax.ShapeDtypeStruct((r, d), BF16),
        grid_spec=grid_spec,
        compiler_params=_cparams(("arbitrary", "arbitrary")),
        name="moe_grouped_ffn",
    )(tile_expert, n_used, h_sorted, w1, w3, w2)


def _combine_kernel(dest_ref, w_ref, y_ref, o_ref, buf_ref, sems, *, tm):
    base = pl.program_id(0) * tm

    def start(t, carry):
        for kk in range(TOP_K):
            _row_copy(y_ref, dest_ref[TOP_K * (base + t) + kk], buf_ref.at[kk], t,
                      sems.at[kk]).start()
        return carry

    def wait(t, carry):
        for kk in range(TOP_K):
            _row_copy(y_ref, dest_ref[TOP_K * (base + t) + kk], buf_ref.at[kk], t,
                      sems.at[kk]).wait()
        return carry

    lax.fori_loop(0, tm, start, 0)
    lax.fori_loop(0, tm, wait, 0)
    acc = buf_ref[0].astype(F32) * w_ref[0]
    for kk in range(1, TOP_K):
        acc += buf_ref[kk].astype(F32) * w_ref[kk]
    o_ref[...] = acc


def moe_combine(dest, w_rows, y_rows, n, tm=256):
    grid_spec = pltpu.PrefetchScalarGridSpec(
        num_scalar_prefetch=1,
        grid=(n // tm,),
        in_specs=[pl.BlockSpec((TOP_K, tm, 1, LANES), lambda i, dest: (0, i, 0, 0)),
                  pl.BlockSpec(memory_space=pl.ANY)],
        out_specs=pl.BlockSpec((tm, ROW_SLABS, LANES), lambda i, dest: (i, 0, 0)),
        scratch_shapes=[pltpu.VMEM((TOP_K, tm, ROW_SLABS, LANES), BF16),
                        pltpu.SemaphoreType.DMA((TOP_K,))],
    )
    return pl.pallas_call(
        functools.partial(_combine_kernel, tm=tm),
        out_shape=jax.ShapeDtypeStruct((n, ROW_SLABS, LANES), F32),
        grid_spec=grid_spec,
        compiler_params=_cparams(("arbitrary",)),
        name="moe_combine",
    )(dest, w_rows, y_rows)


def moe_sparse(h2, info, counts, w1, w3, w2):
    n, d = h2.shape
    tm = MOE_TM
    n_tiles = (TOP_K * n) // tm + N_EXPERTS
    n_rows = n_tiles * tm
    cnt = counts[0, :N_EXPERTS].astype(jnp.int32)
    tiles_per = (cnt + tm - 1) // tm
    tile_end = jnp.cumsum(tiles_per)
    group_start = (tile_end - tiles_per) * tm
    n_used = tile_end[-1:]
    tile_expert = jnp.minimum(
        jnp.searchsorted(tile_end, jnp.arange(n_tiles, dtype=jnp.int32), side="right"),
        N_EXPERTS - 1).astype(jnp.int32)
    tile_expert = jnp.where(jnp.arange(n_tiles) < n_used[0], tile_expert,
                            tile_expert[jnp.maximum(n_used[0] - 1, 0)])
    experts = info[:, INFO_E1:INFO_E2 + 1].astype(jnp.int32)
    ranks = info[:, INFO_R1:INFO_R2 + 1].astype(jnp.int32)
    dest = (group_start[experts] + ranks).reshape(TOP_K * n)
    w_rows = jnp.broadcast_to(info[:, INFO_W1:INFO_W2 + 1].T[:, :, None, None],
                              (TOP_K, n, 1, LANES))

    h_sorted = moe_dispatch(dest, h2.reshape(n, ROW_SLABS, LANES), n_rows)
    y_sorted = moe_grouped_ffn(tile_expert, n_used, h_sorted.reshape(n_rows, d), w1, w3, w2)
    y = moe_combine(dest, w_rows, y_sorted.reshape(n_rows, ROW_SLABS, LANES), n)
    return y.reshape(n, d)


def _ple_kernel(*refs, n_terms):
    full_refs, tile_refs = refs[:n_terms], refs[n_terms:2 * n_terms]
    p_ref, wg_ref, wp_ref, o_ref, xb_ref, pb_ref = refs[2 * n_terms:]

    @pl.when(pl.program_id(1) == 0)
    def _():
        x = full_refs[0][...]
        for r in full_refs[1:]:
            x = x + r[...]
        xb_ref[...] = x.astype(BF16)
        pb_ref[...] = p_ref[...].astype(BF16)

    gate = _sigmoid(jnp.dot(xb_ref[...], wg_ref[...], preferred_element_type=F32))
    proj = jnp.dot(pb_ref[...], wp_ref[...], preferred_element_type=F32)
    x = tile_refs[0][...]
    for r in tile_refs[1:]:
        x = x + r[...]
    o_ref[...] = x + gate * proj


def ple_update(terms, p, w_gate, w_proj, tm=512, tn=512):
    n, d = terms[0].shape
    pd = p.shape[1]
    nt = len(terms)
    return pl.pallas_call(
        functools.partial(_ple_kernel, n_terms=nt),
        out_shape=jax.ShapeDtypeStruct((n, d), F32),
        grid=(n // tm, d // tn),
        in_specs=([pl.BlockSpec((tm, d), lambda i, j: (i, 0))] * nt
                  + [pl.BlockSpec((tm, tn), lambda i, j: (i, j))] * nt
                  + [pl.BlockSpec((tm, pd), lambda i, j: (i, 0)),
                     pl.BlockSpec((d, tn), lambda i, j: (0, j)),
                     pl.BlockSpec((pd, tn), lambda i, j: (0, j))]),
        out_specs=pl.BlockSpec((tm, tn), lambda i, j: (i, j)),
        scratch_shapes=[pltpu.VMEM((tm, d), BF16), pltpu.VMEM((tm, pd), BF16)],
        compiler_params=_cparams(("parallel", "arbitrary")),
        name="ple_update",
    )(*terms, *terms, p, w_gate, w_proj)


def _split_w_in(w_in_i):
    c_lr = COL_GR
    main = jnp.concatenate([w_in_i[:, :c_lr], w_in_i[:, c_lr + GLA_RANK:]], axis=1).astype(BF16)
    glr = jnp.zeros((D_MODEL, LANES), BF16).at[:, :GLA_RANK].set(
        w_in_i[:, c_lr:c_lr + GLA_RANK].astype(BF16))
    return main, glr


def kernel(x, p, g_mix, w_in, w_branch_gate, b_branch_gate, pool_w, pool_scale, sb_gq, sb_gk,
           gla_w_lr, gla_b_lr, gla_g_out, w_up_pool, w_up_sb, w_up_gla, w_o, g_ffn,
           ffn_w1, ffn_w3, ffn_w2, moe_router, moe_w1, moe_w3, moe_w2, ple_w_proj, ple_w_gate):
    bsz, seq, d = x.shape
    n = bsz * seq
    depth = w_in.shape[0]
    xf = x.reshape(n, d)
    for i in range(depth):
        w_main, w_glr = _split_w_in(w_in[i])
        w_lr_pad = jnp.zeros((LANES, GLA_K_W), F32).at[:GLA_RANK].set(gla_w_lr[i])

        h = rmsnorm(xf, g_mix[i])
        z = matmul(h, w_main, out_dtype=BF16, name="in_proj")
        gates = matmul(h, w_branch_gate[i].astype(BF16), out_dtype=BF16, epilogue="bias_sigmoid",
                       extra=b_branch_gate[i], name="branch_gates")
        bcum = gla_gate(h, w_glr, w_lr_pad, gla_b_lr[i])

        z3 = z.reshape(bsz, seq, Z_COLS)
        y_pool = pool_mixer(z3, pool_w[i].astype(BF16), pool_scale[i])
        y_sb = sb_attention(z3, sb_gq[i], sb_gk[i])
        y_gla = gla_mixer(z3, bcum.reshape(bsz, seq, GLA_K_W), gla_g_out[i])

        merged = merge_branches(gates, y_pool.reshape(n, POOL_W), y_sb.reshape(n, SB_W),
                                y_gla.reshape(n, GLA_V_W), w_up_pool[i].astype(BF16),
                                w_up_sb[i].astype(BF16), w_up_gla[i].astype(BF16))
        xf = matmul(merged, w_o[i].astype(BF16), out_dtype=F32, epilogue="residual", extra=xf,
                    name="out_proj")

        j = i // 2
        if i % 2 == 0:
            h2 = rmsnorm(xf, g_ffn[i])
            terms = [ffn_dense(h2, ffn_w1[j].astype(BF16), ffn_w3[j].astype(BF16),
                               ffn_w2[j].astype(BF16), xf)]
        else:
            h2, info, counts = rmsnorm_router(xf, g_ffn[i], moe_router[j])
            terms = [xf, moe_sparse(h2, info, counts, moe_w1[j].astype(BF16),
                                    moe_w3[j].astype(BF16), moe_w2[j].astype(BF16))]

        xf = ple_update(terms, p[i].reshape(n, PLE_DIM), ple_w_gate[i].astype(BF16),
                        ple_w_proj[i].astype(BF16))
    return xf.reshape(bsz, seq, d)
```

```python
import functools

import jax
import jax.numpy as jnp
from jax import lax
from jax.experimental import pallas as pl
from jax.experimental.pallas import tpu as pltpu

F32 = jnp.float32
BF16 = jnp.bfloat16

EPS = 1e-6
D_MODEL = 2048
PLE_DIM = 256
POOL_GROUPS = 4
POOL_GROUP_W = 256
POOL_W = POOL_GROUPS * POOL_GROUP_W
POOL_WINDOWS = (2, 4, 8, 16)
SB_HEADS = 8
SB_HEAD_DIM = 128
SB_W = SB_HEADS * SB_HEAD_DIM
GLA_HEADS = 4
GLA_DK = 128
GLA_DV = 256
GLA_K_W = GLA_HEADS * GLA_DK
GLA_V_W = GLA_HEADS * GLA_DV
GLA_RANK = 16
GLA_TAU = 16.0
D_FF = 5632
N_EXPERTS = 8
N_BRANCH = 3

LANES = 128
VMEM_LIMIT = 56 * 1024 * 1024

COL_POOL = 0
COL_SQ = COL_POOL + POOL_W
COL_SK = COL_SQ + SB_W
COL_SV = COL_SK + SB_W
COL_GQ = COL_SV + SB_W
COL_GK = COL_GQ + GLA_K_W
COL_GV = COL_GK + GLA_K_W
COL_GR = COL_GV + GLA_V_W
Z_COLS = COL_GR + GLA_V_W

GLA_CHUNK = 64
GLA_SUB = 16
GLA_EXP_CAP = 80.0
SB_F32_ZERO_LOG = -104.0


def _cparams(sem):
    return pltpu.CompilerParams(dimension_semantics=sem, vmem_limit_bytes=VMEM_LIMIT)


def _log_sigmoid(z):
    return jnp.minimum(z, 0.0) - jnp.log(1.0 + jnp.exp(-jnp.abs(z)))


def _sigmoid(z):
    return 1.0 / (1.0 + jnp.exp(-z))


def _split_bf16(x):
    hi = x.astype(BF16)
    lo = (x - hi.astype(F32)).astype(BF16)
    return hi, lo


def _rmsnorm_kernel(x_ref, g_ref, o_ref):
    x = x_ref[...]
    ms = jnp.mean(x * x, axis=-1, keepdims=True)
    o_ref[...] = (x * lax.rsqrt(ms + EPS) * g_ref[...]).astype(o_ref.dtype)


def rmsnorm(x, g, tm=512):
    n, d = x.shape
    return pl.pallas_call(
        _rmsnorm_kernel,
        out_shape=jax.ShapeDtypeStruct((n, d), BF16),
        grid=(n // tm,),
        in_specs=[pl.BlockSpec((tm, d), lambda i: (i, 0)),
                  pl.BlockSpec((1, d), lambda i: (0, 0))],
        out_specs=pl.BlockSpec((tm, d), lambda i: (i, 0)),
        compiler_params=_cparams(("parallel",)),
        name="rmsnorm",
    )(x, g.reshape(1, d))


def _rmsnorm_router_kernel(x_ref, g_ref, r_ref, o_ref, info_ref, cnt_ref, carry_ref, *, tm):
    @pl.when(pl.program_id(0) == 0)
    def _():
        carry_ref[...] = jnp.zeros_like(carry_ref)

    x = x_ref[...]
    ms = jnp.mean(x * x, axis=-1, keepdims=True)
    h = x * lax.rsqrt(ms + EPS) * g_ref[...]
    o_ref[...] = h.astype(o_ref.dtype)
    logits = jnp.dot(h, r_ref[...], preferred_element_type=F32, precision=lax.Precision.HIGHEST)
    lane = lax.broadcasted_iota(jnp.int32, logits.shape, 1).astype(F32)
    neg = jnp.float32(-jnp.inf)
    logits = jnp.where(lane < N_EXPERTS, logits, neg)
    m1 = jnp.max(logits, axis=-1, keepdims=True)
    i1 = jnp.min(jnp.where(logits == m1, lane, float(LANES)), axis=-1, keepdims=True)
    sel1 = lane == i1
    rest = jnp.where(sel1, neg, logits)
    m2 = jnp.max(rest, axis=-1, keepdims=True)
    i2 = jnp.min(jnp.where(rest == m2, lane, float(LANES)), axis=-1, keepdims=True)
    sel2 = lane == i2
    e2 = jnp.exp(m2 - m1)
    den = 1.0 + e2
    sel = (sel1 | sel2).astype(BF16)
    before = (lax.broadcasted_iota(jnp.int32, (tm, tm), 1)
              < lax.broadcasted_iota(jnp.int32, (tm, tm), 0)).astype(BF16)
    carry = carry_ref[...]
    rank = jnp.dot(before, sel, preferred_element_type=F32) + carry[0:1, :]
    r1 = jnp.sum(jnp.where(sel1, rank, 0.0), axis=-1, keepdims=True)
    r2 = jnp.sum(jnp.where(sel2, rank, 0.0), axis=-1, keepdims=True)
    fields = (i1, i2, 1.0 / den, e2 / den, r1, r2)
    info = jnp.zeros_like(logits)
    for li, val in enumerate(fields):
        info = jnp.where(lane == li, val, info)
    info_ref[...] = info
    carry = carry + jnp.sum(sel.astype(F32), axis=0, keepdims=True)
    carry_ref[...] = carry
    cnt_ref[...] = carry


INFO_E1, INFO_E2, INFO_W1, INFO_W2, INFO_R1, INFO_R2 = range(6)


def rmsnorm_router(x, g, router, tm=512):
    n, d = x.shape
    r_pad = jnp.zeros((d, LANES), F32).at[:, :N_EXPERTS].set(router)
    return pl.pallas_call(
        functools.partial(_rmsnorm_router_kernel, tm=tm),
        out_shape=(jax.ShapeDtypeStruct((n, d), BF16), jax.ShapeDtypeStruct((n, LANES), F32),
                   jax.ShapeDtypeStruct((8, LANES), F32)),
        grid=(n // tm,),
        in_specs=[pl.BlockSpec((tm, d), lambda i: (i, 0)),
                  pl.BlockSpec((1, d), lambda i: (0, 0)),
                  pl.BlockSpec((d, LANES), lambda i: (0, 0))],
        out_specs=(pl.BlockSpec((tm, d), lambda i: (i, 0)),
                   pl.BlockSpec((tm, LANES), lambda i: (i, 0)),
                   pl.BlockSpec((8, LANES), lambda i: (0, 0))),
        scratch_shapes=[pltpu.VMEM((8, LANES), F32)],
        compiler_params=_cparams(("arbitrary",)),
        name="rmsnorm_router",
    )(x, g.reshape(1, d), r_pad)


def _mm_kernel(a_ref, w_ref, *rest, epilogue):
    acc = jnp.dot(a_ref[...], w_ref[...], preferred_element_type=F32)
    if epilogue == "plain":
        (o_ref,) = rest
    elif epilogue == "bias_sigmoid":
        b_ref, o_ref = rest
        acc = _sigmoid(acc + b_ref[...])
    elif epilogue == "residual":
        r_ref, o_ref = rest
        acc = acc + r_ref[...]
    o_ref[...] = acc.astype(o_ref.dtype)


def matmul(a, w, *, out_dtype, epilogue="plain", extra=None, tm=1024, tn=512, name="matmul"):
    n, k = a.shape
    m = w.shape[1]
    in_specs = [pl.BlockSpec((tm, k), lambda i, j: (i, 0)),
                pl.BlockSpec((k, tn), lambda i, j: (0, j))]
    args = [a, w]
    if epilogue == "bias_sigmoid":
        in_specs.append(pl.BlockSpec((1, tn), lambda i, j: (0, j)))
        args.append(extra.reshape(1, m))
    elif epilogue == "residual":
        in_specs.append(pl.BlockSpec((tm, tn), lambda i, j: (i, j)))
        args.append(extra)
    return pl.pallas_call(
        functools.partial(_mm_kernel, epilogue=epilogue),
        out_shape=jax.ShapeDtypeStruct((n, m), out_dtype),
        grid=(n // tm, m // tn),
        in_specs=in_specs,
        out_specs=pl.BlockSpec((tm, tn), lambda i, j: (i, j)),
        compiler_params=_cparams(("parallel", "arbitrary")),
        name=name,
    )(*args)


def _in_proj_kernel(a_ref, w_ref, g_ref, o_ref, *, tn):
    j = pl.program_id(1)
    acc = jnp.dot(a_ref[...], w_ref[...], preferred_element_type=F32)
    is_qk = (j >= COL_SQ // tn) & (j < COL_SV // tn)

    @pl.when(is_qk)
    def _():
        for hh in range(tn // SB_HEAD_DIM):
            cs = slice(hh * SB_HEAD_DIM, (hh + 1) * SB_HEAD_DIM)
            o_ref[:, cs] = _head_rmsnorm(acc[:, cs], g_ref[:, cs]).astype(o_ref.dtype)

    @pl.when(jnp.logical_not(is_qk))
    def _():
        o_ref[...] = acc.astype(o_ref.dtype)


def in_proj(h, w, qk_gain, tm=1024, tn=512):
    n, k = h.shape
    m = w.shape[1]
    assert COL_SQ % tn == 0 and COL_SV % tn == 0 and tn % SB_HEAD_DIM == 0
    return pl.pallas_call(
        functools.partial(_in_proj_kernel, tn=tn),
        out_shape=jax.ShapeDtypeStruct((n, m), BF16),
        grid=(n // tm, m // tn),
        in_specs=[pl.BlockSpec((tm, k), lambda i, j: (i, 0)),
                  pl.BlockSpec((k, tn), lambda i, j: (0, j)),
                  pl.BlockSpec((1, tn), lambda i, j: (0, j))],
        out_specs=pl.BlockSpec((tm, tn), lambda i, j: (i, j)),
        compiler_params=_cparams(("parallel", "arbitrary")),
        name="in_proj",
    )(h, w, qk_gain)


def _row_proj_kernel(a_ref, w_ref, r_ref, *rest, with_norm):
    x = r_ref[...] + jnp.dot(a_ref[...], w_ref[...], preferred_element_type=F32)
    if with_norm:
        g_ref, o_ref, h_ref = rest
        ms = jnp.mean(x * x, axis=-1, keepdims=True)
        h_ref[...] = (x * lax.rsqrt(ms + EPS) * g_ref[...]).astype(h_ref.dtype)
    else:
        (o_ref,) = rest
    o_ref[...] = x


def out_proj(a, w, resid, norm_gain=None, tm=512):
    n, k = a.shape
    d = w.shape[1]
    with_norm = norm_gain is not None
    row = lambda width: pl.BlockSpec((tm, width), lambda i: (i, 0))
    in_specs = [row(k), pl.BlockSpec((k, d), lambda i: (0, 0), pipeline_mode=pl.Buffered(1)),
                row(d)]
    args = [a, w, resid]
    out_shape = [jax.ShapeDtypeStruct((n, d), F32)]
    out_specs = [row(d)]
    if with_norm:
        in_specs.append(pl.BlockSpec((1, d), lambda i: (0, 0)))
        args.append(norm_gain.reshape(1, d))
        out_shape.append(jax.ShapeDtypeStruct((n, d), BF16))
        out_specs.append(row(d))
    res = pl.pallas_call(
        functools.partial(_row_proj_kernel, with_norm=with_norm),
        out_shape=tuple(out_shape),
        grid=(n // tm,),
        in_specs=in_specs,
        out_specs=tuple(out_specs),
        compiler_params=_cparams(("parallel",)),
        name="out_proj",
    )(*args)
    return res if with_norm else res[0]


POOL_HALO = 128


def _pool_kernel(u_ref, w_ref, s_ref, o_ref, prev_ref, *, t):
    sb = pl.program_id(1)

    @pl.when(sb == 0)
    def _():
        prev_ref[...] = jnp.zeros_like(prev_ref)

    row = lax.broadcasted_iota(jnp.int32, (t, t), 0)
    col = lax.broadcasted_iota(jnp.int32, (t, t), 1)
    prow = lax.broadcasted_iota(jnp.int32, (t, POOL_HALO), 0)
    pcol = lax.broadcasted_iota(jnp.int32, (t, POOL_HALO), 1) - POOL_HALO
    tg = sb * t + lax.broadcasted_iota(jnp.int32, (t, 1), 0)
    for gi, w in enumerate(POOL_WINDOWS):
        cs = slice(gi * POOL_GROUP_W, (gi + 1) * POOL_GROUP_W)
        u = u_ref[:, cs]
        band_cur = ((col <= row) & (col > row - w)).astype(BF16)
        band_prev = ((pcol > prow - w) & (pcol + sb * t >= 0)).astype(BF16)
        win = jnp.dot(band_cur, u, preferred_element_type=F32)
        win = win + jnp.dot(band_prev, prev_ref[:, cs], preferred_element_type=F32)
        count = jnp.minimum(tg + 1, w).astype(F32)
        pooled = win / count - u.astype(F32)
        mixed = jnp.dot(pooled.astype(BF16), w_ref[gi], preferred_element_type=F32)
        o_ref[:, cs] = (mixed * s_ref[:, cs]).astype(o_ref.dtype)
    prev_ref[...] = u_ref[t - POOL_HALO:, :]


def pool_mixer(z3, pool_w, scale, t=256):
    b, s, _ = z3.shape
    return pl.pallas_call(
        functools.partial(_pool_kernel, t=t),
        out_shape=jax.ShapeDtypeStruct((b, s, POOL_W), BF16),
        grid=(b, s // t),
        in_specs=[pl.BlockSpec((None, t, POOL_W), lambda bi, si: (bi, si, COL_POOL // POOL_W)),
                  pl.BlockSpec((POOL_GROUPS, POOL_GROUP_W, POOL_GROUP_W), lambda bi, si: (0, 0, 0)),
                  pl.BlockSpec((1, POOL_W), lambda bi, si: (0, 0))],
        out_specs=pl.BlockSpec((None, t, POOL_W), lambda bi, si: (bi, si, 0)),
        scratch_shapes=[pltpu.VMEM((POOL_HALO, POOL_W), BF16)],
        compiler_params=_cparams(("parallel", "arbitrary")),
        name="pool_mixer",
    )(z3, pool_w, scale.reshape(1, POOL_W))


def _head_rmsnorm(x, g):
    ms = jnp.mean(x * x, axis=-1, keepdims=True)
    return x * lax.rsqrt(ms + EPS) * g


def _sb_kernel(q_ref, k_ref, v_ref, o_ref, acc_ref, c_ref, *, t, nh):
    i = pl.program_id(2)
    hd = SB_HEAD_DIM
    row = lax.broadcasted_iota(jnp.int32, (t, t), 0)
    col = lax.broadcasted_iota(jnp.int32, (t, t), 1)
    diag_mask = col < row
    suffix = (row > col).astype(BF16)
    suffix2 = jnp.concatenate([suffix, suffix], axis=0)
    has_prev = jnp.broadcast_to(i > 0, (t, t))

    def block(h, start, mask, c):
        hs = slice(h * hd, (h + 1) * hd)
        k = k_ref[pl.ds(start, t), hs]
        z = lax.dot_general(q_ref[:, hs], k, (((1,), (1,)), ((), ())),
                            preferred_element_type=F32)
        soft = jnp.log(1.0 + jnp.exp(-jnp.abs(z)))
        neg_part = jnp.minimum(z, 0.0)
        log_beta = neg_part - soft
        log_keep = (neg_part - z) - soft
        if mask is not None:
            log_keep = jnp.where(mask, log_keep, 0.0)
        hi, lo = _split_bf16(log_keep)
        later = jnp.dot(jnp.concatenate([hi, lo], axis=1), suffix2, preferred_element_type=F32)
        a = jnp.exp(log_beta + later + c)
        if mask is not None:
            a = jnp.where(mask, a, 0.0)
        out = jnp.dot(a.astype(BF16), v_ref[pl.ds(start, t), hs], preferred_element_type=F32)
        return out, c + later[:, 0:1] + log_keep[:, 0:1]

    diag_start = pl.multiple_of(i * t, t)
    prev_start = pl.multiple_of(jnp.maximum(i - 1, 0) * t, t)
    live = jnp.int32(0)
    for h in range(nh):
        out_a, c = block(h, diag_start, diag_mask, jnp.zeros((t, 1), F32))
        out_b, c = block(h, prev_start, has_prev, c)
        acc_ref[:, h * hd:(h + 1) * hd] = out_a + out_b
        c_ref[h] = c
        live = jnp.maximum(live, (jnp.max(c) > SB_F32_ZERO_LOG).astype(jnp.int32))

    def body(carry):
        j, _ = carry
        start = pl.multiple_of(j * t, t)
        live = jnp.int32(0)
        for h in range(nh):
            out, c = block(h, start, None, c_ref[h])
            acc_ref[:, h * hd:(h + 1) * hd] += out
            c_ref[h] = c
            live = jnp.maximum(live, (jnp.max(c) > SB_F32_ZERO_LOG).astype(jnp.int32))
        return j - 1, live

    def cond(carry):
        j, live = carry
        return jnp.logical_and(j >= 0, live > 0)

    lax.while_loop(cond, body, (i - 2, live))
    o_ref[...] = acc_ref[...].astype(o_ref.dtype)


def sb_attention(z3, t=256, nh=2):
    b, s, _ = z3.shape
    w = nh * SB_HEAD_DIM
    return pl.pallas_call(
        functools.partial(_sb_kernel, t=t, nh=nh),
        out_shape=jax.ShapeDtypeStruct((b, s, SB_W), BF16),
        grid=(b, SB_HEADS // nh, s // t),
        in_specs=[pl.BlockSpec((None, t, w), lambda bi, h, i: (bi, i, COL_SQ // w + h)),
                  pl.BlockSpec((None, s, w), lambda bi, h, i: (bi, 0, COL_SK // w + h)),
                  pl.BlockSpec((None, s, w), lambda bi, h, i: (bi, 0, COL_SV // w + h))],
        out_specs=pl.BlockSpec((None, t, w), lambda bi, h, i: (bi, i, h)),
        scratch_shapes=[pltpu.VMEM((t, w), F32), pltpu.VMEM((nh, t, 1), F32)],
        compiler_params=_cparams(("parallel", "parallel", "arbitrary")),
        name="sb_attention",
    )(z3, z3, z3)


def _gla_gate_kernel(h_ref, wg_ref, wlr_ref, blr_ref, o_ref, *, tm):
    g = jnp.dot(h_ref[...], wg_ref[...], preferred_element_type=F32)
    pre = jnp.dot(g, wlr_ref[...], preferred_element_type=F32,
                  precision=lax.Precision.HIGHEST) + blr_ref[...]
    log_a = _log_sigmoid(pre) * (1.0 / GLA_TAU)
    row = lax.broadcasted_iota(jnp.int32, (tm, tm), 0)
    col = lax.broadcasted_iota(jnp.int32, (tm, tm), 1)
    shift = GLA_CHUNK.bit_length() - 1
    tri = ((col <= row) & ((col >> shift) == (row >> shift))).astype(BF16)
    hi, lo = _split_bf16(log_a)
    o_ref[...] = (jnp.dot(tri, hi, preferred_element_type=F32)
                  + jnp.dot(tri, lo, preferred_element_type=F32))


def gla_gate(h, w_glr, w_lr, b_lr, tm=256):
    n, d = h.shape
    return pl.pallas_call(
        functools.partial(_gla_gate_kernel, tm=tm),
        out_shape=jax.ShapeDtypeStruct((n, GLA_K_W), F32),
        grid=(n // tm,),
        in_specs=[pl.BlockSpec((tm, d), lambda i: (i, 0)),
                  pl.BlockSpec((d, LANES), lambda i: (0, 0)),
                  pl.BlockSpec((LANES, GLA_K_W), lambda i: (0, 0)),
                  pl.BlockSpec((1, GLA_K_W), lambda i: (0, 0))],
        out_specs=pl.BlockSpec((tm, GLA_K_W), lambda i: (i, 0)),
        compiler_params=_cparams(("parallel",)),
        name="gla_gate",
    )(h, w_glr, w_lr, b_lr.reshape(1, GLA_K_W))


def _gla_kernel(q_ref, k_ref, v_ref, r_ref, b_ref, g_ref, o_ref, st_ref, *, tc):
    @pl.when(pl.program_id(2) == 0)
    def _():
        st_ref[...] = jnp.zeros_like(st_ref)

    c = GLA_CHUNK
    sub = GLA_SUB
    causal = (lax.broadcasted_iota(jnp.int32, (c, c), 1)
              <= lax.broadcasted_iota(jnp.int32, (c, c), 0))
    g_out = g_ref[...]

    def chunk(ci, carry):
        base = pl.multiple_of(ci * c, c)
        rows = pl.ds(base, c)
        q = q_ref[rows, :].astype(F32) * (GLA_DK ** -0.5)
        k = k_ref[rows, :].astype(F32)
        v = v_ref[rows, :]
        b = b_ref[rows, :]
        b_end = b_ref[pl.ds(base + c - 1, 1), :]
        st = st_ref[...]
        inter = lax.dot_general((q * jnp.exp(b)).astype(BF16), st.astype(BF16),
                                (((1,), (1,)), ((), ())), preferred_element_type=F32)
        scores = []
        for si in range(c // sub):
            lo, hi = si * sub, (si + 1) * sub
            if si > 0:
                ref = b_ref[pl.ds(base + lo - 1, 1), :]
            else:
                ref = jnp.zeros((1, GLA_DK), F32)
            q_t = (q[lo:hi] * jnp.exp(b[lo:hi] - ref)).astype(BF16)
            k_t = (k * jnp.exp(jnp.minimum(ref - b, GLA_EXP_CAP))).astype(BF16)
            scores.append(lax.dot_general(q_t, k_t, (((1,), (1,)), ((), ())),
                                          preferred_element_type=F32))
        sc = jnp.where(causal, jnp.concatenate(scores, axis=0), 0.0)
        o = inter + jnp.dot(sc.astype(BF16), v, preferred_element_type=F32)
        k_e = (k * jnp.exp(b_end - b)).astype(BF16)
        v_t = v.astype(F32).T.astype(BF16)
        st_ref[...] = st * jnp.exp(b_end) + jnp.dot(v_t, k_e, preferred_element_type=F32)
        ms = jnp.mean(o * o, axis=-1, keepdims=True)
        o = o * lax.rsqrt(ms + EPS) * g_out
        r = r_ref[rows, :].astype(F32)
        o_ref[rows, :] = (o * (r * _sigmoid(r))).astype(o_ref.dtype)
        return carry

    lax.fori_loop(0, tc // c, chunk, 0)


def gla_mixer(z3, bcum3, g_out, tc=256):
    b, s, _ = z3.shape
    dk, dv = GLA_DK, GLA_DV
    return pl.pallas_call(
        functools.partial(_gla_kernel, tc=tc),
        out_shape=jax.ShapeDtypeStruct((b, s, GLA_V_W), BF16),
        grid=(b, GLA_HEADS, s // tc),
        in_specs=[pl.BlockSpec((None, tc, dk), lambda bi, h, i: (bi, i, COL_GQ // dk + h)),
                  pl.BlockSpec((None, tc, dk), lambda bi, h, i: (bi, i, COL_GK // dk + h)),
                  pl.BlockSpec((None, tc, dv), lambda bi, h, i: (bi, i, COL_GV // dv + h)),
                  pl.BlockSpec((None, tc, dv), lambda bi, h, i: (bi, i, COL_GR // dv + h)),
                  pl.BlockSpec((None, tc, dk), lambda bi, h, i: (bi, i, h)),
                  pl.BlockSpec((1, dv), lambda bi, h, i: (0, 0))],
        out_specs=pl.BlockSpec((None, tc, dv), lambda bi, h, i: (bi, i, h)),
        scratch_shapes=[pltpu.VMEM((dv, dk), F32)],
        compiler_params=_cparams(("parallel", "parallel", "arbitrary")),
        name="gla_mixer",
    )(z3, z3, z3, z3, bcum3, g_out.reshape(1, dv))


def _merge_kernel(ga_ref, gb_ref, gc_ref, yp_ref, ys_ref, yg_ref, wp_ref, ws_ref, wg_ref, o_ref):
    m = ga_ref[...].astype(F32) * jnp.dot(yp_ref[...], wp_ref[...], preferred_element_type=F32)
    m += gb_ref[...].astype(F32) * jnp.dot(ys_ref[...], ws_ref[...], preferred_element_type=F32)
    m += gc_ref[...].astype(F32) * jnp.dot(yg_ref[...], wg_ref[...], preferred_element_type=F32)
    o_ref[...] = m.astype(o_ref.dtype)


def merge_branches(gates, y_pool, y_sb, y_gla, w_p, w_s, w_g, tm=1024, tn=512):
    n = gates.shape[0]
    d = D_MODEL
    nj = d // tn
    gate_spec = lambda br: pl.BlockSpec((tm, tn), lambda i, j: (i, br * nj + j))
    y_spec = lambda w: pl.BlockSpec((tm, w), lambda i, j: (i, 0))
    w_spec = lambda w: pl.BlockSpec((w, tn), lambda i, j: (0, j))
    return pl.pallas_call(
        _merge_kernel,
        out_shape=jax.ShapeDtypeStruct((n, d), BF16),
        grid=(n // tm, nj),
        in_specs=[gate_spec(0), gate_spec(1), gate_spec(2),
                  y_spec(POOL_W), y_spec(SB_W), y_spec(GLA_V_W),
                  w_spec(POOL_W), w_spec(SB_W), w_spec(GLA_V_W)],
        out_specs=pl.BlockSpec((tm, tn), lambda i, j: (i, j)),
        compiler_params=_cparams(("parallel", "arbitrary")),
        name="merge_branches",
    )(gates, gates, gates, y_pool, y_sb, y_gla, w_p, w_s, w_g)


def _silu(a):
    return a * _sigmoid(a)


def _ffn_kernel(h_ref, w1_ref, w3_ref, w2_ref, o_ref):
    h = h_ref[...]
    a = jnp.dot(h, w1_ref[...], preferred_element_type=F32)
    b = jnp.dot(h, w3_ref[...], preferred_element_type=F32)
    g = (_silu(a) * b).astype(BF16)
    y = jnp.dot(g, w2_ref[...], preferred_element_type=F32)

    @pl.when(pl.program_id(1) == 0)
    def _():
        o_ref[...] = y

    @pl.when(pl.program_id(1) > 0)
    def _():
        o_ref[...] += y


def ffn_dense(h, w1, w3, w2, tm=1024, tf=512):
    n, d = h.shape
    ff = w1.shape[1]
    return pl.pallas_call(
        _ffn_kernel,
        out_shape=jax.ShapeDtypeStruct((n, d), F32),
        grid=(n // tm, ff // tf),
        in_specs=[pl.BlockSpec((tm, d), lambda i, f: (i, 0)),
                  pl.BlockSpec((d, tf), lambda i, f: (0, f)),
                  pl.BlockSpec((d, tf), lambda i, f: (0, f)),
                  pl.BlockSpec((tf, d), lambda i, f: (f, 0))],
        out_specs=pl.BlockSpec((tm, d), lambda i, f: (i, 0)),
        compiler_params=_cparams(("parallel", "arbitrary")),
        name="ffn_dense",
    )(h, w1, w3, w2)


ROW_SLABS = D_MODEL // LANES
MOE_TM = 1024
TOP_K = 2


def _row_copy(src_ref, src_idx, dst_ref, dst_idx, sem):
    return pltpu.make_async_copy(src_ref.at[src_idx], dst_ref.at[dst_idx], sem)


def _dispatch_kernel(dest_ref, h_ref, init_ref, o_ref, sem, *, tm):
    del init_ref
    base = pl.program_id(0) * tm

    def start(t, carry):
        for kk in range(TOP_K):
            _row_copy(h_ref, t, o_ref, dest_ref[TOP_K * (base + t) + kk], sem).start()
        return carry

    def wait(t, carry):
        for kk in range(TOP_K):
            _row_copy(h_ref, t, o_ref, dest_ref[TOP_K * (base + t) + kk], sem).wait()
        return carry

    lax.fori_loop(0, tm, start, 0)
    lax.fori_loop(0, tm, wait, 0)


def moe_dispatch(dest, h_rows, n_rows, tm=256):
    n = h_rows.shape[0]
    grid_spec = pltpu.PrefetchScalarGridSpec(
        num_scalar_prefetch=1,
        grid=(n // tm,),
        in_specs=[pl.BlockSpec((tm, ROW_SLABS, LANES), lambda i, dest: (i, 0, 0)),
                  pl.BlockSpec(memory_space=pl.ANY)],
        out_specs=pl.BlockSpec(memory_space=pl.ANY),
        scratch_shapes=[pltpu.SemaphoreType.DMA],
    )
    return pl.pallas_call(
        functools.partial(_dispatch_kernel, tm=tm),
        out_shape=jax.ShapeDtypeStruct((n_rows, ROW_SLABS, LANES), BF16),
        grid_spec=grid_spec,
        input_output_aliases={2: 0},
        compiler_params=_cparams(("arbitrary",)),
        name="moe_dispatch",
    )(dest, h_rows, jnp.zeros((n_rows, ROW_SLABS, LANES), BF16))


def _moe_ffn_kernel(te_ref, nu_ref, h_ref, w1_ref, w3_ref, w2_ref, o_ref, acc_ref):
    del te_ref
    f = pl.program_id(1)
    last = pl.num_programs(1) - 1
    used = pl.program_id(0) < nu_ref[0]

    @pl.when(used)
    def _():
        h = h_ref[...]
        a = jnp.dot(h, w1_ref[...], preferred_element_type=F32)
        b = jnp.dot(h, w3_ref[...], preferred_element_type=F32)
        g = (_silu(a) * b).astype(BF16)
        y = jnp.dot(g, w2_ref[...], preferred_element_type=F32)

        @pl.when(f == 0)
        def _():
            acc_ref[...] = y

        @pl.when(f > 0)
        def _():
            acc_ref[...] += y

        @pl.when(f == last)
        def _():
            o_ref[...] = acc_ref[...].astype(o_ref.dtype)

    @pl.when(jnp.logical_not(used) & (f == last))
    def _():
        o_ref[...] = jnp.zeros_like(o_ref)


def moe_grouped_ffn(tile_expert, n_used, h_sorted, w1, w3, w2, tm=MOE_TM, tf=512):
    r, d = h_sorted.shape
    ff = w1.shape[2]
    nf = ff // tf

    def f_idx(i, f, nu):
        return jnp.where(i < nu[0], f, nf - 1)

    grid_spec = pltpu.PrefetchScalarGridSpec(
        num_scalar_prefetch=2,
        grid=(r // tm, nf),
        in_specs=[pl.BlockSpec((tm, d), lambda i, f, te, nu: (i, 0)),
                  pl.BlockSpec((None, d, tf), lambda i, f, te, nu: (te[i], 0, f_idx(i, f, nu))),
                  pl.BlockSpec((None, d, tf), lambda i, f, te, nu: (te[i], 0, f_idx(i, f, nu))),
                  pl.BlockSpec((None, tf, d), lambda i, f, te, nu: (te[i], f_idx(i, f, nu), 0))],
        out_specs=pl.BlockSpec((tm, d), lambda i, f, te, nu: (i, 0)),
        scratch_shapes=[pltpu.VMEM((tm, d), F32)],
    )
    return pl.pallas_call(
        _moe_ffn_kernel,
        out_shape=jax.ShapeDtypeStruct((r, d), BF16),
        grid_spec=grid_spec,
        compiler_params=_cparams(("arbitrary", "arbitrary")),
        name="moe_grouped_ffn",
    )(tile_expert, n_used, h_sorted, w1, w3, w2)


def _combine_kernel(dest_ref, w_ref, y_ref, o_ref, buf_ref, sems, *, tm):
    base = pl.program_id(0) * tm

    def start(t, carry):
        for kk in range(TOP_K):
            _row_copy(y_ref, dest_ref[TOP_K * (base + t) + kk], buf_ref.at[kk], t,
                      sems.at[kk]).start()
        return carry

    def wait(t, carry):
        for kk in range(TOP_K):
            _row_copy(y_ref, dest_ref[TOP_K * (base + t) + kk], buf_ref.at[kk], t,
                      sems.at[kk]).wait()
        return carry

    lax.fori_loop(0, tm, start, 0)
    lax.fori_loop(0, tm, wait, 0)
    acc = buf_ref[0].astype(F32) * w_ref[0]
    for kk in range(1, TOP_K):
        acc += buf_ref[kk].astype(F32) * w_ref[kk]
    o_ref[...] = acc


def moe_combine(dest, w_rows, y_rows, n, tm=256):
    grid_spec = pltpu.PrefetchScalarGridSpec(
        num_scalar_prefetch=1,
        grid=(n // tm,),
        in_specs=[pl.BlockSpec((TOP_K, tm, 1, LANES), lambda i, dest: (0, i, 0, 0)),
                  pl.BlockSpec(memory_space=pl.ANY)],
        out_specs=pl.BlockSpec((tm, ROW_SLABS, LANES), lambda i, dest: (i, 0, 0)),
        scratch_shapes=[pltpu.VMEM((TOP_K, tm, ROW_SLABS, LANES), BF16),
                        pltpu.SemaphoreType.DMA((TOP_K,))],
    )
    return pl.pallas_call(
        functools.partial(_combine_kernel, tm=tm),
        out_shape=jax.ShapeDtypeStruct((n, ROW_SLABS, LANES), F32),
        grid_spec=grid_spec,
        compiler_params=_cparams(("arbitrary",)),
        name="moe_combine",
    )(dest, w_rows, y_rows)


def moe_sparse(h2, info, counts, w1, w3, w2):
    n, d = h2.shape
    tm = MOE_TM
    n_tiles = (TOP_K * n) // tm + N_EXPERTS
    n_rows = n_tiles * tm
    cnt = counts[0, :N_EXPERTS].astype(jnp.int32)
    tiles_per = (cnt + tm - 1) // tm
    tile_end = jnp.cumsum(tiles_per)
    group_start = (tile_end - tiles_per) * tm
    n_used = tile_end[-1:]
    tile_expert = jnp.minimum(
        jnp.searchsorted(tile_end, jnp.arange(n_tiles, dtype=jnp.int32), side="right"),
        N_EXPERTS - 1).astype(jnp.int32)
    tile_expert = jnp.where(jnp.arange(n_tiles) < n_used[0], tile_expert,
                            tile_expert[jnp.maximum(n_used[0] - 1, 0)])
    experts = info[:, INFO_E1:INFO_E2 + 1].astype(jnp.int32)
    ranks = info[:, INFO_R1:INFO_R2 + 1].astype(jnp.int32)
    dest = (group_start[experts] + ranks).reshape(TOP_K * n)
    w_rows = jnp.broadcast_to(info[:, INFO_W1:INFO_W2 + 1].T[:, :, None, None],
                              (TOP_K, n, 1, LANES))

    h_sorted = moe_dispatch(dest, h2.reshape(n, ROW_SLABS, LANES), n_rows)
    y_sorted = moe_grouped_ffn(tile_expert, n_used, h_sorted.reshape(n_rows, d), w1, w3, w2)
    y = moe_combine(dest, w_rows, y_sorted.reshape(n_rows, ROW_SLABS, LANES), n)
    return y.reshape(n, d)


def _ple_kernel(*refs, n_terms, with_norm):
    term_refs = refs[:n_terms]
    p_ref, wg_ref, wp_ref = refs[n_terms:n_terms + 3]
    rest = refs[n_terms + 3:]
    if with_norm:
        g_ref, o_ref, h_ref = rest
    else:
        (o_ref,) = rest
    x = term_refs[0][...]
    for r in term_refs[1:]:
        x = x + r[...]
    xb = x.astype(BF16)
    pb = p_ref[...].astype(BF16)
    d = x.shape[1]
    sum_sq = jnp.zeros((x.shape[0], 1), F32)
    for c0 in range(0, d, PLE_COL_CHUNK):
        cs = slice(c0, c0 + PLE_COL_CHUNK)
        gate = _sigmoid(jnp.dot(xb, wg_ref[:, cs], preferred_element_type=F32))
        proj = jnp.dot(pb, wp_ref[:, cs], preferred_element_type=F32)
        xc = x[:, cs] + gate * proj
        o_ref[:, cs] = xc
        sum_sq += jnp.sum(xc * xc, axis=-1, keepdims=True)
    if with_norm:
        inv = lax.rsqrt(sum_sq * (1.0 / d) + EPS)
        h_ref[...] = (o_ref[...] * inv * g_ref[...]).astype(h_ref.dtype)


PLE_COL_CHUNK = 512


def ple_update(terms, p, w_gate, w_proj, norm_gain=None, tm=512):
    n, d = terms[0].shape
    pd = p.shape[1]
    nt = len(terms)
    with_norm = norm_gain is not None
    row = lambda width: pl.BlockSpec((tm, width), lambda i: (i, 0))
    resident = lambda shape: pl.BlockSpec(shape, lambda i: (0, 0), pipeline_mode=pl.Buffered(1))
    in_specs = [row(d)] * nt + [row(pd), resident((d, d)), resident((pd, d))]
    args = [*terms, p, w_gate, w_proj]
    out_shape = [jax.ShapeDtypeStruct((n, d), F32)]
    out_specs = [row(d)]
    if with_norm:
        in_specs.append(pl.BlockSpec((1, d), lambda i: (0, 0)))
        args.append(norm_gain.reshape(1, d))
        out_shape.append(jax.ShapeDtypeStruct((n, d), BF16))
        out_specs.append(row(d))
    res = pl.pallas_call(
        functools.partial(_ple_kernel, n_terms=nt, with_norm=with_norm),
        out_shape=tuple(out_shape),
        grid=(n // tm,),
        in_specs=in_specs,
        out_specs=tuple(out_specs),
        compiler_params=_cparams(("parallel",)),
        name="ple_update",
    )(*args)
    return res if with_norm else res[0]


def _split_w_in(w_in_i):
    c_lr = COL_GR
    main = jnp.concatenate([w_in_i[:, :c_lr], w_in_i[:, c_lr + GLA_RANK:]], axis=1).astype(BF16)
    glr = jnp.zeros((D_MODEL, LANES), BF16).at[:, :GLA_RANK].set(
        w_in_i[:, c_lr:c_lr + GLA_RANK].astype(BF16))
    return main, glr


def kernel(x, p, g_mix, w_in, w_branch_gate, b_branch_gate, pool_w, pool_scale, sb_gq, sb_gk,
           gla_w_lr, gla_b_lr, gla_g_out, w_up_pool, w_up_sb, w_up_gla, w_o, g_ffn,
           ffn_w1, ffn_w3, ffn_w2, moe_router, moe_w1, moe_w3, moe_w2, ple_w_proj, ple_w_gate):
    bsz, seq, d = x.shape
    n = bsz * seq
    depth = w_in.shape[0]
    xf = x.reshape(n, d)
    h = rmsnorm(xf, g_mix[0])
    for i in range(depth):
        w_main, w_glr = _split_w_in(w_in[i])
        w_lr_pad = jnp.zeros((LANES, GLA_K_W), F32).at[:GLA_RANK].set(gla_w_lr[i])
        qk_gain = jnp.ones((1, Z_COLS), F32)
        qk_gain = qk_gain.at[0, COL_SQ:COL_SK].set(jnp.tile(sb_gq[i] * SB_HEAD_DIM ** -0.5, SB_HEADS))
        qk_gain = qk_gain.at[0, COL_SK:COL_SV].set(jnp.tile(sb_gk[i], SB_HEADS))

        z = in_proj(h, w_main, qk_gain)
        gates = matmul(h, w_branch_gate[i].astype(BF16), out_dtype=BF16, epilogue="bias_sigmoid",
                       extra=b_branch_gate[i], name="branch_gates")
        bcum = gla_gate(h, w_glr, w_lr_pad, gla_b_lr[i])

        z3 = z.reshape(bsz, seq, Z_COLS)
        y_pool = pool_mixer(z3, pool_w[i].astype(BF16), pool_scale[i])
        y_sb = sb_attention(z3)
        y_gla = gla_mixer(z3, bcum.reshape(bsz, seq, GLA_K_W), gla_g_out[i])

        merged = merge_branches(gates, y_pool.reshape(n, POOL_W), y_sb.reshape(n, SB_W),
                                y_gla.reshape(n, GLA_V_W), w_up_pool[i].astype(BF16),
                                w_up_sb[i].astype(BF16), w_up_gla[i].astype(BF16))

        j = i // 2
        if i % 2 == 0:
            xf, h2 = out_proj(merged, w_o[i].astype(BF16), xf, norm_gain=g_ffn[i])
            terms = [xf, ffn_dense(h2, ffn_w1[j].astype(BF16), ffn_w3[j].astype(BF16),
                                   ffn_w2[j].astype(BF16))]
        else:
            xf = out_proj(merged, w_o[i].astype(BF16), xf)
            h2, info, counts = rmsnorm_router(xf, g_ffn[i], moe_router[j])
            terms = [xf, moe_sparse(h2, info, counts, moe_w1[j].astype(BF16),
                                    moe_w3[j].astype(BF16), moe_w2[j].astype(BF16))]

        if i + 1 < depth:
            xf, h = ple_update(terms, p[i].reshape(n, PLE_DIM), ple_w_gate[i].astype(BF16),
                               ple_w_proj[i].astype(BF16), norm_gain=g_mix[i + 1])
        else:
            xf = ple_update(terms, p[i].reshape(n, PLE_DIM), ple_w_gate[i].astype(BF16),
                            ple_w_proj[i].astype(BF16))
    return xf.reshape(bsz, seq, d)
```

```python
import functools

import jax
import jax.numpy as jnp
from jax import lax
from jax.experimental import pallas as pl
from jax.experimental.pallas import tpu as pltpu

F32 = jnp.float32
BF16 = jnp.bfloat16

EPS = 1e-6
D_MODEL = 2048
PLE_DIM = 256
POOL_GROUPS = 4
POOL_GROUP_W = 256
POOL_W = POOL_GROUPS * POOL_GROUP_W
POOL_WINDOWS = (2, 4, 8, 16)
SB_HEADS = 8
SB_HEAD_DIM = 128
SB_W = SB_HEADS * SB_HEAD_DIM
GLA_HEADS = 4
GLA_DK = 128
GLA_DV = 256
GLA_K_W = GLA_HEADS * GLA_DK
GLA_V_W = GLA_HEADS * GLA_DV
GLA_RANK = 16
GLA_TAU = 16.0
D_FF = 5632
N_EXPERTS = 8
N_BRANCH = 3

LANES = 128
VMEM_LIMIT = 56 * 1024 * 1024

COL_POOL = 0
COL_SQ = COL_POOL + POOL_W
COL_SK = COL_SQ + SB_W
COL_SV = COL_SK + SB_W
COL_GQ = COL_SV + SB_W
COL_GK = COL_GQ + GLA_K_W
COL_GV = COL_GK + GLA_K_W
COL_GR = COL_GV + GLA_V_W
Z_COLS = COL_GR + GLA_V_W

GLA_CHUNK = 64
GLA_SUB = 16
GLA_EXP_CAP = 80.0
SB_F32_ZERO_LOG = -104.0


def _cparams(sem):
    return pltpu.CompilerParams(dimension_semantics=sem, vmem_limit_bytes=VMEM_LIMIT)


def _log_sigmoid(z):
    return jnp.minimum(z, 0.0) - jnp.log(1.0 + jnp.exp(-jnp.abs(z)))


def _sigmoid(z):
    return 1.0 / (1.0 + jnp.exp(-z))


def _split_bf16(x):
    hi = x.astype(BF16)
    lo = (x - hi.astype(F32)).astype(BF16)
    return hi, lo


def _rmsnorm_kernel(x_ref, g_ref, o_ref):
    x = x_ref[...]
    ms = jnp.mean(x * x, axis=-1, keepdims=True)
    o_ref[...] = (x * lax.rsqrt(ms + EPS) * g_ref[...]).astype(o_ref.dtype)


def rmsnorm(x, g, tm=512):
    n, d = x.shape
    return pl.pallas_call(
        _rmsnorm_kernel,
        out_shape=jax.ShapeDtypeStruct((n, d), BF16),
        grid=(n // tm,),
        in_specs=[pl.BlockSpec((tm, d), lambda i: (i, 0)),
                  pl.BlockSpec((1, d), lambda i: (0, 0))],
        out_specs=pl.BlockSpec((tm, d), lambda i: (i, 0)),
        compiler_params=_cparams(("parallel",)),
        name="rmsnorm",
    )(x, g.reshape(1, d))


def _route_top2(h, rhi_ref, rlo_ref, info_ref, cnt_ref, carry_ref):
    tm = h.shape[0]

    @pl.when(pl.program_id(0) == 0)
    def _():
        carry_ref[...] = jnp.zeros_like(carry_ref)

    h_hi, h_lo = _split_bf16(h)
    logits = (jnp.dot(h_hi, rhi_ref[...], preferred_element_type=F32)
              + jnp.dot(h_lo, rhi_ref[...], preferred_element_type=F32)
              + jnp.dot(h_hi, rlo_ref[...], preferred_element_type=F32))
    lane = lax.broadcasted_iota(jnp.int32, logits.shape, 1).astype(F32)
    neg = jnp.float32(-jnp.inf)
    logits = jnp.where(lane < N_EXPERTS, logits, neg)
    m1 = jnp.max(logits, axis=-1, keepdims=True)
    i1 = jnp.min(jnp.where(logits == m1, lane, float(LANES)), axis=-1, keepdims=True)
    sel1 = lane == i1
    rest = jnp.where(sel1, neg, logits)
    m2 = jnp.max(rest, axis=-1, keepdims=True)
    i2 = jnp.min(jnp.where(rest == m2, lane, float(LANES)), axis=-1, keepdims=True)
    sel2 = lane == i2
    e2 = jnp.exp(m2 - m1)
    den = 1.0 + e2
    sel = (sel1 | sel2).astype(BF16)
    before = (lax.broadcasted_iota(jnp.int32, (tm, tm), 1)
              < lax.broadcasted_iota(jnp.int32, (tm, tm), 0)).astype(BF16)
    carry = carry_ref[...]
    rank = jnp.dot(before, sel, preferred_element_type=F32) + carry[0:1, :]
    r1 = jnp.sum(jnp.where(sel1, rank, 0.0), axis=-1, keepdims=True)
    r2 = jnp.sum(jnp.where(sel2, rank, 0.0), axis=-1, keepdims=True)
    fields = (i1, i2, 1.0 / den, e2 / den, r1, r2)
    info = jnp.zeros_like(logits)
    for li, val in enumerate(fields):
        info = jnp.where(lane == li, val, info)
    info_ref[...] = info
    carry = carry + jnp.sum(sel.astype(F32), axis=0, keepdims=True)
    carry_ref[...] = carry
    cnt_ref[...] = carry


INFO_E1, INFO_E2, INFO_W1, INFO_W2, INFO_R1, INFO_R2 = range(6)


def _in_proj_kernel(a_ref, w_ref, g_ref, o_ref, *, tn):
    j = pl.program_id(1)
    acc = jnp.dot(a_ref[...], w_ref[...], preferred_element_type=F32)
    is_qk = (j >= COL_SQ // tn) & (j < COL_SV // tn)

    @pl.when(is_qk)
    def _():
        for hh in range(tn // SB_HEAD_DIM):
            cs = slice(hh * SB_HEAD_DIM, (hh + 1) * SB_HEAD_DIM)
            o_ref[:, cs] = _head_rmsnorm(acc[:, cs], g_ref[:, cs]).astype(o_ref.dtype)

    @pl.when(jnp.logical_not(is_qk))
    def _():
        o_ref[...] = acc.astype(o_ref.dtype)


def in_proj(h, w, qk_gain, tm=1024, tn=512):
    n, k = h.shape
    m = w.shape[1]
    assert COL_SQ % tn == 0 and COL_SV % tn == 0 and tn % SB_HEAD_DIM == 0
    return pl.pallas_call(
        functools.partial(_in_proj_kernel, tn=tn),
        out_shape=jax.ShapeDtypeStruct((n, m), BF16),
        grid=(n // tm, m // tn),
        in_specs=[pl.BlockSpec((tm, k), lambda i, j: (i, 0)),
                  pl.BlockSpec((k, tn), lambda i, j: (0, j)),
                  pl.BlockSpec((1, tn), lambda i, j: (0, j))],
        out_specs=pl.BlockSpec((tm, tn), lambda i, j: (i, j)),
        compiler_params=_cparams(("parallel", "arbitrary")),
        name="in_proj",
    )(h, w, qk_gain)


def _row_proj_kernel(a_ref, w_ref, r_ref, g_ref, o_ref, h_ref):
    x = r_ref[...] + jnp.dot(a_ref[...], w_ref[...], preferred_element_type=F32)
    ms = jnp.mean(x * x, axis=-1, keepdims=True)
    o_ref[...] = x
    h_ref[...] = (x * lax.rsqrt(ms + EPS) * g_ref[...]).astype(h_ref.dtype)


def out_proj(a, w, resid, norm_gain, tm=512):
    n, k = a.shape
    d = w.shape[1]
    row = lambda width: pl.BlockSpec((tm, width), lambda i: (i, 0))
    return pl.pallas_call(
        _row_proj_kernel,
        out_shape=(jax.ShapeDtypeStruct((n, d), F32), jax.ShapeDtypeStruct((n, d), BF16)),
        grid=(n // tm,),
        in_specs=[row(k), pl.BlockSpec((k, d), lambda i: (0, 0), pipeline_mode=pl.Buffered(1)),
                  row(d), pl.BlockSpec((1, d), lambda i: (0, 0))],
        out_specs=(row(d), row(d)),
        compiler_params=_cparams(("parallel",)),
        name="out_proj",
    )(a, w, resid, norm_gain.reshape(1, d))


def _router_kernel(x_ref, g_ref, rhi_ref, rlo_ref, info_ref, cnt_ref, carry_ref):
    x = x_ref[...]
    ms = jnp.mean(x * x, axis=-1, keepdims=True)
    h = x * lax.rsqrt(ms + EPS) * g_ref[...]
    _route_top2(h, rhi_ref, rlo_ref, info_ref, cnt_ref, carry_ref)


def route_tokens(x, g, router, tm=512):
    n, d = x.shape
    r_pad = jnp.zeros((d, LANES), F32).at[:, :N_EXPERTS].set(router)
    fixed = lambda shape: pl.BlockSpec(shape, lambda i: (0, 0))
    return pl.pallas_call(
        _router_kernel,
        out_shape=(jax.ShapeDtypeStruct((n, LANES), F32), jax.ShapeDtypeStruct((8, LANES), F32)),
        grid=(n // tm,),
        in_specs=[pl.BlockSpec((tm, d), lambda i: (i, 0)), fixed((1, d)),
                  fixed((d, LANES)), fixed((d, LANES))],
        out_specs=(pl.BlockSpec((tm, LANES), lambda i: (i, 0)), fixed((8, LANES))),
        scratch_shapes=[pltpu.VMEM((8, LANES), F32)],
        compiler_params=_cparams(("arbitrary",)),
        name="route_tokens",
    )(x, g.reshape(1, d), *_split_bf16(r_pad))


POOL_HALO = 128


def _pool_kernel(u_ref, w_ref, s_ref, o_ref, prev_ref, *, t):
    sb = pl.program_id(1)

    @pl.when(sb == 0)
    def _():
        prev_ref[...] = jnp.zeros_like(prev_ref)

    row = lax.broadcasted_iota(jnp.int32, (t, t), 0)
    col = lax.broadcasted_iota(jnp.int32, (t, t), 1)
    prow = lax.broadcasted_iota(jnp.int32, (t, POOL_HALO), 0)
    pcol = lax.broadcasted_iota(jnp.int32, (t, POOL_HALO), 1) - POOL_HALO
    tg = sb * t + lax.broadcasted_iota(jnp.int32, (t, 1), 0)
    for gi, w in enumerate(POOL_WINDOWS):
        cs = slice(gi * POOL_GROUP_W, (gi + 1) * POOL_GROUP_W)
        u = u_ref[:, cs]
        band_cur = ((col <= row) & (col > row - w)).astype(BF16)
        band_prev = ((pcol > prow - w) & (pcol + sb * t >= 0)).astype(BF16)
        win = jnp.dot(band_cur, u, preferred_element_type=F32)
        win = win + jnp.dot(band_prev, prev_ref[:, cs], preferred_element_type=F32)
        count = jnp.minimum(tg + 1, w).astype(F32)
        pooled = win / count - u.astype(F32)
        mixed = jnp.dot(pooled.astype(BF16), w_ref[gi], preferred_element_type=F32)
        o_ref[:, cs] = (mixed * s_ref[:, cs]).astype(o_ref.dtype)
    prev_ref[...] = u_ref[t - POOL_HALO:, :]


def pool_mixer(z3, pool_w, scale, t=256):
    b, s, _ = z3.shape
    return pl.pallas_call(
        functools.partial(_pool_kernel, t=t),
        out_shape=jax.ShapeDtypeStruct((b, s, POOL_W), BF16),
        grid=(b, s // t),
        in_specs=[pl.BlockSpec((None, t, POOL_W), lambda bi, si: (bi, si, COL_POOL // POOL_W)),
                  pl.BlockSpec((POOL_GROUPS, POOL_GROUP_W, POOL_GROUP_W), lambda bi, si: (0, 0, 0)),
                  pl.BlockSpec((1, POOL_W), lambda bi, si: (0, 0))],
        out_specs=pl.BlockSpec((None, t, POOL_W), lambda bi, si: (bi, si, 0)),
        scratch_shapes=[pltpu.VMEM((POOL_HALO, POOL_W), BF16)],
        compiler_params=_cparams(("parallel", "arbitrary")),
        name="pool_mixer",
    )(z3, pool_w, scale.reshape(1, POOL_W))


def _head_rmsnorm(x, g):
    ms = jnp.mean(x * x, axis=-1, keepdims=True)
    return x * lax.rsqrt(ms + EPS) * g


def _sb_kernel(q_ref, k_ref, v_ref, o_ref, acc_ref, c_ref, *, t, nh):
    i = pl.program_id(2)
    hd = SB_HEAD_DIM
    row = lax.broadcasted_iota(jnp.int32, (t, t), 0)
    col = lax.broadcasted_iota(jnp.int32, (t, t), 1)
    diag_mask = col < row
    suffix = (row > col).astype(BF16)
    suffix2 = jnp.concatenate([suffix, suffix], axis=0)
    has_prev = jnp.broadcast_to(i > 0, (t, t))

    def block(h, start, mask, c):
        hs = slice(h * hd, (h + 1) * hd)
        k = k_ref[pl.ds(start, t), hs]
        z = lax.dot_general(q_ref[:, hs], k, (((1,), (1,)), ((), ())),
                            preferred_element_type=F32)
        soft = jnp.log(1.0 + jnp.exp(-jnp.abs(z)))
        neg_part = jnp.minimum(z, 0.0)
        log_beta = neg_part - soft
        log_keep = (neg_part - z) - soft
        if mask is not None:
            log_keep = jnp.where(mask, log_keep, 0.0)
        hi, lo = _split_bf16(log_keep)
        later = jnp.dot(jnp.concatenate([hi, lo], axis=1), suffix2, preferred_element_type=F32)
        a = jnp.exp(log_beta + later + c)
        if mask is not None:
            a = jnp.where(mask, a, 0.0)
        out = jnp.dot(a.astype(BF16), v_ref[pl.ds(start, t), hs], preferred_element_type=F32)
        return out, c + later[:, 0:1] + log_keep[:, 0:1]

    diag_start = pl.multiple_of(i * t, t)
    prev_start = pl.multiple_of(jnp.maximum(i - 1, 0) * t, t)
    live = jnp.int32(0)
    for h in range(nh):
        out_a, c = block(h, diag_start, diag_mask, jnp.zeros((t, 1), F32))
        out_b, c = block(h, prev_start, has_prev, c)
        acc_ref[:, h * hd:(h + 1) * hd] = out_a + out_b
        c_ref[h] = c
        live = jnp.maximum(live, (jnp.max(c) > SB_F32_ZERO_LOG).astype(jnp.int32))

    def body(carry):
        j, _ = carry
        start = pl.multiple_of(j * t, t)
        live = jnp.int32(0)
        for h in range(nh):
            out, c = block(h, start, None, c_ref[h])
            acc_ref[:, h * hd:(h + 1) * hd] += out
            c_ref[h] = c
            live = jnp.maximum(live, (jnp.max(c) > SB_F32_ZERO_LOG).astype(jnp.int32))
        return j - 1, live

    def cond(carry):
        j, live = carry
        return jnp.logical_and(j >= 0, live > 0)

    lax.while_loop(cond, body, (i - 2, live))
    o_ref[...] = acc_ref[...].astype(o_ref.dtype)


def sb_attention(z3, t=256, nh=2):
    b, s, _ = z3.shape
    w = nh * SB_HEAD_DIM
    return pl.pallas_call(
        functools.partial(_sb_kernel, t=t, nh=nh),
        out_shape=jax.ShapeDtypeStruct((b, s, SB_W), BF16),
        grid=(b, SB_HEADS // nh, s // t),
        in_specs=[pl.BlockSpec((None, t, w), lambda bi, h, i: (bi, i, COL_SQ // w + h)),
                  pl.BlockSpec((None, s, w), lambda bi, h, i: (bi, 0, COL_SK // w + h)),
                  pl.BlockSpec((None, s, w), lambda bi, h, i: (bi, 0, COL_SV // w + h))],
        out_specs=pl.BlockSpec((None, t, w), lambda bi, h, i: (bi, i, h)),
        scratch_shapes=[pltpu.VMEM((t, w), F32), pltpu.VMEM((nh, t, 1), F32)],
        compiler_params=_cparams(("parallel", "parallel", "arbitrary")),
        name="sb_attention",
    )(z3, z3, z3)


def _gla_gate_kernel(h_ref, wg_ref, wlr_ref, blr_ref, o_ref, *, tm):
    g = jnp.dot(h_ref[...], wg_ref[...], preferred_element_type=F32)
    pre = jnp.dot(g, wlr_ref[...], preferred_element_type=F32,
                  precision=lax.Precision.HIGHEST) + blr_ref[...]
    log_a = _log_sigmoid(pre) * (1.0 / GLA_TAU)
    row = lax.broadcasted_iota(jnp.int32, (tm, tm), 0)
    col = lax.broadcasted_iota(jnp.int32, (tm, tm), 1)
    shift = GLA_CHUNK.bit_length() - 1
    tri = ((col <= row) & ((col >> shift) == (row >> shift))).astype(BF16)
    hi, lo = _split_bf16(log_a)
    o_ref[...] = (jnp.dot(tri, hi, preferred_element_type=F32)
                  + jnp.dot(tri, lo, preferred_element_type=F32))


def gla_gate(h, w_glr, w_lr, b_lr, tm=256):
    n, d = h.shape
    return pl.pallas_call(
        functools.partial(_gla_gate_kernel, tm=tm),
        out_shape=jax.ShapeDtypeStruct((n, GLA_K_W), F32),
        grid=(n // tm,),
        in_specs=[pl.BlockSpec((tm, d), lambda i: (i, 0)),
                  pl.BlockSpec((d, LANES), lambda i: (0, 0)),
                  pl.BlockSpec((LANES, GLA_K_W), lambda i: (0, 0)),
                  pl.BlockSpec((1, GLA_K_W), lambda i: (0, 0))],
        out_specs=pl.BlockSpec((tm, GLA_K_W), lambda i: (i, 0)),
        compiler_params=_cparams(("parallel",)),
        name="gla_gate",
    )(h, w_glr, w_lr, b_lr.reshape(1, GLA_K_W))


def _gla_kernel(q_ref, k_ref, v_ref, r_ref, b_ref, g_ref, o_ref, st_ref, *, tc):
    @pl.when(pl.program_id(2) == 0)
    def _():
        st_ref[...] = jnp.zeros_like(st_ref)

    c = GLA_CHUNK
    sub = GLA_SUB
    causal = (lax.broadcasted_iota(jnp.int32, (c, c), 1)
              <= lax.broadcasted_iota(jnp.int32, (c, c), 0))
    g_out = g_ref[...]

    def chunk(ci, carry):
        base = pl.multiple_of(ci * c, c)
        rows = pl.ds(base, c)
        q = q_ref[rows, :].astype(F32) * (GLA_DK ** -0.5)
        k = k_ref[rows, :].astype(F32)
        v = v_ref[rows, :]
        b = b_ref[rows, :]
        b_end = b_ref[pl.ds(base + c - 1, 1), :]
        st = st_ref[...]
        inter = lax.dot_general((q * jnp.exp(b)).astype(BF16), st.astype(BF16),
                                (((1,), (1,)), ((), ())), preferred_element_type=F32)
        scores = []
        for si in range(c // sub):
            lo, hi = si * sub, (si + 1) * sub
            if si > 0:
                ref = b_ref[pl.ds(base + lo - 1, 1), :]
            else:
                ref = jnp.zeros((1, GLA_DK), F32)
            q_t = (q[lo:hi] * jnp.exp(b[lo:hi] - ref)).astype(BF16)
            k_t = (k * jnp.exp(jnp.minimum(ref - b, GLA_EXP_CAP))).astype(BF16)
            scores.append(lax.dot_general(q_t, k_t, (((1,), (1,)), ((), ())),
                                          preferred_element_type=F32))
        sc = jnp.where(causal, jnp.concatenate(scores, axis=0), 0.0)
        o = inter + jnp.dot(sc.astype(BF16), v, preferred_element_type=F32)
        k_e = (k * jnp.exp(b_end - b)).astype(BF16)
        v_t = v.astype(F32).T.astype(BF16)
        st_ref[...] = st * jnp.exp(b_end) + jnp.dot(v_t, k_e, preferred_element_type=F32)
        ms = jnp.mean(o * o, axis=-1, keepdims=True)
        o = o * lax.rsqrt(ms + EPS) * g_out
        r = r_ref[rows, :].astype(F32)
        o_ref[rows, :] = (o * (r * _sigmoid(r))).astype(o_ref.dtype)
        return carry

    lax.fori_loop(0, tc // c, chunk, 0, unroll=True)


def gla_mixer(z3, bcum3, g_out, tc=256):
    b, s, _ = z3.shape
    dk, dv = GLA_DK, GLA_DV
    return pl.pallas_call(
        functools.partial(_gla_kernel, tc=tc),
        out_shape=jax.ShapeDtypeStruct((b, s, GLA_V_W), BF16),
        grid=(b, GLA_HEADS, s // tc),
        in_specs=[pl.BlockSpec((None, tc, dk), lambda bi, h, i: (bi, i, COL_GQ // dk + h)),
                  pl.BlockSpec((None, tc, dk), lambda bi, h, i: (bi, i, COL_GK // dk + h)),
                  pl.BlockSpec((None, tc, dv), lambda bi, h, i: (bi, i, COL_GV // dv + h)),
                  pl.BlockSpec((None, tc, dv), lambda bi, h, i: (bi, i, COL_GR // dv + h)),
                  pl.BlockSpec((None, tc, dk), lambda bi, h, i: (bi, i, h)),
                  pl.BlockSpec((1, dv), lambda bi, h, i: (0, 0))],
        out_specs=pl.BlockSpec((None, tc, dv), lambda bi, h, i: (bi, i, h)),
        scratch_shapes=[pltpu.VMEM((dv, dk), F32)],
        compiler_params=_cparams(("parallel", "parallel", "arbitrary")),
        name="gla_mixer",
    )(z3, z3, z3, z3, bcum3, g_out.reshape(1, dv))


def _merge_kernel(h_ref, wga_ref, wgb_ref, wgc_ref, ba_ref, bb_ref, bc_ref,
                  yp_ref, ys_ref, yg_ref, wp_ref, ws_ref, wg_ref, o_ref):
    h = h_ref[...]
    m = None
    for wgate_ref, b_ref, y_ref, wup_ref in ((wga_ref, ba_ref, yp_ref, wp_ref),
                                             (wgb_ref, bb_ref, ys_ref, ws_ref),
                                             (wgc_ref, bc_ref, yg_ref, wg_ref)):
        gate = _sigmoid(jnp.dot(h, wgate_ref[...], preferred_element_type=F32) + b_ref[...])
        term = gate * jnp.dot(y_ref[...], wup_ref[...], preferred_element_type=F32)
        m = term if m is None else m + term
    o_ref[...] = m.astype(o_ref.dtype)


def merge_branches(h, w_gate, b_gate, y_pool, y_sb, y_gla, w_p, w_s, w_g, tm=1024, tn=512):
    n, d = h.shape
    nj = d // tn
    gate_w_spec = lambda br: pl.BlockSpec((d, tn), lambda i, j: (0, br * nj + j))
    gate_b_spec = lambda br: pl.BlockSpec((1, tn), lambda i, j: (0, br * nj + j))
    y_spec = lambda w: pl.BlockSpec((tm, w), lambda i, j: (i, 0))
    w_spec = lambda w: pl.BlockSpec((w, tn), lambda i, j: (0, j))
    b_gate = b_gate.reshape(1, N_BRANCH * d)
    return pl.pallas_call(
        _merge_kernel,
        out_shape=jax.ShapeDtypeStruct((n, d), BF16),
        grid=(n // tm, nj),
        in_specs=[y_spec(d), gate_w_spec(0), gate_w_spec(1), gate_w_spec(2),
                  gate_b_spec(0), gate_b_spec(1), gate_b_spec(2),
                  y_spec(POOL_W), y_spec(SB_W), y_spec(GLA_V_W),
                  w_spec(POOL_W), w_spec(SB_W), w_spec(GLA_V_W)],
        out_specs=pl.BlockSpec((tm, tn), lambda i, j: (i, j)),
        compiler_params=_cparams(("parallel", "arbitrary")),
        name="merge_branches",
    )(h, w_gate, w_gate, w_gate, b_gate, b_gate, b_gate, y_pool, y_sb, y_gla, w_p, w_s, w_g)


def _silu(a):
    return a * _sigmoid(a)


def _ffn_kernel(h_ref, w1_ref, w3_ref, w2_ref, o_ref):
    @pl.when(pl.program_id(1) == 0)
    def _():
        o_ref[...] = jnp.zeros_like(o_ref)

    h = h_ref[...]
    a = jnp.dot(h, w1_ref[...], preferred_element_type=F32)
    b = jnp.dot(h, w3_ref[...], preferred_element_type=F32)
    g = (_silu(a) * b).astype(BF16)
    o_ref[...] += jnp.dot(g, w2_ref[...], preferred_element_type=F32)


def ffn_dense(h, w1, w3, w2, tm=1024, tf=512):
    n, d = h.shape
    ff = w1.shape[1]
    return pl.pallas_call(
        _ffn_kernel,
        out_shape=jax.ShapeDtypeStruct((n, d), F32),
        grid=(n // tm, ff // tf),
        in_specs=[pl.BlockSpec((tm, d), lambda i, f: (i, 0)),
                  pl.BlockSpec((d, tf), lambda i, f: (0, f)),
                  pl.BlockSpec((d, tf), lambda i, f: (0, f)),
                  pl.BlockSpec((tf, d), lambda i, f: (f, 0))],
        out_specs=pl.BlockSpec((tm, d), lambda i, f: (i, 0)),
        compiler_params=_cparams(("parallel", "arbitrary")),
        name="ffn_dense",
    )(h, w1, w3, w2)


ROW_SLABS = D_MODEL // LANES
MOE_TM = 1024
TOP_K = 2


def _row_copy(src_ref, src_idx, dst_ref, dst_idx, sem):
    return pltpu.make_async_copy(src_ref.at[src_idx], dst_ref.at[dst_idx], sem)


def _dispatch_kernel(dest_ref, h_ref, init_ref, o_ref, sem, *, tm):
    del init_ref
    base = pl.program_id(0) * tm

    def start(t, carry):
        for kk in range(TOP_K):
            _row_copy(h_ref, t, o_ref, dest_ref[TOP_K * (base + t) + kk], sem).start()
        return carry

    def wait(t, carry):
        for kk in range(TOP_K):
            _row_copy(h_ref, t, o_ref, dest_ref[TOP_K * (base + t) + kk], sem).wait()
        return carry

    lax.fori_loop(0, tm, start, 0)
    lax.fori_loop(0, tm, wait, 0)


def moe_dispatch(dest, h_rows, n_rows, tm=256):
    n = h_rows.shape[0]
    grid_spec = pltpu.PrefetchScalarGridSpec(
        num_scalar_prefetch=1,
        grid=(n // tm,),
        in_specs=[pl.BlockSpec((tm, ROW_SLABS, LANES), lambda i, dest: (i, 0, 0)),
                  pl.BlockSpec(memory_space=pl.ANY)],
        out_specs=pl.BlockSpec(memory_space=pl.ANY),
        scratch_shapes=[pltpu.SemaphoreType.DMA],
    )
    return pl.pallas_call(
        functools.partial(_dispatch_kernel, tm=tm),
        out_shape=jax.ShapeDtypeStruct((n_rows, ROW_SLABS, LANES), BF16),
        grid_spec=grid_spec,
        input_output_aliases={2: 0},
        compiler_params=_cparams(("arbitrary",)),
        name="moe_dispatch",
    )(dest, h_rows, jnp.zeros((n_rows, ROW_SLABS, LANES), BF16))


def _moe_ffn_kernel(te_ref, nu_ref, h_ref, w1_ref, w3_ref, w2_ref, o_ref, acc_ref):
    del te_ref
    f = pl.program_id(1)
    last = pl.num_programs(1) - 1
    used = pl.program_id(0) < nu_ref[0]

    @pl.when(used & (f == 0))
    def _():
        acc_ref[...] = jnp.zeros_like(acc_ref)

    @pl.when(used)
    def _():
        h = h_ref[...]
        a = jnp.dot(h, w1_ref[...], preferred_element_type=F32)
        b = jnp.dot(h, w3_ref[...], preferred_element_type=F32)
        g = (_silu(a) * b).astype(BF16)
        acc_ref[...] += jnp.dot(g, w2_ref[...], preferred_element_type=F32)

    @pl.when(used & (f == last))
    def _():
        o_ref[...] = acc_ref[...].astype(o_ref.dtype)

    @pl.when(jnp.logical_not(used) & (f == last))
    def _():
        o_ref[...] = jnp.zeros_like(o_ref)


def moe_grouped_ffn(tile_expert, n_used, h_sorted, w1, w3, w2, tm=MOE_TM, tf=512):
    r, d = h_sorted.shape
    ff = w1.shape[2]
    nf = ff // tf

    def f_idx(i, f, nu):
        return jnp.where(i < nu[0], f, nf - 1)

    grid_spec = pltpu.PrefetchScalarGridSpec(
        num_scalar_prefetch=2,
        grid=(r // tm, nf),
        in_specs=[pl.BlockSpec((tm, d), lambda i, f, te, nu: (i, 0)),
                  pl.BlockSpec((None, d, tf), lambda i, f, te, nu: (te[i], 0, f_idx(i, f, nu))),
                  pl.BlockSpec((None, d, tf), lambda i, f, te, nu: (te[i], 0, f_idx(i, f, nu))),
                  pl.BlockSpec((None, tf, d), lambda i, f, te, nu: (te[i], f_idx(i, f, nu), 0))],
        out_specs=pl.BlockSpec((tm, d), lambda i, f, te, nu: (i, 0)),
        scratch_shapes=[pltpu.VMEM((tm, d), F32)],
    )
    return pl.pallas_call(
        _moe_ffn_kernel,
        out_shape=jax.ShapeDtypeStruct((r, d), BF16),
        grid_spec=grid_spec,
        compiler_params=_cparams(("arbitrary", "arbitrary")),
        name="moe_grouped_ffn",
    )(tile_expert, n_used, h_sorted, w1, w3, w2)


def _combine_kernel(dest_ref, w_ref, y_ref, o_ref, buf_ref, sems, *, tm):
    base = pl.program_id(0) * tm

    def start(t, carry):
        for kk in range(TOP_K):
            _row_copy(y_ref, dest_ref[TOP_K * (base + t) + kk], buf_ref.at[kk], t,
                      sems.at[kk]).start()
        return carry

    def wait(t, carry):
        for kk in range(TOP_K):
            _row_copy(y_ref, dest_ref[TOP_K * (base + t) + kk], buf_ref.at[kk], t,
                      sems.at[kk]).wait()
        return carry

    lax.fori_loop(0, tm, start, 0)
    lax.fori_loop(0, tm, wait, 0)
    acc = buf_ref[0].astype(F32) * w_ref[0]
    for kk in range(1, TOP_K):
        acc += buf_ref[kk].astype(F32) * w_ref[kk]
    o_ref[...] = acc


def moe_combine(dest, w_rows, y_rows, n, tm=256):
    grid_spec = pltpu.PrefetchScalarGridSpec(
        num_scalar_prefetch=1,
        grid=(n // tm,),
        in_specs=[pl.BlockSpec((TOP_K, tm, 1, LANES), lambda i, dest: (0, i, 0, 0)),
                  pl.BlockSpec(memory_space=pl.ANY)],
        out_specs=pl.BlockSpec((tm, ROW_SLABS, LANES), lambda i, dest: (i, 0, 0)),
        scratch_shapes=[pltpu.VMEM((TOP_K, tm, ROW_SLABS, LANES), BF16),
                        pltpu.SemaphoreType.DMA((TOP_K,))],
    )
    return pl.pallas_call(
        functools.partial(_combine_kernel, tm=tm),
        out_shape=jax.ShapeDtypeStruct((n, ROW_SLABS, LANES), F32),
        grid_spec=grid_spec,
        compiler_params=_cparams(("arbitrary",)),
        name="moe_combine",
    )(dest, w_rows, y_rows)


def moe_sparse(h2, info, counts, w1, w3, w2):
    n, d = h2.shape
    tm = MOE_TM
    n_tiles = (TOP_K * n) // tm + N_EXPERTS
    n_rows = n_tiles * tm
    cnt = counts[0, :N_EXPERTS].astype(jnp.int32)
    tiles_per = (cnt + tm - 1) // tm
    tile_end = jnp.cumsum(tiles_per)
    group_start = (tile_end - tiles_per) * tm
    n_used = tile_end[-1:]
    tile_expert = jnp.minimum(
        jnp.searchsorted(tile_end, jnp.arange(n_tiles, dtype=jnp.int32), side="right"),
        N_EXPERTS - 1).astype(jnp.int32)
    tile_expert = jnp.where(jnp.arange(n_tiles) < n_used[0], tile_expert,
                            tile_expert[jnp.maximum(n_used[0] - 1, 0)])
    experts = info[:, INFO_E1:INFO_E2 + 1].astype(jnp.int32)
    ranks = info[:, INFO_R1:INFO_R2 + 1].astype(jnp.int32)
    dest = (group_start[experts] + ranks).reshape(TOP_K * n)
    w_rows = jnp.broadcast_to(info[:, INFO_W1:INFO_W2 + 1].T[:, :, None, None],
                              (TOP_K, n, 1, LANES))

    h_sorted = moe_dispatch(dest, h2.reshape(n, ROW_SLABS, LANES), n_rows)
    y_sorted = moe_grouped_ffn(tile_expert, n_used, h_sorted.reshape(n_rows, d), w1, w3, w2)
    y = moe_combine(dest, w_rows, y_sorted.reshape(n_rows, ROW_SLABS, LANES), n)
    return y.reshape(n, d)


def _ple_kernel(*refs, n_terms, with_norm):
    term_refs = refs[:n_terms]
    p_ref, wg_ref, wp_ref = refs[n_terms:n_terms + 3]
    rest = refs[n_terms + 3:]
    if with_norm:
        g_ref, o_ref, h_ref = rest
    else:
        (o_ref,) = rest
    x = term_refs[0][...]
    for r in term_refs[1:]:
        x = x + r[...]
    xb = x.astype(BF16)
    pb = p_ref[...].astype(BF16)
    d = x.shape[1]
    sum_sq = jnp.zeros((x.shape[0], 1), F32)
    for c0 in range(0, d, PLE_COL_CHUNK):
        cs = slice(c0, c0 + PLE_COL_CHUNK)
        gate = _sigmoid(jnp.dot(xb, wg_ref[:, cs], preferred_element_type=F32))
        proj = jnp.dot(pb, wp_ref[:, cs], preferred_element_type=F32)
        xc = x[:, cs] + gate * proj
        o_ref[:, cs] = xc
        sum_sq += jnp.sum(xc * xc, axis=-1, keepdims=True)
    if with_norm:
        inv = lax.rsqrt(sum_sq * (1.0 / d) + EPS)
        h_ref[...] = (o_ref[...] * inv * g_ref[...]).astype(h_ref.dtype)


PLE_COL_CHUNK = 512


def ple_update(terms, p, w_gate, w_proj, norm_gain=None, tm=512):
    n, d = terms[0].shape
    pd = p.shape[1]
    nt = len(terms)
    with_norm = norm_gain is not None
    row = lambda width: pl.BlockSpec((tm, width), lambda i: (i, 0))
    resident = lambda shape: pl.BlockSpec(shape, lambda i: (0, 0), pipeline_mode=pl.Buffered(1))
    in_specs = [row(d)] * nt + [row(pd), resident((d, d)), resident((pd, d))]
    args = [*terms, p, w_gate, w_proj]
    out_shape = [jax.ShapeDtypeStruct((n, d), F32)]
    out_specs = [row(d)]
    if with_norm:
        in_specs.append(pl.BlockSpec((1, d), lambda i: (0, 0)))
        args.append(norm_gain.reshape(1, d))
        out_shape.append(jax.ShapeDtypeStruct((n, d), BF16))
        out_specs.append(row(d))
    res = pl.pallas_call(
        functools.partial(_ple_kernel, n_terms=nt, with_norm=with_norm),
        out_shape=tuple(out_shape),
        grid=(n // tm,),
        in_specs=in_specs,
        out_specs=tuple(out_specs),
        compiler_params=_cparams(("parallel",)),
        name="ple_update",
    )(*args)
    return res if with_norm else res[0]


def _split_w_in(w_in_i):
    c_lr = COL_GR
    main = jnp.concatenate([w_in_i[:, :c_lr], w_in_i[:, c_lr + GLA_RANK:]], axis=1).astype(BF16)
    glr = jnp.zeros((D_MODEL, LANES), BF16).at[:, :GLA_RANK].set(
        w_in_i[:, c_lr:c_lr + GLA_RANK].astype(BF16))
    return main, glr


def kernel(x, p, g_mix, w_in, w_branch_gate, b_branch_gate, pool_w, pool_scale, sb_gq, sb_gk,
           gla_w_lr, gla_b_lr, gla_g_out, w_up_pool, w_up_sb, w_up_gla, w_o, g_ffn,
           ffn_w1, ffn_w3, ffn_w2, moe_router, moe_w1, moe_w3, moe_w2, ple_w_proj, ple_w_gate):
    bsz, seq, d = x.shape
    n = bsz * seq
    depth = w_in.shape[0]
    xf = x.reshape(n, d)
    h = rmsnorm(xf, g_mix[0])
    for i in range(depth):
        w_main, w_glr = _split_w_in(w_in[i])
        w_lr_pad = jnp.zeros((LANES, GLA_K_W), F32).at[:GLA_RANK].set(gla_w_lr[i])
        qk_gain = jnp.ones((1, Z_COLS), F32)
        qk_gain = qk_gain.at[0, COL_SQ:COL_SK].set(jnp.tile(sb_gq[i] * SB_HEAD_DIM ** -0.5, SB_HEADS))
        qk_gain = qk_gain.at[0, COL_SK:COL_SV].set(jnp.tile(sb_gk[i], SB_HEADS))

        z = in_proj(h, w_main, qk_gain)
        bcum = gla_gate(h, w_glr, w_lr_pad, gla_b_lr[i])

        z3 = z.reshape(bsz, seq, Z_COLS)
        y_pool = pool_mixer(z3, pool_w[i].astype(BF16), pool_scale[i])
        y_sb = sb_attention(z3)
        y_gla = gla_mixer(z3, bcum.reshape(bsz, seq, GLA_K_W), gla_g_out[i])

        merged = merge_branches(h, w_branch_gate[i].astype(BF16), b_branch_gate[i],
                                y_pool.reshape(n, POOL_W), y_sb.reshape(n, SB_W),
                                y_gla.reshape(n, GLA_V_W), w_up_pool[i].astype(BF16),
                                w_up_sb[i].astype(BF16), w_up_gla[i].astype(BF16))

        j = i // 2
        xf, h2 = out_proj(merged, w_o[i].astype(BF16), xf, g_ffn[i])
        if i % 2 == 0:
            terms = [xf, ffn_dense(h2, ffn_w1[j].astype(BF16), ffn_w3[j].astype(BF16),
                                   ffn_w2[j].astype(BF16))]
        else:
            info, counts = route_tokens(xf, g_ffn[i], moe_router[j])
            terms = [xf, moe_sparse(h2, info, counts, moe_w1[j].astype(BF16),
                                    moe_w3[j].astype(BF16), moe_w2[j].astype(BF16))]

        if i + 1 < depth:
            xf, h = ple_update(terms, p[i].reshape(n, PLE_DIM), ple_w_gate[i].astype(BF16),
                               ple_w_proj[i].astype(BF16), norm_gain=g_mix[i + 1])
        else:
            xf = ple_update(terms, p[i].reshape(n, PLE_DIM), ple_w_gate[i].astype(BF16),
                            ple_w_proj[i].astype(BF16))
    return xf.reshape(bsz, seq, d)
```

```python
import functools

import jax
import jax.numpy as jnp
from jax import lax
from jax.experimental import pallas as pl
from jax.experimental.pallas import tpu as pltpu

F32 = jnp.float32
BF16 = jnp.bfloat16

EPS = 1e-6
D_MODEL = 2048
PLE_DIM = 256
POOL_GROUPS = 4
POOL_GROUP_W = 256
POOL_W = POOL_GROUPS * POOL_GROUP_W
POOL_WINDOWS = (2, 4, 8, 16)
SB_HEADS = 8
SB_HEAD_DIM = 128
SB_W = SB_HEADS * SB_HEAD_DIM
GLA_HEADS = 4
GLA_DK = 128
GLA_DV = 256
GLA_K_W = GLA_HEADS * GLA_DK
GLA_V_W = GLA_HEADS * GLA_DV
GLA_RANK = 16
GLA_TAU = 16.0
D_FF = 5632
N_EXPERTS = 8
N_BRANCH = 3

LANES = 128
VMEM_LIMIT = 56 * 1024 * 1024

COL_POOL = 0
COL_SQ = COL_POOL + POOL_W
COL_SK = COL_SQ + SB_W
COL_SV = COL_SK + SB_W
COL_GQ = COL_SV + SB_W
COL_GK = COL_GQ + GLA_K_W
COL_GV = COL_GK + GLA_K_W
COL_GR = COL_GV + GLA_V_W
Z_COLS = COL_GR + GLA_V_W

GLA_CHUNK = 64
GLA_SUB = 16
GLA_EXP_CAP = 80.0
SB_F32_ZERO_LOG = -104.0


def _cparams(sem):
    return pltpu.CompilerParams(dimension_semantics=sem, vmem_limit_bytes=VMEM_LIMIT)


def _log_sigmoid(z):
    return jnp.minimum(z, 0.0) - jnp.log(1.0 + jnp.exp(-jnp.abs(z)))


def _sigmoid(z):
    return 1.0 / (1.0 + jnp.exp(-z))


def _split_bf16(x):
    hi = x.astype(BF16)
    lo = (x - hi.astype(F32)).astype(BF16)
    return hi, lo


def _rmsnorm_kernel(x_ref, g_ref, o_ref):
    x = x_ref[...]
    ms = jnp.mean(x * x, axis=-1, keepdims=True)
    o_ref[...] = (x * lax.rsqrt(ms + EPS) * g_ref[...]).astype(o_ref.dtype)


def rmsnorm(x, g, tm=512):
    n, d = x.shape
    return pl.pallas_call(
        _rmsnorm_kernel,
        out_shape=jax.ShapeDtypeStruct((n, d), BF16),
        grid=(n // tm,),
        in_specs=[pl.BlockSpec((tm, d), lambda i: (i, 0)),
                  pl.BlockSpec((1, d), lambda i: (0, 0))],
        out_specs=pl.BlockSpec((tm, d), lambda i: (i, 0)),
        compiler_params=_cparams(("parallel",)),
        name="rmsnorm",
    )(x, g.reshape(1, d))


def _route_top2(h, rhi_ref, rlo_ref, info_ref, cnt_ref, carry_ref):
    tm = h.shape[0]

    @pl.when(pl.program_id(0) == 0)
    def _():
        carry_ref[...] = jnp.zeros_like(carry_ref)

    h_hi, h_lo = _split_bf16(h)
    logits = (jnp.dot(h_hi, rhi_ref[...], preferred_element_type=F32)
              + jnp.dot(h_lo, rhi_ref[...], preferred_element_type=F32)
              + jnp.dot(h_hi, rlo_ref[...], preferred_element_type=F32))
    lane = lax.broadcasted_iota(jnp.int32, logits.shape, 1).astype(F32)
    neg = jnp.float32(-jnp.inf)
    logits = jnp.where(lane < N_EXPERTS, logits, neg)
    m1 = jnp.max(logits, axis=-1, keepdims=True)
    i1 = jnp.min(jnp.where(logits == m1, lane, float(LANES)), axis=-1, keepdims=True)
    sel1 = lane == i1
    rest = jnp.where(sel1, neg, logits)
    m2 = jnp.max(rest, axis=-1, keepdims=True)
    i2 = jnp.min(jnp.where(rest == m2, lane, float(LANES)), axis=-1, keepdims=True)
    sel2 = lane == i2
    e2 = jnp.exp(m2 - m1)
    den = 1.0 + e2
    sel = (sel1 | sel2).astype(BF16)
    before = (lax.broadcasted_iota(jnp.int32, (tm, tm), 1)
              < lax.broadcasted_iota(jnp.int32, (tm, tm), 0)).astype(BF16)
    carry = carry_ref[...]
    rank = jnp.dot(before, sel, preferred_element_type=F32) + carry[0:1, :]
    r1 = jnp.sum(jnp.where(sel1, rank, 0.0), axis=-1, keepdims=True)
    r2 = jnp.sum(jnp.where(sel2, rank, 0.0), axis=-1, keepdims=True)
    fields = (i1, i2, 1.0 / den, e2 / den, r1, r2)
    info = jnp.zeros_like(logits)
    for li, val in enumerate(fields):
        info = jnp.where(lane == li, val, info)
    info_ref[...] = info
    carry = carry + jnp.sum(sel.astype(F32), axis=0, keepdims=True)
    carry_ref[...] = carry
    cnt_ref[...] = carry


INFO_E1, INFO_E2, INFO_W1, INFO_W2, INFO_R1, INFO_R2 = range(6)


def _in_proj_kernel(a_ref, w_ref, g_ref, o_ref, *, tn):
    j = pl.program_id(1)
    acc = jnp.dot(a_ref[...], w_ref[...], preferred_element_type=F32)
    is_qk = (j >= COL_SQ // tn) & (j < COL_SV // tn)

    @pl.when(is_qk)
    def _():
        for hh in range(tn // SB_HEAD_DIM):
            cs = slice(hh * SB_HEAD_DIM, (hh + 1) * SB_HEAD_DIM)
            o_ref[:, cs] = _head_rmsnorm(acc[:, cs], g_ref[:, cs]).astype(o_ref.dtype)

    @pl.when(jnp.logical_not(is_qk))
    def _():
        o_ref[...] = acc.astype(o_ref.dtype)


def in_proj(h, w, qk_gain, tm=1024, tn=512):
    n, k = h.shape
    m = w.shape[1]
    assert COL_SQ % tn == 0 and COL_SV % tn == 0 and tn % SB_HEAD_DIM == 0
    return pl.pallas_call(
        functools.partial(_in_proj_kernel, tn=tn),
        out_shape=jax.ShapeDtypeStruct((n, m), BF16),
        grid=(n // tm, m // tn),
        in_specs=[pl.BlockSpec((tm, k), lambda i, j: (i, 0)),
                  pl.BlockSpec((k, tn), lambda i, j: (0, j)),
                  pl.BlockSpec((1, tn), lambda i, j: (0, j))],
        out_specs=pl.BlockSpec((tm, tn), lambda i, j: (i, j)),
        compiler_params=_cparams(("parallel", "arbitrary")),
        name="in_proj",
    )(h, w, qk_gain)


ROW_SLABS = D_MODEL // (2 * LANES)
HALF_D = D_MODEL // 2
_HIGH_HALF = -65536


def _bf16_bits(x):
    return lax.bitcast_convert_type(x.astype(BF16).astype(F32), jnp.int32)


def _store_packed_rows(ref, read_cols, t):
    for s in range(ROW_SLABS):
        lo = (_bf16_bits(read_cols(s * LANES)) >> 16) & 0xFFFF
        hi = _bf16_bits(read_cols(HALF_D + s * LANES)) & _HIGH_HALF
        ref[pl.ds(s, t, stride=ROW_SLABS), :] = lo | hi


def _load_packed_slab(ref, s, t):
    w = ref[pl.ds(s, t, stride=ROW_SLABS), :]
    lo = lax.bitcast_convert_type(w << 16, F32)
    hi = lax.bitcast_convert_type(w & _HIGH_HALF, F32)
    return lo, hi


def _row_proj_kernel(a_ref, w_ref, r_ref, g_ref, o_ref, h_ref, *, pack_rows):
    x = r_ref[...] + jnp.dot(a_ref[...], w_ref[...], preferred_element_type=F32)
    ms = jnp.mean(x * x, axis=-1, keepdims=True)
    o_ref[...] = x
    h = x * lax.rsqrt(ms + EPS) * g_ref[...]
    if pack_rows:
        _store_packed_rows(h_ref, lambda c0: h[:, c0:c0 + LANES], x.shape[0])
    else:
        h_ref[...] = h.astype(h_ref.dtype)


def out_proj(a, w, resid, norm_gain, pack_rows=False, tm=512):
    n, k = a.shape
    d = w.shape[1]
    row = lambda width: pl.BlockSpec((tm, width), lambda i: (i, 0))
    if pack_rows:
        h_shape = jax.ShapeDtypeStruct((n * ROW_SLABS, LANES), jnp.int32)
        h_spec = pl.BlockSpec((tm * ROW_SLABS, LANES), lambda i: (i, 0))
    else:
        h_shape, h_spec = jax.ShapeDtypeStruct((n, d), BF16), row(d)
    return pl.pallas_call(
        functools.partial(_row_proj_kernel, pack_rows=pack_rows),
        out_shape=(jax.ShapeDtypeStruct((n, d), F32), h_shape),
        grid=(n // tm,),
        in_specs=[row(k), pl.BlockSpec((k, d), lambda i: (0, 0), pipeline_mode=pl.Buffered(1)),
                  row(d), pl.BlockSpec((1, d), lambda i: (0, 0))],
        out_specs=(row(d), h_spec),
        compiler_params=_cparams(("parallel",)),
        name="out_proj",
    )(a, w, resid, norm_gain.reshape(1, d))


def _router_kernel(x_ref, g_ref, rhi_ref, rlo_ref, info_ref, cnt_ref, carry_ref):
    x = x_ref[...]
    ms = jnp.mean(x * x, axis=-1, keepdims=True)
    h = x * lax.rsqrt(ms + EPS) * g_ref[...]
    _route_top2(h, rhi_ref, rlo_ref, info_ref, cnt_ref, carry_ref)


def route_tokens(x, g, router, tm=512):
    n, d = x.shape
    r_pad = jnp.zeros((d, LANES), F32).at[:, :N_EXPERTS].set(router)
    fixed = lambda shape: pl.BlockSpec(shape, lambda i: (0, 0))
    return pl.pallas_call(
        _router_kernel,
        out_shape=(jax.ShapeDtypeStruct((n, LANES), F32), jax.ShapeDtypeStruct((8, LANES), F32)),
        grid=(n // tm,),
        in_specs=[pl.BlockSpec((tm, d), lambda i: (i, 0)), fixed((1, d)),
                  fixed((d, LANES)), fixed((d, LANES))],
        out_specs=(pl.BlockSpec((tm, LANES), lambda i: (i, 0)), fixed((8, LANES))),
        scratch_shapes=[pltpu.VMEM((8, LANES), F32)],
        compiler_params=_cparams(("arbitrary",)),
        name="route_tokens",
    )(x, g.reshape(1, d), *_split_bf16(r_pad))


POOL_HALO = 128


def _pool_kernel(u_ref, w_ref, s_ref, o_ref, prev_ref, *, t):
    sb = pl.program_id(1)

    @pl.when(sb == 0)
    def _():
        prev_ref[...] = jnp.zeros_like(prev_ref)

    row = lax.broadcasted_iota(jnp.int32, (t, t), 0)
    col = lax.broadcasted_iota(jnp.int32, (t, t), 1)
    prow = lax.broadcasted_iota(jnp.int32, (t, POOL_HALO), 0)
    pcol = lax.broadcasted_iota(jnp.int32, (t, POOL_HALO), 1) - POOL_HALO
    tg = sb * t + lax.broadcasted_iota(jnp.int32, (t, 1), 0)
    for gi, w in enumerate(POOL_WINDOWS):
        cs = slice(gi * POOL_GROUP_W, (gi + 1) * POOL_GROUP_W)
        u = u_ref[:, cs]
        band_cur = ((col <= row) & (col > row - w)).astype(BF16)
        band_prev = ((pcol > prow - w) & (pcol + sb * t >= 0)).astype(BF16)
        win = jnp.dot(band_cur, u, preferred_element_type=F32)
        win = win + jnp.dot(band_prev, prev_ref[:, cs], preferred_element_type=F32)
        count = jnp.minimum(tg + 1, w).astype(F32)
        pooled = win / count - u.astype(F32)
        mixed = jnp.dot(pooled.astype(BF16), w_ref[gi], preferred_element_type=F32)
        o_ref[:, cs] = (mixed * s_ref[:, cs]).astype(o_ref.dtype)
    prev_ref[...] = u_ref[t - POOL_HALO:, :]


def pool_mixer(z3, pool_w, scale, t=256):
    b, s, _ = z3.shape
    return pl.pallas_call(
        functools.partial(_pool_kernel, t=t),
        out_shape=jax.ShapeDtypeStruct((b, s, POOL_W), BF16),
        grid=(b, s // t),
        in_specs=[pl.BlockSpec((None, t, POOL_W), lambda bi, si: (bi, si, COL_POOL // POOL_W)),
                  pl.BlockSpec((POOL_GROUPS, POOL_GROUP_W, POOL_GROUP_W), lambda bi, si: (0, 0, 0)),
                  pl.BlockSpec((1, POOL_W), lambda bi, si: (0, 0))],
        out_specs=pl.BlockSpec((None, t, POOL_W), lambda bi, si: (bi, si, 0)),
        scratch_shapes=[pltpu.VMEM((POOL_HALO, POOL_W), BF16)],
        compiler_params=_cparams(("parallel", "arbitrary")),
        name="pool_mixer",
    )(z3, pool_w, scale.reshape(1, POOL_W))


def _head_rmsnorm(x, g):
    ms = jnp.mean(x * x, axis=-1, keepdims=True)
    return x * lax.rsqrt(ms + EPS) * g


def _sb_kernel(q_ref, k_ref, v_ref, o_ref, acc_ref, c_ref, *, t, nh):
    i = pl.program_id(2)
    hd = SB_HEAD_DIM
    row = lax.broadcasted_iota(jnp.int32, (t, t), 0)
    col = lax.broadcasted_iota(jnp.int32, (t, t), 1)
    diag_mask = col < row
    suffix = (row > col).astype(BF16)
    suffix2 = jnp.concatenate([suffix, suffix], axis=0)
    has_prev = jnp.broadcast_to(i > 0, (t, t))

    def block(h, start, mask, c):
        hs = slice(h * hd, (h + 1) * hd)
        k = k_ref[pl.ds(start, t), hs]
        z = lax.dot_general(q_ref[:, hs], k, (((1,), (1,)), ((), ())),
                            preferred_element_type=F32)
        soft = jnp.log(1.0 + jnp.exp(-jnp.abs(z)))
        neg_part = jnp.minimum(z, 0.0)
        log_beta = neg_part - soft
        log_keep = (neg_part - z) - soft
        if mask is not None:
            log_keep = jnp.where(mask, log_keep, 0.0)
        hi, lo = _split_bf16(log_keep)
        later = jnp.dot(jnp.concatenate([hi, lo], axis=1), suffix2, preferred_element_type=F32)
        a = jnp.exp(log_beta + later + c)
        if mask is not None:
            a = jnp.where(mask, a, 0.0)
        out = jnp.dot(a.astype(BF16), v_ref[pl.ds(start, t), hs], preferred_element_type=F32)
        return out, c + later[:, 0:1] + log_keep[:, 0:1]

    diag_start = pl.multiple_of(i * t, t)
    prev_start = pl.multiple_of(jnp.maximum(i - 1, 0) * t, t)
    live = jnp.int32(0)
    for h in range(nh):
        out_a, c = block(h, diag_start, diag_mask, jnp.zeros((t, 1), F32))
        out_b, c = block(h, prev_start, has_prev, c)
        acc_ref[:, h * hd:(h + 1) * hd] = out_a + out_b
        c_ref[h] = c
        live = jnp.maximum(live, (jnp.max(c) > SB_F32_ZERO_LOG).astype(jnp.int32))

    def body(carry):
        j, _ = carry
        start = pl.multiple_of(j * t, t)
        live = jnp.int32(0)
        for h in range(nh):
            out, c = block(h, start, None, c_ref[h])
            acc_ref[:, h * hd:(h + 1) * hd] += out
            c_ref[h] = c
            live = jnp.maximum(live, (jnp.max(c) > SB_F32_ZERO_LOG).astype(jnp.int32))
        return j - 1, live

    def cond(carry):
        j, live = carry
        return jnp.logical_and(j >= 0, live > 0)

    lax.while_loop(cond, body, (i - 2, live))
    o_ref[...] = acc_ref[...].astype(o_ref.dtype)


def sb_attention(z3, t=256, nh=2):
    b, s, _ = z3.shape
    w = nh * SB_HEAD_DIM
    return pl.pallas_call(
        functools.partial(_sb_kernel, t=t, nh=nh),
        out_shape=jax.ShapeDtypeStruct((b, s, SB_W), BF16),
        grid=(b, SB_HEADS // nh, s // t),
        in_specs=[pl.BlockSpec((None, t, w), lambda bi, h, i: (bi, i, COL_SQ // w + h)),
                  pl.BlockSpec((None, s, w), lambda bi, h, i: (bi, 0, COL_SK // w + h)),
                  pl.BlockSpec((None, s, w), lambda bi, h, i: (bi, 0, COL_SV // w + h))],
        out_specs=pl.BlockSpec((None, t, w), lambda bi, h, i: (bi, i, h)),
        scratch_shapes=[pltpu.VMEM((t, w), F32), pltpu.VMEM((nh, t, 1), F32)],
        compiler_params=_cparams(("parallel", "parallel", "arbitrary")),
        name="sb_attention",
    )(z3, z3, z3)


def _gla_gate_kernel(h_ref, wg_ref, wlr_hi_ref, wlr_lo_ref, blr_ref, o_ref, *, tm):
    g = jnp.dot(h_ref[...], wg_ref[...], preferred_element_type=F32)
    g_hi, g_lo = _split_bf16(g)
    pre = (jnp.dot(g_hi, wlr_hi_ref[...], preferred_element_type=F32)
           + jnp.dot(g_lo, wlr_hi_ref[...], preferred_element_type=F32)
           + jnp.dot(g_hi, wlr_lo_ref[...], preferred_element_type=F32)) + blr_ref[...]
    log_a = _log_sigmoid(pre) * (1.0 / GLA_TAU)
    row = lax.broadcasted_iota(jnp.int32, (tm, tm), 0)
    col = lax.broadcasted_iota(jnp.int32, (tm, tm), 1)
    shift = GLA_CHUNK.bit_length() - 1
    tri = ((col <= row) & ((col >> shift) == (row >> shift))).astype(BF16)
    hi, lo = _split_bf16(log_a)
    o_ref[...] = (jnp.dot(tri, hi, preferred_element_type=F32)
                  + jnp.dot(tri, lo, preferred_element_type=F32))


def gla_gate(h, w_glr, w_lr, b_lr, tm=256):
    n, d = h.shape
    return pl.pallas_call(
        functools.partial(_gla_gate_kernel, tm=tm),
        out_shape=jax.ShapeDtypeStruct((n, GLA_K_W), F32),
        grid=(n // tm,),
        in_specs=[pl.BlockSpec((tm, d), lambda i: (i, 0)),
                  pl.BlockSpec((d, LANES), lambda i: (0, 0)),
                  pl.BlockSpec((LANES, GLA_K_W), lambda i: (0, 0)),
                  pl.BlockSpec((LANES, GLA_K_W), lambda i: (0, 0)),
                  pl.BlockSpec((1, GLA_K_W), lambda i: (0, 0))],
        out_specs=pl.BlockSpec((tm, GLA_K_W), lambda i: (i, 0)),
        compiler_params=_cparams(("parallel",)),
        name="gla_gate",
    )(h, w_glr, *_split_bf16(w_lr), b_lr.reshape(1, GLA_K_W))


def _gla_kernel(q_ref, k_ref, v_ref, r_ref, b_ref, g_ref, o_ref, st_ref, *, tc):
    @pl.when(pl.program_id(2) == 0)
    def _():
        st_ref[...] = jnp.zeros_like(st_ref)

    c = GLA_CHUNK
    sub = GLA_SUB
    causal = (lax.broadcasted_iota(jnp.int32, (c, c), 1)
              <= lax.broadcasted_iota(jnp.int32, (c, c), 0))
    g_out = g_ref[...]

    def chunk(ci, carry):
        base = pl.multiple_of(ci * c, c)
        rows = pl.ds(base, c)
        q = q_ref[rows, :].astype(F32) * (GLA_DK ** -0.5)
        k = k_ref[rows, :].astype(F32)
        v = v_ref[rows, :]
        b = b_ref[rows, :]
        b_end = b_ref[pl.ds(base + c - 1, 1), :]
        st = st_ref[...]
        inter = lax.dot_general((q * jnp.exp(b)).astype(BF16), st.astype(BF16),
                                (((1,), (1,)), ((), ())), preferred_element_type=F32)
        scores = []
        for si in range(c // sub):
            lo, hi = si * sub, (si + 1) * sub
            if si > 0:
                ref = b_ref[pl.ds(base + lo - 1, 1), :]
            else:
                ref = jnp.zeros((1, GLA_DK), F32)
            q_t = (q[lo:hi] * jnp.exp(b[lo:hi] - ref)).astype(BF16)
            k_t = (k * jnp.exp(jnp.minimum(ref - b, GLA_EXP_CAP))).astype(BF16)
            scores.append(lax.dot_general(q_t, k_t, (((1,), (1,)), ((), ())),
                                          preferred_element_type=F32))
        sc = jnp.where(causal, jnp.concatenate(scores, axis=0), 0.0)
        o = inter + jnp.dot(sc.astype(BF16), v, preferred_element_type=F32)
        k_e = (k * jnp.exp(b_end - b)).astype(BF16)
        v_t = v.astype(F32).T.astype(BF16)
        st_ref[...] = st * jnp.exp(b_end) + jnp.dot(v_t, k_e, preferred_element_type=F32)
        ms = jnp.mean(o * o, axis=-1, keepdims=True)
        o = o * lax.rsqrt(ms + EPS) * g_out
        r = r_ref[rows, :].astype(F32)
        o_ref[rows, :] = (o * (r * _sigmoid(r))).astype(o_ref.dtype)
        return carry

    lax.fori_loop(0, tc // c, chunk, 0, unroll=True)


def gla_mixer(z3, bcum3, g_out, tc=256):
    b, s, _ = z3.shape
    dk, dv = GLA_DK, GLA_DV
    return pl.pallas_call(
        functools.partial(_gla_kernel, tc=tc),
        out_shape=jax.ShapeDtypeStruct((b, s, GLA_V_W), BF16),
        grid=(b, GLA_HEADS, s // tc),
        in_specs=[pl.BlockSpec((None, tc, dk), lambda bi, h, i: (bi, i, COL_GQ // dk + h)),
                  pl.BlockSpec((None, tc, dk), lambda bi, h, i: (bi, i, COL_GK // dk + h)),
                  pl.BlockSpec((None, tc, dv), lambda bi, h, i: (bi, i, COL_GV // dv + h)),
                  pl.BlockSpec((None, tc, dv), lambda bi, h, i: (bi, i, COL_GR // dv + h)),
                  pl.BlockSpec((None, tc, dk), lambda bi, h, i: (bi, i, h)),
                  pl.BlockSpec((1, dv), lambda bi, h, i: (0, 0))],
        out_specs=pl.BlockSpec((None, tc, dv), lambda bi, h, i: (bi, i, h)),
        scratch_shapes=[pltpu.VMEM((dv, dk), F32)],
        compiler_params=_cparams(("parallel", "parallel", "arbitrary")),
        name="gla_mixer",
    )(z3, z3, z3, z3, bcum3, g_out.reshape(1, dv))


def _merge_kernel(h_ref, wga_ref, wgb_ref, wgc_ref, ba_ref, bb_ref, bc_ref,
                  yp_ref, ys_ref, yg_ref, wp_ref, ws_ref, wg_ref, o_ref):
    h = h_ref[...]
    m = None
    for wgate_ref, b_ref, y_ref, wup_ref in ((wga_ref, ba_ref, yp_ref, wp_ref),
                                             (wgb_ref, bb_ref, ys_ref, ws_ref),
                                             (wgc_ref, bc_ref, yg_ref, wg_ref)):
        gate = _sigmoid(jnp.dot(h, wgate_ref[...], preferred_element_type=F32) + b_ref[...])
        term = gate * jnp.dot(y_ref[...], wup_ref[...], preferred_element_type=F32)
        m = term if m is None else m + term
    o_ref[...] = m.astype(o_ref.dtype)


def merge_branches(h, w_gate, b_gate, y_pool, y_sb, y_gla, w_p, w_s, w_g, tm=1024, tn=512):
    n, d = h.shape
    nj = d // tn
    gate_w_spec = lambda br: pl.BlockSpec((d, tn), lambda i, j: (0, br * nj + j))
    gate_b_spec = lambda br: pl.BlockSpec((1, tn), lambda i, j: (0, br * nj + j))
    y_spec = lambda w: pl.BlockSpec((tm, w), lambda i, j: (i, 0))
    w_spec = lambda w: pl.BlockSpec((w, tn), lambda i, j: (0, j))
    b_gate = b_gate.reshape(1, N_BRANCH * d)
    return pl.pallas_call(
        _merge_kernel,
        out_shape=jax.ShapeDtypeStruct((n, d), BF16),
        grid=(n // tm, nj),
        in_specs=[y_spec(d), gate_w_spec(0), gate_w_spec(1), gate_w_spec(2),
                  gate_b_spec(0), gate_b_spec(1), gate_b_spec(2),
                  y_spec(POOL_W), y_spec(SB_W), y_spec(GLA_V_W),
                  w_spec(POOL_W), w_spec(SB_W), w_spec(GLA_V_W)],
        out_specs=pl.BlockSpec((tm, tn), lambda i, j: (i, j)),
        compiler_params=_cparams(("parallel", "arbitrary")),
        name="merge_branches",
    )(h, w_gate, w_gate, w_gate, b_gate, b_gate, b_gate, y_pool, y_sb, y_gla, w_p, w_s, w_g)


def _silu(a):
    return a * _sigmoid(a)


def _ffn_kernel(h_ref, w1_ref, w3_ref, w2_ref, o_ref):
    @pl.when(pl.program_id(1) == 0)
    def _():
        o_ref[...] = jnp.zeros_like(o_ref)

    h = h_ref[...]
    a = jnp.dot(h, w1_ref[...], preferred_element_type=F32)
    b = jnp.dot(h, w3_ref[...], preferred_element_type=F32)
    g = (_silu(a) * b).astype(BF16)
    o_ref[...] += jnp.dot(g, w2_ref[...], preferred_element_type=F32)


def ffn_dense(h, w1, w3, w2, tm=1024, tf=512):
    n, d = h.shape
    ff = w1.shape[1]
    return pl.pallas_call(
        _ffn_kernel,
        out_shape=jax.ShapeDtypeStruct((n, d), F32),
        grid=(n // tm, ff // tf),
        in_specs=[pl.BlockSpec((tm, d), lambda i, f: (i, 0)),
                  pl.BlockSpec((d, tf), lambda i, f: (0, f)),
                  pl.BlockSpec((d, tf), lambda i, f: (0, f)),
                  pl.BlockSpec((tf, d), lambda i, f: (f, 0))],
        out_specs=pl.BlockSpec((tm, d), lambda i, f: (i, 0)),
        compiler_params=_cparams(("parallel", "arbitrary")),
        name="ffn_dense",
    )(h, w1, w3, w2)


MOE_TM = 1024
TOP_K = 2


def _row_copy(src_ref, src_idx, dst_ref, dst_idx, sem):
    return pltpu.make_async_copy(src_ref.at[src_idx], dst_ref.at[dst_idx], sem)


def _dispatch_kernel(dest_ref, h_ref, init_ref, o_ref, sem, *, tm):
    del init_ref
    i = pl.program_id(0)

    def copies(tile, t):
        tok = tile * tm + t
        return [_row_copy(h_ref, tok, o_ref, dest_ref[TOP_K * tok + kk], sem)
                for kk in range(TOP_K)]

    def start(t, carry):
        for cp in copies(i, t):
            cp.start()
        return carry

    def wait_tile(tile):
        def body(t, carry):
            for cp in copies(tile, t):
                cp.wait()
            return carry
        lax.fori_loop(0, tm, body, 0)

    lax.fori_loop(0, tm, start, 0)

    @pl.when(i > 0)
    def _():
        wait_tile(i - 1)

    @pl.when(i == pl.num_programs(0) - 1)
    def _():
        wait_tile(i)


def moe_dispatch(dest, h_rows, n_rows, tm=256):
    n = h_rows.shape[0]
    grid_spec = pltpu.PrefetchScalarGridSpec(
        num_scalar_prefetch=1,
        grid=(n // tm,),
        in_specs=[pl.BlockSpec(memory_space=pl.ANY),
                  pl.BlockSpec(memory_space=pl.ANY)],
        out_specs=pl.BlockSpec(memory_space=pl.ANY),
        scratch_shapes=[pltpu.SemaphoreType.DMA],
    )
    return pl.pallas_call(
        functools.partial(_dispatch_kernel, tm=tm),
        out_shape=jax.ShapeDtypeStruct((n_rows, ROW_SLABS, LANES), h_rows.dtype),
        grid_spec=grid_spec,
        input_output_aliases={2: 0},
        compiler_params=_cparams(("arbitrary",)),
        name="moe_dispatch",
    )(dest, h_rows, jnp.zeros((n_rows, ROW_SLABS, LANES), h_rows.dtype))


def _moe_ffn_kernel(te_ref, nu_ref, h_ref, w1_ref, w3_ref, w2_ref, o_ref, hb_ref, acc_ref, *, tm):
    del te_ref
    f = pl.program_id(1)
    last = pl.num_programs(1) - 1
    used = pl.program_id(0) < nu_ref[0]

    @pl.when(used & (f == 0))
    def _():
        acc_ref[...] = jnp.zeros_like(acc_ref)
        for s in range(ROW_SLABS):
            lo, hi = _load_packed_slab(h_ref, s, tm)
            hb_ref[:, s * LANES:(s + 1) * LANES] = lo.astype(BF16)
            hb_ref[:, HALF_D + s * LANES:HALF_D + (s + 1) * LANES] = hi.astype(BF16)

    @pl.when(used)
    def _():
        h = hb_ref[...]
        a = jnp.dot(h, w1_ref[...], preferred_element_type=F32)
        b = jnp.dot(h, w3_ref[...], preferred_element_type=F32)
        g = (_silu(a) * b).astype(BF16)
        acc_ref[...] += jnp.dot(g, w2_ref[...], preferred_element_type=F32)

    @pl.when(used & (f == last))
    def _():
        _store_packed_rows(o_ref, lambda c0: acc_ref[:, c0:c0 + LANES], tm)

    @pl.when(jnp.logical_not(used) & (f == last))
    def _():
        o_ref[...] = jnp.zeros_like(o_ref)


def moe_grouped_ffn(tile_expert, n_used, h_sorted, w1, w3, w2, tm=MOE_TM, tf=512):
    r = h_sorted.shape[0] // ROW_SLABS
    d, ff = w1.shape[1], w1.shape[2]
    nf = ff // tf

    def f_idx(i, f, nu):
        return jnp.where(i < nu[0], f, nf - 1)

    rows = pl.BlockSpec((tm * ROW_SLABS, LANES), lambda i, f, te, nu: (i, 0))
    grid_spec = pltpu.PrefetchScalarGridSpec(
        num_scalar_prefetch=2,
        grid=(r // tm, nf),
        in_specs=[rows,
                  pl.BlockSpec((None, d, tf), lambda i, f, te, nu: (te[i], 0, f_idx(i, f, nu))),
                  pl.BlockSpec((None, d, tf), lambda i, f, te, nu: (te[i], 0, f_idx(i, f, nu))),
                  pl.BlockSpec((None, tf, d), lambda i, f, te, nu: (te[i], f_idx(i, f, nu), 0))],
        out_specs=rows,
        scratch_shapes=[pltpu.VMEM((tm, d), BF16), pltpu.VMEM((tm, d), F32)],
    )
    return pl.pallas_call(
        functools.partial(_moe_ffn_kernel, tm=tm),
        out_shape=jax.ShapeDtypeStruct(h_sorted.shape, jnp.int32),
        grid_spec=grid_spec,
        compiler_params=_cparams(("arbitrary", "arbitrary")),
        name="moe_grouped_ffn",
    )(tile_expert, n_used, h_sorted, w1, w3, w2)


def _combine_kernel(dest_ref, w_ref, y_ref, o_ref, buf_ref, sems, *, tm):
    i = pl.program_id(0)
    slot = i % 2

    def copies(tile, t):
        tok = tile * tm + t
        sl = tile % 2
        return [_row_copy(y_ref, dest_ref[TOP_K * tok + kk], buf_ref.at[sl, kk], t, sems.at[sl, kk])
                for kk in range(TOP_K)]

    def start_tile(tile):
        def body(t, carry):
            for cp in copies(tile, t):
                cp.start()
            return carry
        lax.fori_loop(0, tm, body, 0)

    def wait_tile(tile):
        def body(t, carry):
            for cp in copies(tile, t):
                cp.wait()
            return carry
        lax.fori_loop(0, tm, body, 0)

    @pl.when(i == 0)
    def _():
        start_tile(i)

    @pl.when(i + 1 < pl.num_programs(0))
    def _():
        start_tile(i + 1)

    wait_tile(i)
    lo = hi = None
    for kk in range(TOP_K):
        words = buf_ref[slot, kk]
        w = w_ref[kk]
        lo_k = lax.bitcast_convert_type(words << 16, F32) * w
        hi_k = lax.bitcast_convert_type(words & _HIGH_HALF, F32) * w
        lo = lo_k if lo is None else lo + lo_k
        hi = hi_k if hi is None else hi + hi_k
    o_ref[0] = lo
    o_ref[1] = hi


def moe_combine(dest, w_rows, y_rows, n, tm=256):
    grid_spec = pltpu.PrefetchScalarGridSpec(
        num_scalar_prefetch=1,
        grid=(n // tm,),
        in_specs=[pl.BlockSpec((TOP_K, tm, 1, LANES), lambda i, dest: (0, i, 0, 0)),
                  pl.BlockSpec(memory_space=pl.ANY)],
        out_specs=pl.BlockSpec((2, tm, ROW_SLABS, LANES), lambda i, dest: (0, i, 0, 0)),
        scratch_shapes=[pltpu.VMEM((2, TOP_K, tm, ROW_SLABS, LANES), jnp.int32),
                        pltpu.SemaphoreType.DMA((2, TOP_K))],
    )
    return pl.pallas_call(
        functools.partial(_combine_kernel, tm=tm),
        out_shape=jax.ShapeDtypeStruct((2, n, ROW_SLABS, LANES), F32),
        grid_spec=grid_spec,
        compiler_params=_cparams(("arbitrary",)),
        name="moe_combine",
    )(dest, w_rows, y_rows)


def moe_sparse(h_rows, info, counts, w1, w3, w2):
    n = h_rows.shape[0] // ROW_SLABS
    tm = MOE_TM
    n_tiles = (TOP_K * n) // tm + N_EXPERTS
    n_rows = n_tiles * tm
    cnt = counts[0, :N_EXPERTS].astype(jnp.int32)
    tiles_per = (cnt + tm - 1) // tm
    tile_end = jnp.cumsum(tiles_per)
    group_start = (tile_end - tiles_per) * tm
    n_used = tile_end[-1:]
    tile_expert = jnp.minimum(
        jnp.searchsorted(tile_end, jnp.arange(n_tiles, dtype=jnp.int32), side="right"),
        N_EXPERTS - 1).astype(jnp.int32)
    tile_expert = jnp.where(jnp.arange(n_tiles) < n_used[0], tile_expert,
                            tile_expert[jnp.maximum(n_used[0] - 1, 0)])
    experts = info[:, INFO_E1:INFO_E2 + 1].astype(jnp.int32)
    ranks = info[:, INFO_R1:INFO_R2 + 1].astype(jnp.int32)
    dest = (group_start[experts] + ranks).reshape(TOP_K * n)
    w_rows = jnp.broadcast_to(info[:, INFO_W1:INFO_W2 + 1].T[:, :, None, None],
                              (TOP_K, n, 1, LANES))

    h_sorted = moe_dispatch(dest, h_rows.reshape(n, ROW_SLABS, LANES), n_rows)
    y_sorted = moe_grouped_ffn(tile_expert, n_used,
                               h_sorted.reshape(n_rows * ROW_SLABS, LANES), w1, w3, w2)
    y = moe_combine(dest, w_rows, y_sorted.reshape(n_rows, ROW_SLABS, LANES), n)
    return y.reshape(2, n * ROW_SLABS, LANES)


def _ple_kernel(x_ref, y_ref, p_ref, wg_ref, wp_ref, *rest, y_slabs, with_norm):
    if with_norm:
        g_ref, o_ref, h_ref = rest
    else:
        (o_ref,) = rest
    if y_slabs:
        t = x_ref.shape[0]
        parts = []
        for half in range(2):
            for s in range(ROW_SLABS):
                parts.append(y_ref[half, pl.ds(s, t, stride=ROW_SLABS), :])
        x = x_ref[...] + jnp.concatenate(parts, axis=1)
    else:
        x = x_ref[...] + y_ref[...]
    xb = x.astype(BF16)
    pb = p_ref[...].astype(BF16)
    d = x.shape[1]
    sum_sq = jnp.zeros((x.shape[0], 1), F32)
    for c0 in range(0, d, PLE_COL_CHUNK):
        cs = slice(c0, c0 + PLE_COL_CHUNK)
        gate = _sigmoid(jnp.dot(xb, wg_ref[:, cs], preferred_element_type=F32))
        proj = jnp.dot(pb, wp_ref[:, cs], preferred_element_type=F32)
        xc = x[:, cs] + gate * proj
        o_ref[:, cs] = xc
        sum_sq += jnp.sum(xc * xc, axis=-1, keepdims=True)
    if with_norm:
        inv = lax.rsqrt(sum_sq * (1.0 / d) + EPS)
        h_ref[...] = (o_ref[...] * inv * g_ref[...]).astype(h_ref.dtype)


PLE_COL_CHUNK = 512


def ple_update(x, y, p, w_gate, w_proj, norm_gain=None, tm=512):
    n, d = x.shape
    pd = p.shape[1]
    y_slabs = y.ndim == 3
    with_norm = norm_gain is not None
    row = lambda width: pl.BlockSpec((tm, width), lambda i: (i, 0))
    resident = lambda shape: pl.BlockSpec(shape, lambda i: (0, 0), pipeline_mode=pl.Buffered(1))
    y_spec = (pl.BlockSpec((2, tm * ROW_SLABS, LANES), lambda i: (0, i, 0)) if y_slabs else row(d))
    in_specs = [row(d), y_spec, row(pd), resident((d, d)), resident((pd, d))]
    args = [x, y, p, w_gate, w_proj]
    out_shape = [jax.ShapeDtypeStruct((n, d), F32)]
    out_specs = [row(d)]
    if with_norm:
        in_specs.append(pl.BlockSpec((1, d), lambda i: (0, 0)))
        args.append(norm_gain.reshape(1, d))
        out_shape.append(jax.ShapeDtypeStruct((n, d), BF16))
        out_specs.append(row(d))
    res = pl.pallas_call(
        functools.partial(_ple_kernel, y_slabs=y_slabs, with_norm=with_norm),
        out_shape=tuple(out_shape),
        grid=(n // tm,),
        in_specs=in_specs,
        out_specs=tuple(out_specs),
        compiler_params=_cparams(("parallel",)),
        name="ple_update",
    )(*args)
    return res if with_norm else res[0]


def _split_w_in(w_in_i):
    c_lr = COL_GR
    main = jnp.concatenate([w_in_i[:, :c_lr], w_in_i[:, c_lr + GLA_RANK:]], axis=1).astype(BF16)
    glr = jnp.zeros((D_MODEL, LANES), BF16).at[:, :GLA_RANK].set(
        w_in_i[:, c_lr:c_lr + GLA_RANK].astype(BF16))
    return main, glr


def kernel(x, p, g_mix, w_in, w_branch_gate, b_branch_gate, pool_w, pool_scale, sb_gq, sb_gk,
           gla_w_lr, gla_b_lr, gla_g_out, w_up_pool, w_up_sb, w_up_gla, w_o, g_ffn,
           ffn_w1, ffn_w3, ffn_w2, moe_router, moe_w1, moe_w3, moe_w2, ple_w_proj, ple_w_gate):
    bsz, seq, d = x.shape
    n = bsz * seq
    depth = w_in.shape[0]
    xf = x.reshape(n, d)
    h = rmsnorm(xf, g_mix[0])
    for i in range(depth):
        w_main, w_glr = _split_w_in(w_in[i])
        w_lr_pad = jnp.zeros((LANES, GLA_K_W), F32).at[:GLA_RANK].set(gla_w_lr[i])
        qk_gain = jnp.ones((1, Z_COLS), F32)
        qk_gain = qk_gain.at[0, COL_SQ:COL_SK].set(jnp.tile(sb_gq[i] * SB_HEAD_DIM ** -0.5, SB_HEADS))
        qk_gain = qk_gain.at[0, COL_SK:COL_SV].set(jnp.tile(sb_gk[i], SB_HEADS))

        z = in_proj(h, w_main, qk_gain)
        bcum = gla_gate(h, w_glr, w_lr_pad, gla_b_lr[i])

        z3 = z.reshape(bsz, seq, Z_COLS)
        y_pool = pool_mixer(z3, pool_w[i].astype(BF16), pool_scale[i])
        y_sb = sb_attention(z3)
        y_gla = gla_mixer(z3, bcum.reshape(bsz, seq, GLA_K_W), gla_g_out[i])

        merged = merge_branches(h, w_branch_gate[i].astype(BF16), b_branch_gate[i],
                                y_pool.reshape(n, POOL_W), y_sb.reshape(n, SB_W),
                                y_gla.reshape(n, GLA_V_W), w_up_pool[i].astype(BF16),
                                w_up_sb[i].astype(BF16), w_up_gla[i].astype(BF16))

        j = i // 2
        if i % 2 == 0:
            xf, h2 = out_proj(merged, w_o[i].astype(BF16), xf, g_ffn[i])
            y = ffn_dense(h2, ffn_w1[j].astype(BF16), ffn_w3[j].astype(BF16),
                          ffn_w2[j].astype(BF16))
        else:
            xf, h2_rows = out_proj(merged, w_o[i].astype(BF16), xf, g_ffn[i], pack_rows=True)
            info, counts = route_tokens(xf, g_ffn[i], moe_router[j])
            y = moe_sparse(h2_rows, info, counts, moe_w1[j].astype(BF16),
                           moe_w3[j].astype(BF16), moe_w2[j].astype(BF16))

        if i + 1 < depth:
            xf, h = ple_update(xf, y, p[i].reshape(n, PLE_DIM), ple_w_gate[i].astype(BF16),
                               ple_w_proj[i].astype(BF16), norm_gain=g_mix[i + 1])
        else:
            xf = ple_update(xf, y, p[i].reshape(n, PLE_DIM), ple_w_gate[i].astype(BF16),
                            ple_w_proj[i].astype(BF16))
    return xf.reshape(bsz, seq, d)
```

```python
import functools

import jax
import jax.numpy as jnp
from jax import lax
from jax.experimental import pallas as pl
from jax.experimental.pallas import tpu as pltpu

F32 = jnp.float32
BF16 = jnp.bfloat16

EPS = 1e-6
D_MODEL = 2048
PLE_DIM = 256
POOL_GROUPS = 4
POOL_GROUP_W = 256
POOL_W = POOL_GROUPS * POOL_GROUP_W
POOL_WINDOWS = (2, 4, 8, 16)
SB_HEADS = 8
SB_HEAD_DIM = 128
SB_W = SB_HEADS * SB_HEAD_DIM
GLA_HEADS = 4
GLA_DK = 128
GLA_DV = 256
GLA_K_W = GLA_HEADS * GLA_DK
GLA_V_W = GLA_HEADS * GLA_DV
GLA_RANK = 16
GLA_TAU = 16.0
D_FF = 5632
N_EXPERTS = 8
N_BRANCH = 3

LANES = 128
VMEM_LIMIT = 56 * 1024 * 1024

COL_POOL = 0
COL_SQ = COL_POOL + POOL_W
COL_SK = COL_SQ + SB_W
COL_SV = COL_SK + SB_W
COL_GQ = COL_SV + SB_W
COL_GK = COL_GQ + GLA_K_W
COL_GV = COL_GK + GLA_K_W
COL_GR = COL_GV + GLA_V_W
Z_COLS = COL_GR + GLA_V_W

GLA_CHUNK = 64
GLA_SUB = 16
GLA_EXP_CAP = 80.0
SB_F32_ZERO_LOG = -104.0


def _cparams(sem):
    return pltpu.CompilerParams(dimension_semantics=sem, vmem_limit_bytes=VMEM_LIMIT)


def _log_sigmoid(z):
    return jnp.minimum(z, 0.0) - jnp.log(1.0 + jnp.exp(-jnp.abs(z)))


def _sigmoid(z):
    return 1.0 / (1.0 + jnp.exp(-z))


def _split_bf16(x):
    hi = x.astype(BF16)
    lo = (x - hi.astype(F32)).astype(BF16)
    return hi, lo


def _rmsnorm_kernel(x_ref, g_ref, o_ref):
    x = x_ref[...]
    ms = jnp.mean(x * x, axis=-1, keepdims=True)
    o_ref[...] = (x * lax.rsqrt(ms + EPS) * g_ref[...]).astype(o_ref.dtype)


def rmsnorm(x, g, tm=512):
    n, d = x.shape
    return pl.pallas_call(
        _rmsnorm_kernel,
        out_shape=jax.ShapeDtypeStruct((n, d), BF16),
        grid=(n // tm,),
        in_specs=[pl.BlockSpec((tm, d), lambda i: (i, 0)),
                  pl.BlockSpec((1, d), lambda i: (0, 0))],
        out_specs=pl.BlockSpec((tm, d), lambda i: (i, 0)),
        compiler_params=_cparams(("parallel",)),
        name="rmsnorm",
    )(x, g.reshape(1, d))


def _route_top2(h, rhi_ref, rlo_ref, info_ref, cnt_ref, carry_ref):
    tm = h.shape[0]

    @pl.when(pl.program_id(0) == 0)
    def _():
        carry_ref[...] = jnp.zeros_like(carry_ref)

    h_hi, h_lo = _split_bf16(h)
    logits = (jnp.dot(h_hi, rhi_ref[...], preferred_element_type=F32)
              + jnp.dot(h_lo, rhi_ref[...], preferred_element_type=F32)
              + jnp.dot(h_hi, rlo_ref[...], preferred_element_type=F32))
    lane = lax.broadcasted_iota(jnp.int32, logits.shape, 1).astype(F32)
    neg = jnp.float32(-jnp.inf)
    logits = jnp.where(lane < N_EXPERTS, logits, neg)
    m1 = jnp.max(logits, axis=-1, keepdims=True)
    i1 = jnp.min(jnp.where(logits == m1, lane, float(LANES)), axis=-1, keepdims=True)
    sel1 = lane == i1
    rest = jnp.where(sel1, neg, logits)
    m2 = jnp.max(rest, axis=-1, keepdims=True)
    i2 = jnp.min(jnp.where(rest == m2, lane, float(LANES)), axis=-1, keepdims=True)
    sel2 = lane == i2
    e2 = jnp.exp(m2 - m1)
    den = 1.0 + e2
    sel = (sel1 | sel2).astype(BF16)
    before = (lax.broadcasted_iota(jnp.int32, (tm, tm), 1)
              < lax.broadcasted_iota(jnp.int32, (tm, tm), 0)).astype(BF16)
    carry = carry_ref[...]
    rank = jnp.dot(before, sel, preferred_element_type=F32) + carry[0:1, :]
    r1 = jnp.sum(jnp.where(sel1, rank, 0.0), axis=-1, keepdims=True)
    r2 = jnp.sum(jnp.where(sel2, rank, 0.0), axis=-1, keepdims=True)
    fields = (i1, i2, 1.0 / den, e2 / den, r1, r2)
    info = jnp.zeros_like(logits)
    for li, val in enumerate(fields):
        info = jnp.where(lane == li, val, info)
    info_ref[...] = info
    carry = carry + jnp.sum(sel.astype(F32), axis=0, keepdims=True)
    carry_ref[...] = carry
    cnt_ref[...] = carry


INFO_E1, INFO_E2, INFO_W1, INFO_W2, INFO_R1, INFO_R2 = range(6)


def _in_proj_kernel(a_ref, w_ref, g_ref, o_ref, *, tn):
    j = pl.program_id(1)
    acc = jnp.dot(a_ref[...], w_ref[...], preferred_element_type=F32)
    is_qk = (j >= COL_SQ // tn) & (j < COL_SV // tn)

    @pl.when(is_qk)
    def _():
        for hh in range(tn // SB_HEAD_DIM):
            cs = slice(hh * SB_HEAD_DIM, (hh + 1) * SB_HEAD_DIM)
            o_ref[:, cs] = _head_rmsnorm(acc[:, cs], g_ref[:, cs]).astype(o_ref.dtype)

    @pl.when(jnp.logical_not(is_qk))
    def _():
        o_ref[...] = acc.astype(o_ref.dtype)


def in_proj(h, w, qk_gain, tm=1024, tn=512):
    n, k = h.shape
    m = w.shape[1]
    assert COL_SQ % tn == 0 and COL_SV % tn == 0 and tn % SB_HEAD_DIM == 0
    return pl.pallas_call(
        functools.partial(_in_proj_kernel, tn=tn),
        out_shape=jax.ShapeDtypeStruct((n, m), BF16),
        grid=(n // tm, m // tn),
        in_specs=[pl.BlockSpec((tm, k), lambda i, j: (i, 0)),
                  pl.BlockSpec((k, tn), lambda i, j: (0, j)),
                  pl.BlockSpec((1, tn), lambda i, j: (0, j))],
        out_specs=pl.BlockSpec((tm, tn), lambda i, j: (i, j)),
        compiler_params=_cparams(("parallel", "arbitrary")),
        name="in_proj",
    )(h, w, qk_gain)


ROW_SLABS = D_MODEL // (2 * LANES)
HALF_D = D_MODEL // 2
_HIGH_HALF = -65536


def _bf16_bits(x):
    return lax.bitcast_convert_type(x.astype(BF16).astype(F32), jnp.int32)


def _store_packed_rows(ref, read_cols, t):
    for s in range(ROW_SLABS):
        lo = (_bf16_bits(read_cols(s * LANES)) >> 16) & 0xFFFF
        hi = _bf16_bits(read_cols(HALF_D + s * LANES)) & _HIGH_HALF
        ref[pl.ds(s, t, stride=ROW_SLABS), :] = lo | hi


def _load_packed_slab(ref, s, t):
    w = ref[pl.ds(s, t, stride=ROW_SLABS), :]
    lo = lax.bitcast_convert_type(w << 16, F32)
    hi = lax.bitcast_convert_type(w & _HIGH_HALF, F32)
    return lo, hi


def _row_proj_kernel(a_ref, w_ref, r_ref, g_ref, o_ref, h_ref, *, pack_rows):
    x = r_ref[...] + jnp.dot(a_ref[...], w_ref[...], preferred_element_type=F32)
    ms = jnp.mean(x * x, axis=-1, keepdims=True)
    o_ref[...] = x
    h = x * lax.rsqrt(ms + EPS) * g_ref[...]
    if pack_rows:
        _store_packed_rows(h_ref, lambda c0: h[:, c0:c0 + LANES], x.shape[0])
    else:
        h_ref[...] = h.astype(h_ref.dtype)


def out_proj(a, w, resid, norm_gain, pack_rows=False, tm=512):
    n, k = a.shape
    d = w.shape[1]
    row = lambda width: pl.BlockSpec((tm, width), lambda i: (i, 0))
    if pack_rows:
        h_shape = jax.ShapeDtypeStruct((n * ROW_SLABS, LANES), jnp.int32)
        h_spec = pl.BlockSpec((tm * ROW_SLABS, LANES), lambda i: (i, 0))
    else:
        h_shape, h_spec = jax.ShapeDtypeStruct((n, d), BF16), row(d)
    return pl.pallas_call(
        functools.partial(_row_proj_kernel, pack_rows=pack_rows),
        out_shape=(jax.ShapeDtypeStruct((n, d), F32), h_shape),
        grid=(n // tm,),
        in_specs=[row(k), pl.BlockSpec((k, d), lambda i: (0, 0), pipeline_mode=pl.Buffered(1)),
                  row(d), pl.BlockSpec((1, d), lambda i: (0, 0))],
        out_specs=(row(d), h_spec),
        compiler_params=_cparams(("parallel",)),
        name="out_proj",
    )(a, w, resid, norm_gain.reshape(1, d))


def _router_kernel(x_ref, g_ref, rhi_ref, rlo_ref, info_ref, cnt_ref, carry_ref):
    x = x_ref[...]
    ms = jnp.mean(x * x, axis=-1, keepdims=True)
    h = x * lax.rsqrt(ms + EPS) * g_ref[...]
    _route_top2(h, rhi_ref, rlo_ref, info_ref, cnt_ref, carry_ref)


def route_tokens(x, g, router, tm=512):
    n, d = x.shape
    r_pad = jnp.zeros((d, LANES), F32).at[:, :N_EXPERTS].set(router)
    fixed = lambda shape: pl.BlockSpec(shape, lambda i: (0, 0))
    return pl.pallas_call(
        _router_kernel,
        out_shape=(jax.ShapeDtypeStruct((n, LANES), F32), jax.ShapeDtypeStruct((8, LANES), F32)),
        grid=(n // tm,),
        in_specs=[pl.BlockSpec((tm, d), lambda i: (i, 0)), fixed((1, d)),
                  fixed((d, LANES)), fixed((d, LANES))],
        out_specs=(pl.BlockSpec((tm, LANES), lambda i: (i, 0)), fixed((8, LANES))),
        scratch_shapes=[pltpu.VMEM((8, LANES), F32)],
        compiler_params=_cparams(("arbitrary",)),
        name="route_tokens",
    )(x, g.reshape(1, d), *_split_bf16(r_pad))


POOL_HALO = 128


def _pool_kernel(u_ref, w_ref, s_ref, o_ref, prev_ref, *, t):
    sb = pl.program_id(1)

    @pl.when(sb == 0)
    def _():
        prev_ref[...] = jnp.zeros_like(prev_ref)

    row = lax.broadcasted_iota(jnp.int32, (t, t), 0)
    col = lax.broadcasted_iota(jnp.int32, (t, t), 1)
    prow = lax.broadcasted_iota(jnp.int32, (t, POOL_HALO), 0)
    pcol = lax.broadcasted_iota(jnp.int32, (t, POOL_HALO), 1) - POOL_HALO
    tg = sb * t + lax.broadcasted_iota(jnp.int32, (t, 1), 0)
    for gi, w in enumerate(POOL_WINDOWS):
        cs = slice(gi * POOL_GROUP_W, (gi + 1) * POOL_GROUP_W)
        u = u_ref[:, cs]
        band_cur = ((col <= row) & (col > row - w)).astype(BF16)
        band_prev = ((pcol > prow - w) & (pcol + sb * t >= 0)).astype(BF16)
        win = jnp.dot(band_cur, u, preferred_element_type=F32)
        win = win + jnp.dot(band_prev, prev_ref[:, cs], preferred_element_type=F32)
        count = jnp.minimum(tg + 1, w).astype(F32)
        pooled = win / count - u.astype(F32)
        mixed = jnp.dot(pooled.astype(BF16), w_ref[gi], preferred_element_type=F32)
        o_ref[:, cs] = (mixed * s_ref[:, cs]).astype(o_ref.dtype)
    prev_ref[...] = u_ref[t - POOL_HALO:, :]


def pool_mixer(z3, pool_w, scale, t=256):
    b, s, _ = z3.shape
    return pl.pallas_call(
        functools.partial(_pool_kernel, t=t),
        out_shape=jax.ShapeDtypeStruct((b, s, POOL_W), BF16),
        grid=(b, s // t),
        in_specs=[pl.BlockSpec((None, t, POOL_W), lambda bi, si: (bi, si, COL_POOL // POOL_W)),
                  pl.BlockSpec((POOL_GROUPS, POOL_GROUP_W, POOL_GROUP_W), lambda bi, si: (0, 0, 0)),
                  pl.BlockSpec((1, POOL_W), lambda bi, si: (0, 0))],
        out_specs=pl.BlockSpec((None, t, POOL_W), lambda bi, si: (bi, si, 0)),
        scratch_shapes=[pltpu.VMEM((POOL_HALO, POOL_W), BF16)],
        compiler_params=_cparams(("parallel", "arbitrary")),
        name="pool_mixer",
    )(z3, pool_w, scale.reshape(1, POOL_W))


def _head_rmsnorm(x, g):
    ms = jnp.mean(x * x, axis=-1, keepdims=True)
    return x * lax.rsqrt(ms + EPS) * g


def _sb_kernel(q_ref, k_ref, v_ref, o_ref, acc_ref, c_ref, *, t, nh):
    i = pl.program_id(2)
    hd = SB_HEAD_DIM
    row = lax.broadcasted_iota(jnp.int32, (t, t), 0)
    col = lax.broadcasted_iota(jnp.int32, (t, t), 1)
    diag_mask = col < row
    suffix = (row > col).astype(BF16)
    suffix2 = jnp.concatenate([suffix, suffix], axis=0)
    has_prev = jnp.broadcast_to(i > 0, (t, t))

    def block(h, start, mask, c):
        hs = slice(h * hd, (h + 1) * hd)
        k = k_ref[pl.ds(start, t), hs]
        z = lax.dot_general(q_ref[:, hs], k, (((1,), (1,)), ((), ())),
                            preferred_element_type=F32)
        soft = jnp.log(1.0 + jnp.exp(-jnp.abs(z)))
        neg_part = jnp.minimum(z, 0.0)
        log_beta = neg_part - soft
        log_keep = (neg_part - z) - soft
        if mask is not None:
            log_keep = jnp.where(mask, log_keep, 0.0)
        hi, lo = _split_bf16(log_keep)
        later = jnp.dot(jnp.concatenate([hi, lo], axis=1), suffix2, preferred_element_type=F32)
        a = jnp.exp(log_beta + later + c)
        if mask is not None:
            a = jnp.where(mask, a, 0.0)
        out = jnp.dot(a.astype(BF16), v_ref[pl.ds(start, t), hs], preferred_element_type=F32)
        return out, c + later[:, 0:1] + log_keep[:, 0:1]

    diag_start = pl.multiple_of(i * t, t)
    prev_start = pl.multiple_of(jnp.maximum(i - 1, 0) * t, t)
    live = jnp.int32(0)
    for h in range(nh):
        out_a, c = block(h, diag_start, diag_mask, jnp.zeros((t, 1), F32))
        out_b, c = block(h, prev_start, has_prev, c)
        acc_ref[:, h * hd:(h + 1) * hd] = out_a + out_b
        c_ref[h] = c
        live = jnp.maximum(live, (jnp.max(c) > SB_F32_ZERO_LOG).astype(jnp.int32))

    def body(carry):
        j, _ = carry
        start = pl.multiple_of(j * t, t)
        live = jnp.int32(0)
        for h in range(nh):
            out, c = block(h, start, None, c_ref[h])
            acc_ref[:, h * hd:(h + 1) * hd] += out
            c_ref[h] = c
            live = jnp.maximum(live, (jnp.max(c) > SB_F32_ZERO_LOG).astype(jnp.int32))
        return j - 1, live

    def cond(carry):
        j, live = carry
        return jnp.logical_and(j >= 0, live > 0)

    lax.while_loop(cond, body, (i - 2, live))
    o_ref[...] = acc_ref[...].astype(o_ref.dtype)


def sb_attention(z3, t=256, nh=2):
    b, s, _ = z3.shape
    w = nh * SB_HEAD_DIM
    return pl.pallas_call(
        functools.partial(_sb_kernel, t=t, nh=nh),
        out_shape=jax.ShapeDtypeStruct((b, s, SB_W), BF16),
        grid=(b, SB_HEADS // nh, s // t),
        in_specs=[pl.BlockSpec((None, t, w), lambda bi, h, i: (bi, i, COL_SQ // w + h)),
                  pl.BlockSpec((None, s, w), lambda bi, h, i: (bi, 0, COL_SK // w + h)),
                  pl.BlockSpec((None, s, w), lambda bi, h, i: (bi, 0, COL_SV // w + h))],
        out_specs=pl.BlockSpec((None, t, w), lambda bi, h, i: (bi, i, h)),
        scratch_shapes=[pltpu.VMEM((t, w), F32), pltpu.VMEM((nh, t, 1), F32)],
        compiler_params=_cparams(("parallel", "parallel", "arbitrary")),
        name="sb_attention",
    )(z3, z3, z3)


def _gla_gate_kernel(h_ref, wg_ref, wlr_hi_ref, wlr_lo_ref, blr_ref, o_ref, *, tm):
    g = jnp.dot(h_ref[...], wg_ref[...], preferred_element_type=F32)
    g_hi, g_lo = _split_bf16(g)
    pre = (jnp.dot(g_hi, wlr_hi_ref[...], preferred_element_type=F32)
           + jnp.dot(g_lo, wlr_hi_ref[...], preferred_element_type=F32)
           + jnp.dot(g_hi, wlr_lo_ref[...], preferred_element_type=F32)) + blr_ref[...]
    log_a = _log_sigmoid(pre) * (1.0 / GLA_TAU)
    row = lax.broadcasted_iota(jnp.int32, (tm, tm), 0)
    col = lax.broadcasted_iota(jnp.int32, (tm, tm), 1)
    shift = GLA_CHUNK.bit_length() - 1
    tri = ((col <= row) & ((col >> shift) == (row >> shift))).astype(BF16)
    hi, lo = _split_bf16(log_a)
    o_ref[...] = (jnp.dot(tri, hi, preferred_element_type=F32)
                  + jnp.dot(tri, lo, preferred_element_type=F32))


def gla_gate(h, w_glr, w_lr, b_lr, tm=256):
    n, d = h.shape
    return pl.pallas_call(
        functools.partial(_gla_gate_kernel, tm=tm),
        out_shape=jax.ShapeDtypeStruct((n, GLA_K_W), F32),
        grid=(n // tm,),
        in_specs=[pl.BlockSpec((tm, d), lambda i: (i, 0)),
                  pl.BlockSpec((d, LANES), lambda i: (0, 0)),
                  pl.BlockSpec((LANES, GLA_K_W), lambda i: (0, 0)),
                  pl.BlockSpec((LANES, GLA_K_W), lambda i: (0, 0)),
                  pl.BlockSpec((1, GLA_K_W), lambda i: (0, 0))],
        out_specs=pl.BlockSpec((tm, GLA_K_W), lambda i: (i, 0)),
        compiler_params=_cparams(("parallel",)),
        name="gla_gate",
    )(h, w_glr, *_split_bf16(w_lr), b_lr.reshape(1, GLA_K_W))


def _gla_kernel(q_ref, k_ref, v_ref, r_ref, b_ref, g_ref, o_ref, st_ref, *, tc):
    @pl.when(pl.program_id(2) == 0)
    def _():
        st_ref[...] = jnp.zeros_like(st_ref)

    c = GLA_CHUNK
    sub = GLA_SUB
    causal = (lax.broadcasted_iota(jnp.int32, (c, c), 1)
              <= lax.broadcasted_iota(jnp.int32, (c, c), 0))
    g_out = g_ref[...]

    def chunk(ci, carry):
        base = pl.multiple_of(ci * c, c)
        rows = pl.ds(base, c)
        q = q_ref[rows, :].astype(F32) * (GLA_DK ** -0.5)
        k = k_ref[rows, :].astype(F32)
        v = v_ref[rows, :]
        b = b_ref[rows, :]
        b_end = b_ref[pl.ds(base + c - 1, 1), :]
        st = st_ref[...]
        inter = lax.dot_general((q * jnp.exp(b)).astype(BF16), st.astype(BF16),
                                (((1,), (1,)), ((), ())), preferred_element_type=F32)
        scores = []
        for si in range(c // sub):
            lo, hi = si * sub, (si + 1) * sub
            if si > 0:
                ref = b_ref[pl.ds(base + lo - 1, 1), :]
            else:
                ref = jnp.zeros((1, GLA_DK), F32)
            q_t = (q[lo:hi] * jnp.exp(b[lo:hi] - ref)).astype(BF16)
            k_t = (k * jnp.exp(jnp.minimum(ref - b, GLA_EXP_CAP))).astype(BF16)
            scores.append(lax.dot_general(q_t, k_t, (((1,), (1,)), ((), ())),
                                          preferred_element_type=F32))
        sc = jnp.where(causal, jnp.concatenate(scores, axis=0), 0.0)
        o = inter + jnp.dot(sc.astype(BF16), v, preferred_element_type=F32)
        k_e = (k * jnp.exp(b_end - b)).astype(BF16)
        v_t = v.astype(F32).T.astype(BF16)
        st_ref[...] = st * jnp.exp(b_end) + jnp.dot(v_t, k_e, preferred_element_type=F32)
        ms = jnp.mean(o * o, axis=-1, keepdims=True)
        o = o * lax.rsqrt(ms + EPS) * g_out
        r = r_ref[rows, :].astype(F32)
        o_ref[rows, :] = (o * (r * _sigmoid(r))).astype(o_ref.dtype)
        return carry

    lax.fori_loop(0, tc // c, chunk, 0, unroll=True)


def gla_mixer(z3, bcum3, g_out, tc=256):
    b, s, _ = z3.shape
    dk, dv = GLA_DK, GLA_DV
    return pl.pallas_call(
        functools.partial(_gla_kernel, tc=tc),
        out_shape=jax.ShapeDtypeStruct((b, s, GLA_V_W), BF16),
        grid=(b, GLA_HEADS, s // tc),
        in_specs=[pl.BlockSpec((None, tc, dk), lambda bi, h, i: (bi, i, COL_GQ // dk + h)),
                  pl.BlockSpec((None, tc, dk), lambda bi, h, i: (bi, i, COL_GK // dk + h)),
                  pl.BlockSpec((None, tc, dv), lambda bi, h, i: (bi, i, COL_GV // dv + h)),
                  pl.BlockSpec((None, tc, dv), lambda bi, h, i: (bi, i, COL_GR // dv + h)),
                  pl.BlockSpec((None, tc, dk), lambda bi, h, i: (bi, i, h)),
                  pl.BlockSpec((1, dv), lambda bi, h, i: (0, 0))],
        out_specs=pl.BlockSpec((None, tc, dv), lambda bi, h, i: (bi, i, h)),
        scratch_shapes=[pltpu.VMEM((dv, dk), F32)],
        compiler_params=_cparams(("parallel", "parallel", "arbitrary")),
        name="gla_mixer",
    )(z3, z3, z3, z3, bcum3, g_out.reshape(1, dv))


def _merge_kernel(h_ref, wga_ref, wgb_ref, wgc_ref, ba_ref, bb_ref, bc_ref,
                  yp_ref, ys_ref, yg_ref, wp_ref, ws_ref, wg_ref, o_ref):
    h = h_ref[...]
    m = None
    for wgate_ref, b_ref, y_ref, wup_ref in ((wga_ref, ba_ref, yp_ref, wp_ref),
                                             (wgb_ref, bb_ref, ys_ref, ws_ref),
                                             (wgc_ref, bc_ref, yg_ref, wg_ref)):
        gate = _sigmoid(jnp.dot(h, wgate_ref[...], preferred_element_type=F32) + b_ref[...])
        term = gate * jnp.dot(y_ref[...], wup_ref[...], preferred_element_type=F32)
        m = term if m is None else m + term
    o_ref[...] = m.astype(o_ref.dtype)


def merge_branches(h, w_gate, b_gate, y_pool, y_sb, y_gla, w_p, w_s, w_g, tm=1024, tn=512):
    n, d = h.shape
    nj = d // tn
    gate_w_spec = lambda br: pl.BlockSpec((d, tn), lambda i, j: (0, br * nj + j))
    gate_b_spec = lambda br: pl.BlockSpec((1, tn), lambda i, j: (0, br * nj + j))
    y_spec = lambda w: pl.BlockSpec((tm, w), lambda i, j: (i, 0))
    w_spec = lambda w: pl.BlockSpec((w, tn), lambda i, j: (0, j))
    b_gate = b_gate.reshape(1, N_BRANCH * d)
    return pl.pallas_call(
        _merge_kernel,
        out_shape=jax.ShapeDtypeStruct((n, d), BF16),
        grid=(n // tm, nj),
        in_specs=[y_spec(d), gate_w_spec(0), gate_w_spec(1), gate_w_spec(2),
                  gate_b_spec(0), gate_b_spec(1), gate_b_spec(2),
                  y_spec(POOL_W), y_spec(SB_W), y_spec(GLA_V_W),
                  w_spec(POOL_W), w_spec(SB_W), w_spec(GLA_V_W)],
        out_specs=pl.BlockSpec((tm, tn), lambda i, j: (i, j)),
        compiler_params=_cparams(("parallel", "arbitrary")),
        name="merge_branches",
    )(h, w_gate, w_gate, w_gate, b_gate, b_gate, b_gate, y_pool, y_sb, y_gla, w_p, w_s, w_g)


def _silu(a):
    return a * _sigmoid(a)


def _ffn_kernel(h_ref, w1_ref, w3_ref, w2_ref, o_ref):
    @pl.when(pl.program_id(1) == 0)
    def _():
        o_ref[...] = jnp.zeros_like(o_ref)

    h = h_ref[...]
    a = jnp.dot(h, w1_ref[...], preferred_element_type=F32)
    b = jnp.dot(h, w3_ref[...], preferred_element_type=F32)
    g = (_silu(a) * b).astype(BF16)
    o_ref[...] += jnp.dot(g, w2_ref[...], preferred_element_type=F32)


def ffn_dense(h, w1, w3, w2, tm=1024, tf=512):
    n, d = h.shape
    ff = w1.shape[1]
    return pl.pallas_call(
        _ffn_kernel,
        out_shape=jax.ShapeDtypeStruct((n, d), F32),
        grid=(n // tm, ff // tf),
        in_specs=[pl.BlockSpec((tm, d), lambda i, f: (i, 0)),
                  pl.BlockSpec((d, tf), lambda i, f: (0, f)),
                  pl.BlockSpec((d, tf), lambda i, f: (0, f)),
                  pl.BlockSpec((tf, d), lambda i, f: (f, 0))],
        out_specs=pl.BlockSpec((tm, d), lambda i, f: (i, 0)),
        compiler_params=_cparams(("parallel", "arbitrary")),
        name="ffn_dense",
    )(h, w1, w3, w2)


MOE_TM = 1024
TOP_K = 2


def _row_copy(src_ref, src_idx, dst_ref, dst_idx, sem):
    return pltpu.make_async_copy(src_ref.at[src_idx], dst_ref.at[dst_idx], sem)


def _dispatch_kernel(dest_ref, h_ref, init_ref, o_ref, sem, *, tm):
    del init_ref
    base = pl.program_id(0) * tm

    def copies(t):
        return [_row_copy(h_ref, t, o_ref, dest_ref[TOP_K * (base + t) + kk], sem)
                for kk in range(TOP_K)]

    def start(t, carry):
        for kk, cp in enumerate(copies(t)):
            cp.start(priority=kk % 2)
        return carry

    def wait(t, carry):
        for cp in copies(t):
            cp.wait()
        return carry

    lax.fori_loop(0, tm, start, 0)
    lax.fori_loop(0, tm, wait, 0)


def moe_dispatch(dest, h_rows, n_rows, tm=256):
    n = h_rows.shape[0]
    grid_spec = pltpu.PrefetchScalarGridSpec(
        num_scalar_prefetch=1,
        grid=(n // tm,),
        in_specs=[pl.BlockSpec((tm, ROW_SLABS, LANES), lambda i, dest: (i, 0, 0)),
                  pl.BlockSpec(memory_space=pl.ANY)],
        out_specs=pl.BlockSpec(memory_space=pl.ANY),
        scratch_shapes=[pltpu.SemaphoreType.DMA],
    )
    return pl.pallas_call(
        functools.partial(_dispatch_kernel, tm=tm),
        out_shape=jax.ShapeDtypeStruct((n_rows, ROW_SLABS, LANES), h_rows.dtype),
        grid_spec=grid_spec,
        input_output_aliases={2: 0},
        compiler_params=_cparams(("arbitrary",)),
        name="moe_dispatch",
    )(dest, h_rows, jnp.zeros((n_rows, ROW_SLABS, LANES), h_rows.dtype))


def _moe_ffn_kernel(te_ref, nu_ref, h_ref, w1_ref, w3_ref, w2_ref, o_ref, hb_ref, acc_ref, *, tm):
    del te_ref
    f = pl.program_id(1)
    last = pl.num_programs(1) - 1
    used = pl.program_id(0) < nu_ref[0]

    @pl.when(used & (f == 0))
    def _():
        acc_ref[...] = jnp.zeros_like(acc_ref)
        for s in range(ROW_SLABS):
            lo, hi = _load_packed_slab(h_ref, s, tm)
            hb_ref[:, s * LANES:(s + 1) * LANES] = lo.astype(BF16)
            hb_ref[:, HALF_D + s * LANES:HALF_D + (s + 1) * LANES] = hi.astype(BF16)

    @pl.when(used)
    def _():
        h = hb_ref[...]
        a = jnp.dot(h, w1_ref[...], preferred_element_type=F32)
        b = jnp.dot(h, w3_ref[...], preferred_element_type=F32)
        g = (_silu(a) * b).astype(BF16)
        acc_ref[...] += jnp.dot(g, w2_ref[...], preferred_element_type=F32)

    @pl.when(used & (f == last))
    def _():
        _store_packed_rows(o_ref, lambda c0: acc_ref[:, c0:c0 + LANES], tm)

    @pl.when(jnp.logical_not(used) & (f == last))
    def _():
        o_ref[...] = jnp.zeros_like(o_ref)


def moe_grouped_ffn(tile_expert, n_used, h_sorted, w1, w3, w2, tm=MOE_TM, tf=512):
    r = h_sorted.shape[0] // ROW_SLABS
    d, ff = w1.shape[1], w1.shape[2]
    nf = ff // tf

    def f_idx(i, f, nu):
        return jnp.where(i < nu[0], f, nf - 1)

    rows = pl.BlockSpec((tm * ROW_SLABS, LANES), lambda i, f, te, nu: (i, 0))
    grid_spec = pltpu.PrefetchScalarGridSpec(
        num_scalar_prefetch=2,
        grid=(r // tm, nf),
        in_specs=[rows,
                  pl.BlockSpec((None, d, tf), lambda i, f, te, nu: (te[i], 0, f_idx(i, f, nu))),
                  pl.BlockSpec((None, d, tf), lambda i, f, te, nu: (te[i], 0, f_idx(i, f, nu))),
                  pl.BlockSpec((None, tf, d), lambda i, f, te, nu: (te[i], f_idx(i, f, nu), 0))],
        out_specs=rows,
        scratch_shapes=[pltpu.VMEM((tm, d), BF16), pltpu.VMEM((tm, d), F32)],
    )
    return pl.pallas_call(
        functools.partial(_moe_ffn_kernel, tm=tm),
        out_shape=jax.ShapeDtypeStruct(h_sorted.shape, jnp.int32),
        grid_spec=grid_spec,
        compiler_params=_cparams(("arbitrary", "arbitrary")),
        name="moe_grouped_ffn",
    )(tile_expert, n_used, h_sorted, w1, w3, w2)


def _combine_kernel(dest_ref, w_ref, y_ref, o_ref, buf_ref, sems, *, tm):
    base = pl.program_id(0) * tm

    def copies(t):
        return [_row_copy(y_ref, dest_ref[TOP_K * (base + t) + kk], buf_ref.at[kk], t, sems.at[kk])
                for kk in range(TOP_K)]

    def start(t, carry):
        for kk, cp in enumerate(copies(t)):
            cp.start(priority=kk % 2)
        return carry

    def wait(t, carry):
        for cp in copies(t):
            cp.wait()
        return carry

    lax.fori_loop(0, tm, start, 0)
    lax.fori_loop(0, tm, wait, 0)
    lo = hi = None
    for kk in range(TOP_K):
        words = buf_ref[kk]
        w = w_ref[kk]
        lo_k = lax.bitcast_convert_type(words << 16, F32) * w
        hi_k = lax.bitcast_convert_type(words & _HIGH_HALF, F32) * w
        lo = lo_k if lo is None else lo + lo_k
        hi = hi_k if hi is None else hi + hi_k
    o_ref[0] = lo
    o_ref[1] = hi


def moe_combine(dest, w_rows, y_rows, n, tm=256):
    grid_spec = pltpu.PrefetchScalarGridSpec(
        num_scalar_prefetch=1,
        grid=(n // tm,),
        in_specs=[pl.BlockSpec((TOP_K, tm, 1, LANES), lambda i, dest: (0, i, 0, 0)),
                  pl.BlockSpec(memory_space=pl.ANY)],
        out_specs=pl.BlockSpec((2, tm, ROW_SLABS, LANES), lambda i, dest: (0, i, 0, 0)),
        scratch_shapes=[pltpu.VMEM((TOP_K, tm, ROW_SLABS, LANES), jnp.int32),
                        pltpu.SemaphoreType.DMA((TOP_K,))],
    )
    return pl.pallas_call(
        functools.partial(_combine_kernel, tm=tm),
        out_shape=jax.ShapeDtypeStruct((2, n, ROW_SLABS, LANES), F32),
        grid_spec=grid_spec,
        compiler_params=_cparams(("arbitrary",)),
        name="moe_combine",
    )(dest, w_rows, y_rows)


def moe_sparse(h_rows, info, counts, w1, w3, w2):
    n = h_rows.shape[0] // ROW_SLABS
    tm = MOE_TM
    n_tiles = (TOP_K * n) // tm + N_EXPERTS
    n_rows = n_tiles * tm
    cnt = counts[0, :N_EXPERTS].astype(jnp.int32)
    tiles_per = (cnt + tm - 1) // tm
    tile_end = jnp.cumsum(tiles_per)
    group_start = (tile_end - tiles_per) * tm
    n_used = tile_end[-1:]
    tile_expert = jnp.minimum(
        jnp.searchsorted(tile_end, jnp.arange(n_tiles, dtype=jnp.int32), side="right"),
        N_EXPERTS - 1).astype(jnp.int32)
    tile_expert = jnp.where(jnp.arange(n_tiles) < n_used[0], tile_expert,
                            tile_expert[jnp.maximum(n_used[0] - 1, 0)])
    experts = info[:, INFO_E1:INFO_E2 + 1].astype(jnp.int32)
    ranks = info[:, INFO_R1:INFO_R2 + 1].astype(jnp.int32)
    dest = (group_start[experts] + ranks).reshape(TOP_K * n)
    w_rows = jnp.broadcast_to(info[:, INFO_W1:INFO_W2 + 1].T[:, :, None, None],
                              (TOP_K, n, 1, LANES))

    h_sorted = moe_dispatch(dest, h_rows.reshape(n, ROW_SLABS, LANES), n_rows)
    y_sorted = moe_grouped_ffn(tile_expert, n_used,
                               h_sorted.reshape(n_rows * ROW_SLABS, LANES), w1, w3, w2)
    y = moe_combine(dest, w_rows, y_sorted.reshape(n_rows, ROW_SLABS, LANES), n)
    return y.reshape(2, n * ROW_SLABS, LANES)


def _ple_kernel(x_ref, y_ref, p_ref, wg_ref, wp_ref, *rest, y_slabs, with_norm):
    if with_norm:
        g_ref, o_ref, h_ref = rest
    else:
        (o_ref,) = rest
    if y_slabs:
        t = x_ref.shape[0]
        parts = []
        for half in range(2):
            for s in range(ROW_SLABS):
                parts.append(y_ref[half, pl.ds(s, t, stride=ROW_SLABS), :])
        x = x_ref[...] + jnp.concatenate(parts, axis=1)
    else:
        x = x_ref[...] + y_ref[...]
    xb = x.astype(BF16)
    pb = p_ref[...].astype(BF16)
    d = x.shape[1]
    sum_sq = jnp.zeros((x.shape[0], 1), F32)
    for c0 in range(0, d, PLE_COL_CHUNK):
        cs = slice(c0, c0 + PLE_COL_CHUNK)
        gate = _sigmoid(jnp.dot(xb, wg_ref[:, cs], preferred_element_type=F32))
        proj = jnp.dot(pb, wp_ref[:, cs], preferred_element_type=F32)
        xc = x[:, cs] + gate * proj
        o_ref[:, cs] = xc
        sum_sq += jnp.sum(xc * xc, axis=-1, keepdims=True)
    if with_norm:
        inv = lax.rsqrt(sum_sq * (1.0 / d) + EPS)
        h_ref[...] = (o_ref[...] * inv * g_ref[...]).astype(h_ref.dtype)


PLE_COL_CHUNK = 512


def ple_update(x, y, p, w_gate, w_proj, norm_gain=None, tm=512):
    n, d = x.shape
    pd = p.shape[1]
    y_slabs = y.ndim == 3
    with_norm = norm_gain is not None
    row = lambda width: pl.BlockSpec((tm, width), lambda i: (i, 0))
    resident = lambda shape: pl.BlockSpec(shape, lambda i: (0, 0), pipeline_mode=pl.Buffered(1))
    y_spec = (pl.BlockSpec((2, tm * ROW_SLABS, LANES), lambda i: (0, i, 0)) if y_slabs else row(d))
    in_specs = [row(d), y_spec, row(pd), resident((d, d)), resident((pd, d))]
    args = [x, y, p, w_gate, w_proj]
    out_shape = [jax.ShapeDtypeStruct((n, d), F32)]
    out_specs = [row(d)]
    if with_norm:
        in_specs.append(pl.BlockSpec((1, d), lambda i: (0, 0)))
        args.append(norm_gain.reshape(1, d))
        out_shape.append(jax.ShapeDtypeStruct((n, d), BF16))
        out_specs.append(row(d))
    res = pl.pallas_call(
        functools.partial(_ple_kernel, y_slabs=y_slabs, with_norm=with_norm),
        out_shape=tuple(out_shape),
        grid=(n // tm,),
        in_specs=in_specs,
        out_specs=tuple(out_specs),
        compiler_params=_cparams(("parallel",)),
        name="ple_update",
    )(*args)
    return res if with_norm else res[0]


def _split_w_in(w_in_i):
    c_lr = COL_GR
    main = jnp.concatenate([w_in_i[:, :c_lr], w_in_i[:, c_lr + GLA_RANK:]], axis=1).astype(BF16)
    glr = jnp.zeros((D_MODEL, LANES), BF16).at[:, :GLA_RANK].set(
        w_in_i[:, c_lr:c_lr + GLA_RANK].astype(BF16))
    return main, glr


def kernel(x, p, g_mix, w_in, w_branch_gate, b_branch_gate, pool_w, pool_scale, sb_gq, sb_gk,
           gla_w_lr, gla_b_lr, gla_g_out, w_up_pool, w_up_sb, w_up_gla, w_o, g_ffn,
           ffn_w1, ffn_w3, ffn_w2, moe_router, moe_w1, moe_w3, moe_w2, ple_w_proj, ple_w_gate):
    bsz, seq, d = x.shape
    n = bsz * seq
    depth = w_in.shape[0]
    xf = x.reshape(n, d)
    h = rmsnorm(xf, g_mix[0])
    for i in range(depth):
        w_main, w_glr = _split_w_in(w_in[i])
        w_lr_pad = jnp.zeros((LANES, GLA_K_W), F32).at[:GLA_RANK].set(gla_w_lr[i])
        qk_gain = jnp.ones((1, Z_COLS), F32)
        qk_gain = qk_gain.at[0, COL_SQ:COL_SK].set(jnp.tile(sb_gq[i] * SB_HEAD_DIM ** -0.5, SB_HEADS))
        qk_gain = qk_gain.at[0, COL_SK:COL_SV].set(jnp.tile(sb_gk[i], SB_HEADS))

        z = in_proj(h, w_main, qk_gain)
        bcum = gla_gate(h, w_glr, w_lr_pad, gla_b_lr[i])

        z3 = z.reshape(bsz, seq, Z_COLS)
        y_pool = pool_mixer(z3, pool_w[i].astype(BF16), pool_scale[i])
        y_sb = sb_attention(z3)
        y_gla = gla_mixer(z3, bcum.reshape(bsz, seq, GLA_K_W), gla_g_out[i])

        merged = merge_branches(h, w_branch_gate[i].astype(BF16), b_branch_gate[i],
                                y_pool.reshape(n, POOL_W), y_sb.reshape(n, SB_W),
                                y_gla.reshape(n, GLA_V_W), w_up_pool[i].astype(BF16),
                                w_up_sb[i].astype(BF16), w_up_gla[i].astype(BF16))

        j = i // 2
        if i % 2 == 0:
            xf, h2 = out_proj(merged, w_o[i].astype(BF16), xf, g_ffn[i])
            y = ffn_dense(h2, ffn_w1[j].astype(BF16), ffn_w3[j].astype(BF16),
                          ffn_w2[j].astype(BF16))
        else:
            xf, h2_rows = out_proj(merged, w_o[i].astype(BF16), xf, g_ffn[i], pack_rows=True)
            info, counts = route_tokens(xf, g_ffn[i], moe_router[j])
            y = moe_sparse(h2_rows, info, counts, moe_w1[j].astype(BF16),
                           moe_w3[j].astype(BF16), moe_w2[j].astype(BF16))

        if i + 1 < depth:
            xf, h = ple_update(xf, y, p[i].reshape(n, PLE_DIM), ple_w_gate[i].astype(BF16),
                               ple_w_proj[i].astype(BF16), norm_gain=g_mix[i + 1])
        else:
            xf = ple_update(xf, y, p[i].reshape(n, PLE_DIM), ple_w_gate[i].astype(BF16),
                            ple_w_proj[i].astype(BF16))
    return xf.reshape(bsz, seq, d)
```

```python
import functools

import jax
import jax.numpy as jnp
from jax import lax
from jax.experimental import pallas as pl
from jax.experimental.pallas import tpu as pltpu

F32 = jnp.float32
BF16 = jnp.bfloat16

EPS = 1e-6
D_MODEL = 2048
PLE_DIM = 256
POOL_GROUPS = 4
POOL_GROUP_W = 256
POOL_W = POOL_GROUPS * POOL_GROUP_W
POOL_WINDOWS = (2, 4, 8, 16)
SB_HEADS = 8
SB_HEAD_DIM = 128
SB_W = SB_HEADS * SB_HEAD_DIM
GLA_HEADS = 4
GLA_DK = 128
GLA_DV = 256
GLA_K_W = GLA_HEADS * GLA_DK
GLA_V_W = GLA_HEADS * GLA_DV
GLA_RANK = 16
GLA_TAU = 16.0
D_FF = 5632
N_EXPERTS = 8
N_BRANCH = 3

LANES = 128
VMEM_LIMIT = 56 * 1024 * 1024

COL_POOL = 0
COL_SQ = COL_POOL + POOL_W
COL_SK = COL_SQ + SB_W
COL_SV = COL_SK + SB_W
COL_GQ = COL_SV + SB_W
COL_GK = COL_GQ + GLA_K_W
COL_GV = COL_GK + GLA_K_W
COL_GR = COL_GV + GLA_V_W
Z_COLS = COL_GR + GLA_V_W

GLA_CHUNK = 64
GLA_SUB = 16
GLA_EXP_CAP = 80.0
SB_F32_ZERO_LOG = -104.0


def _cparams(sem):
    return pltpu.CompilerParams(dimension_semantics=sem, vmem_limit_bytes=VMEM_LIMIT)


def _log_sigmoid(z):
    return jnp.minimum(z, 0.0) - jnp.log(1.0 + jnp.exp(-jnp.abs(z)))


def _sigmoid(z):
    return 1.0 / (1.0 + jnp.exp(-z))


def _split_bf16(x):
    hi = x.astype(BF16)
    lo = (x - hi.astype(F32)).astype(BF16)
    return hi, lo


def _rmsnorm_kernel(x_ref, g_ref, o_ref):
    x = x_ref[...]
    ms = jnp.mean(x * x, axis=-1, keepdims=True)
    o_ref[...] = (x * lax.rsqrt(ms + EPS) * g_ref[...]).astype(o_ref.dtype)


def rmsnorm(x, g, tm=512):
    n, d = x.shape
    return pl.pallas_call(
        _rmsnorm_kernel,
        out_shape=jax.ShapeDtypeStruct((n, d), BF16),
        grid=(n // tm,),
        in_specs=[pl.BlockSpec((tm, d), lambda i: (i, 0)),
                  pl.BlockSpec((1, d), lambda i: (0, 0))],
        out_specs=pl.BlockSpec((tm, d), lambda i: (i, 0)),
        compiler_params=_cparams(("parallel",)),
        name="rmsnorm",
    )(x, g.reshape(1, d))


def _route_top2(h, rhi_ref, rlo_ref, info_ref, cnt_ref, carry_ref):
    tm = h.shape[0]

    @pl.when(pl.program_id(0) == 0)
    def _():
        carry_ref[...] = jnp.zeros_like(carry_ref)

    h_hi, h_lo = _split_bf16(h)
    logits = (jnp.dot(h_hi, rhi_ref[...], preferred_element_type=F32)
              + jnp.dot(h_lo, rhi_ref[...], preferred_element_type=F32)
              + jnp.dot(h_hi, rlo_ref[...], preferred_element_type=F32))
    lane = lax.broadcasted_iota(jnp.int32, logits.shape, 1).astype(F32)
    neg = jnp.float32(-jnp.inf)
    logits = jnp.where(lane < N_EXPERTS, logits, neg)
    m1 = jnp.max(logits, axis=-1, keepdims=True)
    i1 = jnp.min(jnp.where(logits == m1, lane, float(LANES)), axis=-1, keepdims=True)
    sel1 = lane == i1
    rest = jnp.where(sel1, neg, logits)
    m2 = jnp.max(rest, axis=-1, keepdims=True)
    i2 = jnp.min(jnp.where(rest == m2, lane, float(LANES)), axis=-1, keepdims=True)
    sel2 = lane == i2
    e2 = jnp.exp(m2 - m1)
    den = 1.0 + e2
    sel = (sel1 | sel2).astype(BF16)
    before = (lax.broadcasted_iota(jnp.int32, (tm, tm), 1)
              < lax.broadcasted_iota(jnp.int32, (tm, tm), 0)).astype(BF16)
    carry = carry_ref[...]
    rank = jnp.dot(before, sel, preferred_element_type=F32) + carry[0:1, :]
    r1 = jnp.sum(jnp.where(sel1, rank, 0.0), axis=-1, keepdims=True)
    r2 = jnp.sum(jnp.where(sel2, rank, 0.0), axis=-1, keepdims=True)
    fields = (i1, i2, 1.0 / den, e2 / den, r1, r2)
    info = jnp.zeros_like(logits)
    for li, val in enumerate(fields):
        info = jnp.where(lane == li, val, info)
    info_ref[...] = info
    carry = carry + jnp.sum(sel.astype(F32), axis=0, keepdims=True)
    carry_ref[...] = carry
    cnt_ref[...] = carry


INFO_E1, INFO_E2, INFO_W1, INFO_W2, INFO_R1, INFO_R2 = range(6)


def _in_proj_kernel(a_ref, w_ref, g_ref, o_ref, *, tn):
    j = pl.program_id(1)
    acc = jnp.dot(a_ref[...], w_ref[...], preferred_element_type=F32)
    is_qk = (j >= COL_SQ // tn) & (j < COL_SV // tn)

    @pl.when(is_qk)
    def _():
        for hh in range(tn // SB_HEAD_DIM):
            cs = slice(hh * SB_HEAD_DIM, (hh + 1) * SB_HEAD_DIM)
            o_ref[:, cs] = _head_rmsnorm(acc[:, cs], g_ref[:, cs]).astype(o_ref.dtype)

    @pl.when(jnp.logical_not(is_qk))
    def _():
        o_ref[...] = acc.astype(o_ref.dtype)


def in_proj(h, w, qk_gain, tm=1024, tn=1024):
    n, k = h.shape
    m = w.shape[1]
    assert COL_SQ % tn == 0 and COL_SV % tn == 0 and tn % SB_HEAD_DIM == 0
    return pl.pallas_call(
        functools.partial(_in_proj_kernel, tn=tn),
        out_shape=jax.ShapeDtypeStruct((n, m), BF16),
        grid=(n // tm, m // tn),
        in_specs=[pl.BlockSpec((tm, k), lambda i, j: (i, 0)),
                  pl.BlockSpec((k, tn), lambda i, j: (0, j)),
                  pl.BlockSpec((1, tn), lambda i, j: (0, j))],
        out_specs=pl.BlockSpec((tm, tn), lambda i, j: (i, j)),
        compiler_params=_cparams(("parallel", "arbitrary")),
        name="in_proj",
    )(h, w, qk_gain)


ROW_SLABS = D_MODEL // (2 * LANES)
HALF_D = D_MODEL // 2
_HIGH_HALF = -65536


def _bf16_bits(x):
    return lax.bitcast_convert_type(x.astype(BF16).astype(F32), jnp.int32)


def _store_packed_rows(ref, read_cols, t):
    for s in range(ROW_SLABS):
        lo = (_bf16_bits(read_cols(s * LANES)) >> 16) & 0xFFFF
        hi = _bf16_bits(read_cols(HALF_D + s * LANES)) & _HIGH_HALF
        ref[pl.ds(s, t, stride=ROW_SLABS), :] = lo | hi


def _load_packed_slab(ref, s, t):
    w = ref[pl.ds(s, t, stride=ROW_SLABS), :]
    lo = lax.bitcast_convert_type(w << 16, F32)
    hi = lax.bitcast_convert_type(w & _HIGH_HALF, F32)
    return lo, hi


def _row_proj_kernel(a_ref, w_ref, r_ref, g_ref, o_ref, h_ref, *, pack_rows):
    x = r_ref[...] + jnp.dot(a_ref[...], w_ref[...], preferred_element_type=F32)
    ms = jnp.mean(x * x, axis=-1, keepdims=True)
    o_ref[...] = x
    h = x * lax.rsqrt(ms + EPS) * g_ref[...]
    if pack_rows:
        _store_packed_rows(h_ref, lambda c0: h[:, c0:c0 + LANES], x.shape[0])
    else:
        h_ref[...] = h.astype(h_ref.dtype)


def out_proj(a, w, resid, norm_gain, pack_rows=False, tm=512):
    n, k = a.shape
    d = w.shape[1]
    row = lambda width: pl.BlockSpec((tm, width), lambda i: (i, 0))
    if pack_rows:
        h_shape = jax.ShapeDtypeStruct((n * ROW_SLABS, LANES), jnp.int32)
        h_spec = pl.BlockSpec((tm * ROW_SLABS, LANES), lambda i: (i, 0))
    else:
        h_shape, h_spec = jax.ShapeDtypeStruct((n, d), BF16), row(d)
    return pl.pallas_call(
        functools.partial(_row_proj_kernel, pack_rows=pack_rows),
        out_shape=(jax.ShapeDtypeStruct((n, d), F32), h_shape),
        grid=(n // tm,),
        in_specs=[row(k), pl.BlockSpec((k, d), lambda i: (0, 0), pipeline_mode=pl.Buffered(1)),
                  row(d), pl.BlockSpec((1, d), lambda i: (0, 0))],
        out_specs=(row(d), h_spec),
        compiler_params=_cparams(("parallel",)),
        name="out_proj",
    )(a, w, resid, norm_gain.reshape(1, d))


def _router_kernel(x_ref, g_ref, rhi_ref, rlo_ref, info_ref, cnt_ref, carry_ref):
    x = x_ref[...]
    ms = jnp.mean(x * x, axis=-1, keepdims=True)
    h = x * lax.rsqrt(ms + EPS) * g_ref[...]
    _route_top2(h, rhi_ref, rlo_ref, info_ref, cnt_ref, carry_ref)


def route_tokens(x, g, router, tm=512):
    n, d = x.shape
    r_pad = jnp.zeros((d, LANES), F32).at[:, :N_EXPERTS].set(router)
    fixed = lambda shape: pl.BlockSpec(shape, lambda i: (0, 0))
    return pl.pallas_call(
        _router_kernel,
        out_shape=(jax.ShapeDtypeStruct((n, LANES), F32), jax.ShapeDtypeStruct((8, LANES), F32)),
        grid=(n // tm,),
        in_specs=[pl.BlockSpec((tm, d), lambda i: (i, 0)), fixed((1, d)),
                  fixed((d, LANES)), fixed((d, LANES))],
        out_specs=(pl.BlockSpec((tm, LANES), lambda i: (i, 0)), fixed((8, LANES))),
        scratch_shapes=[pltpu.VMEM((8, LANES), F32)],
        compiler_params=_cparams(("arbitrary",)),
        name="route_tokens",
    )(x, g.reshape(1, d), *_split_bf16(r_pad))


POOL_HALO = 128


def _pool_kernel(u_ref, w_ref, s_ref, o_ref, prev_ref, *, t):
    sb = pl.program_id(1)

    @pl.when(sb == 0)
    def _():
        prev_ref[...] = jnp.zeros_like(prev_ref)

    row = lax.broadcasted_iota(jnp.int32, (t, t), 0)
    col = lax.broadcasted_iota(jnp.int32, (t, t), 1)
    prow = lax.broadcasted_iota(jnp.int32, (t, POOL_HALO), 0)
    pcol = lax.broadcasted_iota(jnp.int32, (t, POOL_HALO), 1) - POOL_HALO
    tg = sb * t + lax.broadcasted_iota(jnp.int32, (t, 1), 0)
    for gi, w in enumerate(POOL_WINDOWS):
        cs = slice(gi * POOL_GROUP_W, (gi + 1) * POOL_GROUP_W)
        u = u_ref[:, cs]
        band_cur = ((col <= row) & (col > row - w)).astype(BF16)
        band_prev = ((pcol > prow - w) & (pcol + sb * t >= 0)).astype(BF16)
        win = jnp.dot(band_cur, u, preferred_element_type=F32)
        win = win + jnp.dot(band_prev, prev_ref[:, cs], preferred_element_type=F32)
        count = jnp.minimum(tg + 1, w).astype(F32)
        pooled = win / count - u.astype(F32)
        mixed = jnp.dot(pooled.astype(BF16), w_ref[gi], preferred_element_type=F32)
        o_ref[:, cs] = (mixed * s_ref[:, cs]).astype(o_ref.dtype)
    prev_ref[...] = u_ref[t - POOL_HALO:, :]


def pool_mixer(z3, pool_w, scale, t=256):
    b, s, _ = z3.shape
    return pl.pallas_call(
        functools.partial(_pool_kernel, t=t),
        out_shape=jax.ShapeDtypeStruct((b, s, POOL_W), BF16),
        grid=(b, s // t),
        in_specs=[pl.BlockSpec((None, t, POOL_W), lambda bi, si: (bi, si, COL_POOL // POOL_W)),
                  pl.BlockSpec((POOL_GROUPS, POOL_GROUP_W, POOL_GROUP_W), lambda bi, si: (0, 0, 0)),
                  pl.BlockSpec((1, POOL_W), lambda bi, si: (0, 0))],
        out_specs=pl.BlockSpec((None, t, POOL_W), lambda bi, si: (bi, si, 0)),
        scratch_shapes=[pltpu.VMEM((POOL_HALO, POOL_W), BF16)],
        compiler_params=_cparams(("parallel", "arbitrary")),
        name="pool_mixer",
    )(z3, pool_w, scale.reshape(1, POOL_W))


def _head_rmsnorm(x, g):
    ms = jnp.mean(x * x, axis=-1, keepdims=True)
    return x * lax.rsqrt(ms + EPS) * g


def _sb_kernel(q_ref, k_ref, v_ref, o_ref, acc_ref, c_ref, *, t, nh):
    i = pl.program_id(2)
    hd = SB_HEAD_DIM
    row = lax.broadcasted_iota(jnp.int32, (t, t), 0)
    col = lax.broadcasted_iota(jnp.int32, (t, t), 1)
    diag_mask = col < row
    suffix = (row > col).astype(BF16)
    suffix2 = jnp.concatenate([suffix, suffix], axis=0)
    has_prev = jnp.broadcast_to(i > 0, (t, t))

    def block(h, start, mask, c):
        hs = slice(h * hd, (h + 1) * hd)
        k = k_ref[pl.ds(start, t), hs]
        z = lax.dot_general(q_ref[:, hs], k, (((1,), (1,)), ((), ())),
                            preferred_element_type=F32)
        soft = jnp.log(1.0 + jnp.exp(-jnp.abs(z)))
        neg_part = jnp.minimum(z, 0.0)
        log_beta = neg_part - soft
        log_keep = (neg_part - z) - soft
        if mask is not None:
            log_keep = jnp.where(mask, log_keep, 0.0)
        hi, lo = _split_bf16(log_keep)
        later = jnp.dot(jnp.concatenate([hi, lo], axis=1), suffix2, preferred_element_type=F32)
        a = jnp.exp(log_beta + later + c)
        if mask is not None:
            a = jnp.where(mask, a, 0.0)
        out = jnp.dot(a.astype(BF16), v_ref[pl.ds(start, t), hs], preferred_element_type=F32)
        return out, c + later[:, 0:1] + log_keep[:, 0:1]

    diag_start = pl.multiple_of(i * t, t)
    prev_start = pl.multiple_of(jnp.maximum(i - 1, 0) * t, t)
    live = jnp.int32(0)
    for h in range(nh):
        out_a, c = block(h, diag_start, diag_mask, jnp.zeros((t, 1), F32))
        out_b, c = block(h, prev_start, has_prev, c)
        acc_ref[:, h * hd:(h + 1) * hd] = out_a + out_b
        c_ref[h] = c
        live = jnp.maximum(live, (jnp.max(c) > SB_F32_ZERO_LOG).astype(jnp.int32))

    def body(carry):
        j, _ = carry
        start = pl.multiple_of(j * t, t)
        live = jnp.int32(0)
        for h in range(nh):
            out, c = block(h, start, None, c_ref[h])
            acc_ref[:, h * hd:(h + 1) * hd] += out
            c_ref[h] = c
            live = jnp.maximum(live, (jnp.max(c) > SB_F32_ZERO_LOG).astype(jnp.int32))
        return j - 1, live

    def cond(carry):
        j, live = carry
        return jnp.logical_and(j >= 0, live > 0)

    lax.while_loop(cond, body, (i - 2, live))
    o_ref[...] = acc_ref[...].astype(o_ref.dtype)


def sb_attention(z3, t=256, nh=4):
    b, s, _ = z3.shape
    w = nh * SB_HEAD_DIM
    return pl.pallas_call(
        functools.partial(_sb_kernel, t=t, nh=nh),
        out_shape=jax.ShapeDtypeStruct((b, s, SB_W), BF16),
        grid=(b, SB_HEADS // nh, s // t),
        in_specs=[pl.BlockSpec((None, t, w), lambda bi, h, i: (bi, i, COL_SQ // w + h)),
                  pl.BlockSpec((None, s, w), lambda bi, h, i: (bi, 0, COL_SK // w + h)),
                  pl.BlockSpec((None, s, w), lambda bi, h, i: (bi, 0, COL_SV // w + h))],
        out_specs=pl.BlockSpec((None, t, w), lambda bi, h, i: (bi, i, h)),
        scratch_shapes=[pltpu.VMEM((t, w), F32), pltpu.VMEM((nh, t, 1), F32)],
        compiler_params=_cparams(("parallel", "parallel", "arbitrary")),
        name="sb_attention",
    )(z3, z3, z3)


def _gla_gate_kernel(h_ref, wg_ref, wlr_hi_ref, wlr_lo_ref, blr_ref, o_ref, *, tm):
    g = jnp.dot(h_ref[...], wg_ref[...], preferred_element_type=F32)
    g_hi, g_lo = _split_bf16(g)
    pre = (jnp.dot(g_hi, wlr_hi_ref[...], preferred_element_type=F32)
           + jnp.dot(g_lo, wlr_hi_ref[...], preferred_element_type=F32)
           + jnp.dot(g_hi, wlr_lo_ref[...], preferred_element_type=F32)) + blr_ref[...]
    log_a = _log_sigmoid(pre) * (1.0 / GLA_TAU)
    row = lax.broadcasted_iota(jnp.int32, (tm, tm), 0)
    col = lax.broadcasted_iota(jnp.int32, (tm, tm), 1)
    shift = GLA_CHUNK.bit_length() - 1
    tri = ((col <= row) & ((col >> shift) == (row >> shift))).astype(BF16)
    hi, lo = _split_bf16(log_a)
    o_ref[...] = (jnp.dot(tri, hi, preferred_element_type=F32)
                  + jnp.dot(tri, lo, preferred_element_type=F32))


def gla_gate(h, w_glr, w_lr, b_lr, tm=256):
    n, d = h.shape
    return pl.pallas_call(
        functools.partial(_gla_gate_kernel, tm=tm),
        out_shape=jax.ShapeDtypeStruct((n, GLA_K_W), F32),
        grid=(n // tm,),
        in_specs=[pl.BlockSpec((tm, d), lambda i: (i, 0)),
                  pl.BlockSpec((d, LANES), lambda i: (0, 0)),
                  pl.BlockSpec((LANES, GLA_K_W), lambda i: (0, 0)),
                  pl.BlockSpec((LANES, GLA_K_W), lambda i: (0, 0)),
                  pl.BlockSpec((1, GLA_K_W), lambda i: (0, 0))],
        out_specs=pl.BlockSpec((tm, GLA_K_W), lambda i: (i, 0)),
        compiler_params=_cparams(("parallel",)),
        name="gla_gate",
    )(h, w_glr, *_split_bf16(w_lr), b_lr.reshape(1, GLA_K_W))


def _gla_kernel(q_ref, k_ref, v_ref, r_ref, b_ref, g_ref, o_ref, st_ref, *, tc):
    @pl.when(pl.program_id(2) == 0)
    def _():
        st_ref[...] = jnp.zeros_like(st_ref)

    c = GLA_CHUNK
    sub = GLA_SUB
    causal = (lax.broadcasted_iota(jnp.int32, (c, c), 1)
              <= lax.broadcasted_iota(jnp.int32, (c, c), 0))
    g_out = g_ref[...]

    def chunk(ci, carry):
        base = pl.multiple_of(ci * c, c)
        rows = pl.ds(base, c)
        q = q_ref[rows, :].astype(F32) * (GLA_DK ** -0.5)
        k = k_ref[rows, :].astype(F32)
        v = v_ref[rows, :]
        b = b_ref[rows, :]
        b_end = b_ref[pl.ds(base + c - 1, 1), :]
        st = st_ref[...]
        inter = lax.dot_general((q * jnp.exp(b)).astype(BF16), st.astype(BF16),
                                (((1,), (1,)), ((), ())), preferred_element_type=F32)
        scores = []
        for si in range(c // sub):
            lo, hi = si * sub, (si + 1) * sub
            if si > 0:
                ref = b_ref[pl.ds(base + lo - 1, 1), :]
            else:
                ref = jnp.zeros((1, GLA_DK), F32)
            q_t = (q[lo:hi] * jnp.exp(b[lo:hi] - ref)).astype(BF16)
            k_t = (k * jnp.exp(jnp.minimum(ref - b, GLA_EXP_CAP))).astype(BF16)
            scores.append(lax.dot_general(q_t, k_t, (((1,), (1,)), ((), ())),
                                          preferred_element_type=F32))
        sc = jnp.where(causal, jnp.concatenate(scores, axis=0), 0.0)
        o = inter + jnp.dot(sc.astype(BF16), v, preferred_element_type=F32)
        k_e = (k * jnp.exp(b_end - b)).astype(BF16)
        v_t = v.astype(F32).T.astype(BF16)
        st_ref[...] = st * jnp.exp(b_end) + jnp.dot(v_t, k_e, preferred_element_type=F32)
        ms = jnp.mean(o * o, axis=-1, keepdims=True)
        o = o * lax.rsqrt(ms + EPS) * g_out
        r = r_ref[rows, :].astype(F32)
        o_ref[rows, :] = (o * (r * _sigmoid(r))).astype(o_ref.dtype)
        return carry

    lax.fori_loop(0, tc // c, chunk, 0, unroll=True)


def gla_mixer(z3, bcum3, g_out, tc=512):
    b, s, _ = z3.shape
    dk, dv = GLA_DK, GLA_DV
    return pl.pallas_call(
        functools.partial(_gla_kernel, tc=tc),
        out_shape=jax.ShapeDtypeStruct((b, s, GLA_V_W), BF16),
        grid=(b, GLA_HEADS, s // tc),
        in_specs=[pl.BlockSpec((None, tc, dk), lambda bi, h, i: (bi, i, COL_GQ // dk + h)),
                  pl.BlockSpec((None, tc, dk), lambda bi, h, i: (bi, i, COL_GK // dk + h)),
                  pl.BlockSpec((None, tc, dv), lambda bi, h, i: (bi, i, COL_GV // dv + h)),
                  pl.BlockSpec((None, tc, dv), lambda bi, h, i: (bi, i, COL_GR // dv + h)),
                  pl.BlockSpec((None, tc, dk), lambda bi, h, i: (bi, i, h)),
                  pl.BlockSpec((1, dv), lambda bi, h, i: (0, 0))],
        out_specs=pl.BlockSpec((None, tc, dv), lambda bi, h, i: (bi, i, h)),
        scratch_shapes=[pltpu.VMEM((dv, dk), F32)],
        compiler_params=_cparams(("parallel", "parallel", "arbitrary")),
        name="gla_mixer",
    )(z3, z3, z3, z3, bcum3, g_out.reshape(1, dv))


def _merge_kernel(h_ref, wga_ref, wgb_ref, wgc_ref, ba_ref, bb_ref, bc_ref,
                  yp_ref, ys_ref, yg_ref, wp_ref, ws_ref, wg_ref, o_ref):
    h = h_ref[...]
    m = None
    for wgate_ref, b_ref, y_ref, wup_ref in ((wga_ref, ba_ref, yp_ref, wp_ref),
                                             (wgb_ref, bb_ref, ys_ref, ws_ref),
                                             (wgc_ref, bc_ref, yg_ref, wg_ref)):
        gate = _sigmoid(jnp.dot(h, wgate_ref[...], preferred_element_type=F32) + b_ref[...])
        term = gate * jnp.dot(y_ref[...], wup_ref[...], preferred_element_type=F32)
        m = term if m is None else m + term
    o_ref[...] = m.astype(o_ref.dtype)


def merge_branches(h, w_gate, b_gate, y_pool, y_sb, y_gla, w_p, w_s, w_g, tm=1024, tn=512):
    n, d = h.shape
    nj = d // tn
    gate_w_spec = lambda br: pl.BlockSpec((d, tn), lambda i, j: (0, br * nj + j))
    gate_b_spec = lambda br: pl.BlockSpec((1, tn), lambda i, j: (0, br * nj + j))
    y_spec = lambda w: pl.BlockSpec((tm, w), lambda i, j: (i, 0))
    w_spec = lambda w: pl.BlockSpec((w, tn), lambda i, j: (0, j))
    b_gate = b_gate.reshape(1, N_BRANCH * d)
    return pl.pallas_call(
        _merge_kernel,
        out_shape=jax.ShapeDtypeStruct((n, d), BF16),
        grid=(n // tm, nj),
        in_specs=[y_spec(d), gate_w_spec(0), gate_w_spec(1), gate_w_spec(2),
                  gate_b_spec(0), gate_b_spec(1), gate_b_spec(2),
                  y_spec(POOL_W), y_spec(SB_W), y_spec(GLA_V_W),
                  w_spec(POOL_W), w_spec(SB_W), w_spec(GLA_V_W)],
        out_specs=pl.BlockSpec((tm, tn), lambda i, j: (i, j)),
        compiler_params=_cparams(("parallel", "arbitrary")),
        name="merge_branches",
    )(h, w_gate, w_gate, w_gate, b_gate, b_gate, b_gate, y_pool, y_sb, y_gla, w_p, w_s, w_g)


def _silu(a):
    return a * _sigmoid(a)


def _ffn_kernel(h_ref, w1_ref, w3_ref, w2_ref, o_ref):
    @pl.when(pl.program_id(1) == 0)
    def _():
        o_ref[...] = jnp.zeros_like(o_ref)

    h = h_ref[...]
    a = jnp.dot(h, w1_ref[...], preferred_element_type=F32)
    b = jnp.dot(h, w3_ref[...], preferred_element_type=F32)
    g = (_silu(a) * b).astype(BF16)
    o_ref[...] += jnp.dot(g, w2_ref[...], preferred_element_type=F32)


def ffn_dense(h, w1, w3, w2, tm=1024, tf=512):
    n, d = h.shape
    ff = w1.shape[1]
    return pl.pallas_call(
        _ffn_kernel,
        out_shape=jax.ShapeDtypeStruct((n, d), F32),
        grid=(n // tm, ff // tf),
        in_specs=[pl.BlockSpec((tm, d), lambda i, f: (i, 0)),
                  pl.BlockSpec((d, tf), lambda i, f: (0, f)),
                  pl.BlockSpec((d, tf), lambda i, f: (0, f)),
                  pl.BlockSpec((tf, d), lambda i, f: (f, 0))],
        out_specs=pl.BlockSpec((tm, d), lambda i, f: (i, 0)),
        compiler_params=_cparams(("parallel", "arbitrary")),
        name="ffn_dense",
    )(h, w1, w3, w2)


MOE_TM = 1024
TOP_K = 2


def _row_copy(src_ref, src_idx, dst_ref, dst_idx, sem):
    return pltpu.make_async_copy(src_ref.at[src_idx], dst_ref.at[dst_idx], sem)


def _dispatch_kernel(dest_ref, h_ref, init_ref, o_ref, sem, *, tm):
    del init_ref
    base = pl.program_id(0) * tm

    def copies(t):
        return [_row_copy(h_ref, t, o_ref, dest_ref[TOP_K * (base + t) + kk], sem)
                for kk in range(TOP_K)]

    def start(t, carry):
        for kk, cp in enumerate(copies(t)):
            cp.start(priority=kk % 2)
        return carry

    def wait(t, carry):
        for cp in copies(t):
            cp.wait()
        return carry

    lax.fori_loop(0, tm, start, 0)
    lax.fori_loop(0, tm, wait, 0)


def moe_dispatch(dest, h_rows, n_rows, tm=256):
    n = h_rows.shape[0]
    grid_spec = pltpu.PrefetchScalarGridSpec(
        num_scalar_prefetch=1,
        grid=(n // tm,),
        in_specs=[pl.BlockSpec((tm, ROW_SLABS, LANES), lambda i, dest: (i, 0, 0)),
                  pl.BlockSpec(memory_space=pl.ANY)],
        out_specs=pl.BlockSpec(memory_space=pl.ANY),
        scratch_shapes=[pltpu.SemaphoreType.DMA],
    )
    return pl.pallas_call(
        functools.partial(_dispatch_kernel, tm=tm),
        out_shape=jax.ShapeDtypeStruct((n_rows, ROW_SLABS, LANES), h_rows.dtype),
        grid_spec=grid_spec,
        input_output_aliases={2: 0},
        compiler_params=_cparams(("arbitrary",)),
        name="moe_dispatch",
    )(dest, h_rows, jnp.zeros((n_rows, ROW_SLABS, LANES), h_rows.dtype))


def _moe_ffn_kernel(te_ref, nu_ref, h_ref, w1_ref, w3_ref, w2_ref, o_ref, hb_ref, acc_ref, *, tm):
    del te_ref
    f = pl.program_id(1)
    last = pl.num_programs(1) - 1
    used = pl.program_id(0) < nu_ref[0]

    @pl.when(used & (f == 0))
    def _():
        acc_ref[...] = jnp.zeros_like(acc_ref)
        for s in range(ROW_SLABS):
            lo, hi = _load_packed_slab(h_ref, s, tm)
            hb_ref[:, s * LANES:(s + 1) * LANES] = lo.astype(BF16)
            hb_ref[:, HALF_D + s * LANES:HALF_D + (s + 1) * LANES] = hi.astype(BF16)

    @pl.when(used)
    def _():
        h = hb_ref[...]
        a = jnp.dot(h, w1_ref[...], preferred_element_type=F32)
        b = jnp.dot(h, w3_ref[...], preferred_element_type=F32)
        g = (_silu(a) * b).astype(BF16)
        acc_ref[...] += jnp.dot(g, w2_ref[...], preferred_element_type=F32)

    @pl.when(used & (f == last))
    def _():
        _store_packed_rows(o_ref, lambda c0: acc_ref[:, c0:c0 + LANES], tm)

    @pl.when(jnp.logical_not(used) & (f == last))
    def _():
        o_ref[...] = jnp.zeros_like(o_ref)


def moe_grouped_ffn(tile_expert, n_used, h_sorted, w1, w3, w2, tm=MOE_TM, tf=512):
    r = h_sorted.shape[0] // ROW_SLABS
    d, ff = w1.shape[1], w1.shape[2]
    nf = ff // tf

    def f_idx(i, f, nu):
        return jnp.where(i < nu[0], f, nf - 1)

    rows = pl.BlockSpec((tm * ROW_SLABS, LANES), lambda i, f, te, nu: (i, 0))
    grid_spec = pltpu.PrefetchScalarGridSpec(
        num_scalar_prefetch=2,
        grid=(r // tm, nf),
        in_specs=[rows,
                  pl.BlockSpec((None, d, tf), lambda i, f, te, nu: (te[i], 0, f_idx(i, f, nu))),
                  pl.BlockSpec((None, d, tf), lambda i, f, te, nu: (te[i], 0, f_idx(i, f, nu))),
                  pl.BlockSpec((None, tf, d), lambda i, f, te, nu: (te[i], f_idx(i, f, nu), 0))],
        out_specs=rows,
        scratch_shapes=[pltpu.VMEM((tm, d), BF16), pltpu.VMEM((tm, d), F32)],
    )
    return pl.pallas_call(
        functools.partial(_moe_ffn_kernel, tm=tm),
        out_shape=jax.ShapeDtypeStruct(h_sorted.shape, jnp.int32),
        grid_spec=grid_spec,
        compiler_params=_cparams(("arbitrary", "arbitrary")),
        name="moe_grouped_ffn",
    )(tile_expert, n_used, h_sorted, w1, w3, w2)


def _combine_kernel(dest_ref, w_ref, y_ref, o_ref, buf_ref, sems, *, tm):
    base = pl.program_id(0) * tm

    def copies(t):
        return [_row_copy(y_ref, dest_ref[TOP_K * (base + t) + kk], buf_ref.at[kk], t, sems.at[kk])
                for kk in range(TOP_K)]

    def start(t, carry):
        for kk, cp in enumerate(copies(t)):
            cp.start(priority=kk % 2)
        return carry

    def wait(t, carry):
        for cp in copies(t):
            cp.wait()
        return carry

    lax.fori_loop(0, tm, start, 0)
    lax.fori_loop(0, tm, wait, 0)
    lo = hi = None
    for kk in range(TOP_K):
        words = buf_ref[kk]
        w = w_ref[kk]
        lo_k = lax.bitcast_convert_type(words << 16, F32) * w
        hi_k = lax.bitcast_convert_type(words & _HIGH_HALF, F32) * w
        lo = lo_k if lo is None else lo + lo_k
        hi = hi_k if hi is None else hi + hi_k
    o_ref[0] = lo
    o_ref[1] = hi


def moe_combine(dest, w_rows, y_rows, n, tm=256):
    grid_spec = pltpu.PrefetchScalarGridSpec(
        num_scalar_prefetch=1,
        grid=(n // tm,),
        in_specs=[pl.BlockSpec((TOP_K, tm, 1, LANES), lambda i, dest: (0, i, 0, 0)),
                  pl.BlockSpec(memory_space=pl.ANY)],
        out_specs=pl.BlockSpec((2, tm, ROW_SLABS, LANES), lambda i, dest: (0, i, 0, 0)),
        scratch_shapes=[pltpu.VMEM((TOP_K, tm, ROW_SLABS, LANES), jnp.int32),
                        pltpu.SemaphoreType.DMA((TOP_K,))],
    )
    return pl.pallas_call(
        functools.partial(_combine_kernel, tm=tm),
        out_shape=jax.ShapeDtypeStruct((2, n, ROW_SLABS, LANES), F32),
        grid_spec=grid_spec,
        compiler_params=_cparams(("arbitrary",)),
        name="moe_combine",
    )(dest, w_rows, y_rows)


def moe_sparse(h_rows, info, counts, w1, w3, w2):
    n = h_rows.shape[0] // ROW_SLABS
    tm = MOE_TM
    n_tiles = (TOP_K * n) // tm + N_EXPERTS
    n_rows = n_tiles * tm
    cnt = counts[0, :N_EXPERTS].astype(jnp.int32)
    tiles_per = (cnt + tm - 1) // tm
    tile_end = jnp.cumsum(tiles_per)
    group_start = (tile_end - tiles_per) * tm
    n_used = tile_end[-1:]
    tile_expert = jnp.minimum(
        jnp.searchsorted(tile_end, jnp.arange(n_tiles, dtype=jnp.int32), side="right"),
        N_EXPERTS - 1).astype(jnp.int32)
    tile_expert = jnp.where(jnp.arange(n_tiles) < n_used[0], tile_expert,
                            tile_expert[jnp.maximum(n_used[0] - 1, 0)])
    experts = info[:, INFO_E1:INFO_E2 + 1].astype(jnp.int32)
    ranks = info[:, INFO_R1:INFO_R2 + 1].astype(jnp.int32)
    dest = (group_start[experts] + ranks).reshape(TOP_K * n)
    w_rows = jnp.broadcast_to(info[:, INFO_W1:INFO_W2 + 1].T[:, :, None, None],
                              (TOP_K, n, 1, LANES))

    h_sorted = moe_dispatch(dest, h_rows.reshape(n, ROW_SLABS, LANES), n_rows)
    y_sorted = moe_grouped_ffn(tile_expert, n_used,
                               h_sorted.reshape(n_rows * ROW_SLABS, LANES), w1, w3, w2)
    y = moe_combine(dest, w_rows, y_sorted.reshape(n_rows, ROW_SLABS, LANES), n)
    return y.reshape(2, n * ROW_SLABS, LANES)


def _ple_kernel(x_ref, y_ref, p_ref, wg_ref, wp_ref, *rest, y_slabs, with_norm):
    if with_norm:
        g_ref, o_ref, h_ref = rest
    else:
        (o_ref,) = rest
    if y_slabs:
        t = x_ref.shape[0]
        parts = []
        for half in range(2):
            for s in range(ROW_SLABS):
                parts.append(y_ref[half, pl.ds(s, t, stride=ROW_SLABS), :])
        x = x_ref[...] + jnp.concatenate(parts, axis=1)
    else:
        x = x_ref[...] + y_ref[...]
    xb = x.astype(BF16)
    pb = p_ref[...].astype(BF16)
    d = x.shape[1]
    sum_sq = jnp.zeros((x.shape[0], 1), F32)
    for c0 in range(0, d, PLE_COL_CHUNK):
        cs = slice(c0, c0 + PLE_COL_CHUNK)
        gate = _sigmoid(jnp.dot(xb, wg_ref[:, cs], preferred_element_type=F32))
        proj = jnp.dot(pb, wp_ref[:, cs], preferred_element_type=F32)
        xc = x[:, cs] + gate * proj
        o_ref[:, cs] = xc
        sum_sq += jnp.sum(xc * xc, axis=-1, keepdims=True)
    if with_norm:
        inv = lax.rsqrt(sum_sq * (1.0 / d) + EPS)
        h_ref[...] = (o_ref[...] * inv * g_ref[...]).astype(h_ref.dtype)


PLE_COL_CHUNK = 512


def ple_update(x, y, p, w_gate, w_proj, norm_gain=None, tm=512):
    n, d = x.shape
    pd = p.shape[1]
    y_slabs = y.ndim == 3
    with_norm = norm_gain is not None
    row = lambda width: pl.BlockSpec((tm, width), lambda i: (i, 0))
    resident = lambda shape: pl.BlockSpec(shape, lambda i: (0, 0), pipeline_mode=pl.Buffered(1))
    y_spec = (pl.BlockSpec((2, tm * ROW_SLABS, LANES), lambda i: (0, i, 0)) if y_slabs else row(d))
    in_specs = [row(d), y_spec, row(pd), resident((d, d)), resident((pd, d))]
    args = [x, y, p, w_gate, w_proj]
    out_shape = [jax.ShapeDtypeStruct((n, d), F32)]
    out_specs = [row(d)]
    if with_norm:
        in_specs.append(pl.BlockSpec((1, d), lambda i: (0, 0)))
        args.append(norm_gain.reshape(1, d))
        out_shape.append(jax.ShapeDtypeStruct((n, d), BF16))
        out_specs.append(row(d))
    res = pl.pallas_call(
        functools.partial(_ple_kernel, y_slabs=y_slabs, with_norm=with_norm),
        out_shape=tuple(out_shape),
        grid=(n // tm,),
        in_specs=in_specs,
        out_specs=tuple(out_specs),
        compiler_params=_cparams(("parallel",)),
        name="ple_update",
    )(*args)
    return res if with_norm else res[0]


def _split_w_in(w_in_i):
    c_lr = COL_GR
    main = jnp.concatenate([w_in_i[:, :c_lr], w_in_i[:, c_lr + GLA_RANK:]], axis=1).astype(BF16)
    glr = jnp.zeros((D_MODEL, LANES), BF16).at[:, :GLA_RANK].set(
        w_in_i[:, c_lr:c_lr + GLA_RANK].astype(BF16))
    return main, glr


def kernel(x, p, g_mix, w_in, w_branch_gate, b_branch_gate, pool_w, pool_scale, sb_gq, sb_gk,
           gla_w_lr, gla_b_lr, gla_g_out, w_up_pool, w_up_sb, w_up_gla, w_o, g_ffn,
           ffn_w1, ffn_w3, ffn_w2, moe_router, moe_w1, moe_w3, moe_w2, ple_w_proj, ple_w_gate):
    bsz, seq, d = x.shape
    n = bsz * seq
    depth = w_in.shape[0]
    xf = x.reshape(n, d)
    h = rmsnorm(xf, g_mix[0])
    for i in range(depth):
        w_main, w_glr = _split_w_in(w_in[i])
        w_lr_pad = jnp.zeros((LANES, GLA_K_W), F32).at[:GLA_RANK].set(gla_w_lr[i])
        qk_gain = jnp.ones((1, Z_COLS), F32)
        qk_gain = qk_gain.at[0, COL_SQ:COL_SK].set(jnp.tile(sb_gq[i] * SB_HEAD_DIM ** -0.5, SB_HEADS))
        qk_gain = qk_gain.at[0, COL_SK:COL_SV].set(jnp.tile(sb_gk[i], SB_HEADS))

        z = in_proj(h, w_main, qk_gain)
        bcum = gla_gate(h, w_glr, w_lr_pad, gla_b_lr[i])

        z3 = z.reshape(bsz, seq, Z_COLS)
        y_pool = pool_mixer(z3, pool_w[i].astype(BF16), pool_scale[i])
        y_sb = sb_attention(z3)
        y_gla = gla_mixer(z3, bcum.reshape(bsz, seq, GLA_K_W), gla_g_out[i])

        merged = merge_branches(h, w_branch_gate[i].astype(BF16), b_branch_gate[i],
                                y_pool.reshape(n, POOL_W), y_sb.reshape(n, SB_W),
                                y_gla.reshape(n, GLA_V_W), w_up_pool[i].astype(BF16),
                                w_up_sb[i].astype(BF16), w_up_gla[i].astype(BF16))

        j = i // 2
        if i % 2 == 0:
            xf, h2 = out_proj(merged, w_o[i].astype(BF16), xf, g_ffn[i])
            y = ffn_dense(h2, ffn_w1[j].astype(BF16), ffn_w3[j].astype(BF16),
                          ffn_w2[j].astype(BF16))
        else:
            xf, h2_rows = out_proj(merged, w_o[i].astype(BF16), xf, g_ffn[i], pack_rows=True)
            info, counts = route_tokens(xf, g_ffn[i], moe_router[j])
            y = moe_sparse(h2_rows, info, counts, moe_w1[j].astype(BF16),
                           moe_w3[j].astype(BF16), moe_w2[j].astype(BF16))

        if i + 1 < depth:
            xf, h = ple_update(xf, y, p[i].reshape(n, PLE_DIM), ple_w_gate[i].astype(BF16),
                               ple_w_proj[i].astype(BF16), norm_gain=g_mix[i + 1])
        else:
            xf = ple_update(xf, y, p[i].reshape(n, PLE_DIM), ple_w_gate[i].astype(BF16),
                            ple_w_proj[i].astype(BF16))
    return xf.reshape(bsz, seq, d)
```

```python
import functools

import jax
import jax.numpy as jnp
from jax import lax
from jax.experimental import pallas as pl
from jax.experimental.pallas import tpu as pltpu

F32 = jnp.float32
BF16 = jnp.bfloat16

EPS = 1e-6
D_MODEL = 2048
PLE_DIM = 256
POOL_GROUPS = 4
POOL_GROUP_W = 256
POOL_W = POOL_GROUPS * POOL_GROUP_W
POOL_WINDOWS = (2, 4, 8, 16)
SB_HEADS = 8
SB_HEAD_DIM = 128
SB_W = SB_HEADS * SB_HEAD_DIM
GLA_HEADS = 4
GLA_DK = 128
GLA_DV = 256
GLA_K_W = GLA_HEADS * GLA_DK
GLA_V_W = GLA_HEADS * GLA_DV
GLA_RANK = 16
GLA_TAU = 16.0
D_FF = 5632
N_EXPERTS = 8
N_BRANCH = 3

LANES = 128
VMEM_LIMIT = 56 * 1024 * 1024

COL_POOL = 0
COL_SQ = COL_POOL + POOL_W
COL_SK = COL_SQ + SB_W
COL_SV = COL_SK + SB_W
COL_GQ = COL_SV + SB_W
COL_GK = COL_GQ + GLA_K_W
COL_GV = COL_GK + GLA_K_W
COL_GR = COL_GV + GLA_V_W
Z_COLS = COL_GR + GLA_V_W

GLA_CHUNK = 64
GLA_SUB = 16
GLA_EXP_CAP = 80.0
SB_F32_ZERO_LOG = -104.0


def _cparams(sem):
    return pltpu.CompilerParams(dimension_semantics=sem, vmem_limit_bytes=VMEM_LIMIT)


def _log_sigmoid(z):
    return jnp.minimum(z, 0.0) - jnp.log(1.0 + jnp.exp(-jnp.abs(z)))


def _sigmoid(z):
    return 1.0 / (1.0 + jnp.exp(-z))


def _split_bf16(x):
    hi = x.astype(BF16)
    lo = (x - hi.astype(F32)).astype(BF16)
    return hi, lo


def _rmsnorm_kernel(x_ref, g_ref, o_ref):
    x = x_ref[...]
    ms = jnp.mean(x * x, axis=-1, keepdims=True)
    o_ref[...] = (x * lax.rsqrt(ms + EPS) * g_ref[...]).astype(o_ref.dtype)


def rmsnorm(x, g, tm=512):
    n, d = x.shape
    return pl.pallas_call(
        _rmsnorm_kernel,
        out_shape=jax.ShapeDtypeStruct((n, d), BF16),
        grid=(n // tm,),
        in_specs=[pl.BlockSpec((tm, d), lambda i: (i, 0)),
                  pl.BlockSpec((1, d), lambda i: (0, 0))],
        out_specs=pl.BlockSpec((tm, d), lambda i: (i, 0)),
        compiler_params=_cparams(("parallel",)),
        name="rmsnorm",
    )(x, g.reshape(1, d))


def _route_top2(h, rhi_ref, rlo_ref, info_ref, cnt_ref, carry_ref):
    tm = h.shape[0]

    @pl.when(pl.program_id(0) == 0)
    def _():
        carry_ref[...] = jnp.zeros_like(carry_ref)

    h_hi, h_lo = _split_bf16(h)
    logits = (jnp.dot(h_hi, rhi_ref[...], preferred_element_type=F32)
              + jnp.dot(h_lo, rhi_ref[...], preferred_element_type=F32)
              + jnp.dot(h_hi, rlo_ref[...], preferred_element_type=F32))
    lane = lax.broadcasted_iota(jnp.int32, logits.shape, 1).astype(F32)
    neg = jnp.float32(-jnp.inf)
    logits = jnp.where(lane < N_EXPERTS, logits, neg)
    m1 = jnp.max(logits, axis=-1, keepdims=True)
    i1 = jnp.min(jnp.where(logits == m1, lane, float(LANES)), axis=-1, keepdims=True)
    sel1 = lane == i1
    rest = jnp.where(sel1, neg, logits)
    m2 = jnp.max(rest, axis=-1, keepdims=True)
    i2 = jnp.min(jnp.where(rest == m2, lane, float(LANES)), axis=-1, keepdims=True)
    sel2 = lane == i2
    e2 = jnp.exp(m2 - m1)
    den = 1.0 + e2
    sel = (sel1 | sel2).astype(BF16)
    before = (lax.broadcasted_iota(jnp.int32, (tm, tm), 1)
              < lax.broadcasted_iota(jnp.int32, (tm, tm), 0)).astype(BF16)
    carry = carry_ref[...]
    rank = jnp.dot(before, sel, preferred_element_type=F32) + carry[0:1, :]
    r1 = jnp.sum(jnp.where(sel1, rank, 0.0), axis=-1, keepdims=True)
    r2 = jnp.sum(jnp.where(sel2, rank, 0.0), axis=-1, keepdims=True)
    fields = (i1, i2, 1.0 / den, e2 / den, r1, r2)
    info = jnp.zeros_like(logits)
    for li, val in enumerate(fields):
        info = jnp.where(lane == li, val, info)
    info_ref[...] = info
    carry = carry + jnp.sum(sel.astype(F32), axis=0, keepdims=True)
    carry_ref[...] = carry
    cnt_ref[...] = carry


INFO_E1, INFO_E2, INFO_W1, INFO_W2, INFO_R1, INFO_R2 = range(6)


def _in_proj_kernel(a_ref, w_ref, g_ref, o_ref, *, tn):
    j = pl.program_id(1)
    acc = jnp.dot(a_ref[...], w_ref[...], preferred_element_type=F32)
    is_qk = (j >= COL_SQ // tn) & (j < COL_SV // tn)

    @pl.when(is_qk)
    def _():
        for hh in range(tn // SB_HEAD_DIM):
            cs = slice(hh * SB_HEAD_DIM, (hh + 1) * SB_HEAD_DIM)
            o_ref[:, cs] = _head_rmsnorm(acc[:, cs], g_ref[:, cs]).astype(o_ref.dtype)

    @pl.when(jnp.logical_not(is_qk))
    def _():
        o_ref[...] = acc.astype(o_ref.dtype)


def in_proj(h, w, qk_gain, tm=1024, tn=1024):
    n, k = h.shape
    m = w.shape[1]
    assert COL_SQ % tn == 0 and COL_SV % tn == 0 and tn % SB_HEAD_DIM == 0
    return pl.pallas_call(
        functools.partial(_in_proj_kernel, tn=tn),
        out_shape=jax.ShapeDtypeStruct((n, m), BF16),
        grid=(n // tm, m // tn),
        in_specs=[pl.BlockSpec((tm, k), lambda i, j: (i, 0)),
                  pl.BlockSpec((k, tn), lambda i, j: (0, j)),
                  pl.BlockSpec((1, tn), lambda i, j: (0, j))],
        out_specs=pl.BlockSpec((tm, tn), lambda i, j: (i, j)),
        compiler_params=_cparams(("parallel", "arbitrary")),
        name="in_proj",
    )(h, w, qk_gain)


ROW_SLABS = D_MODEL // (2 * LANES)
HALF_D = D_MODEL // 2
_HIGH_HALF = -65536


def _bf16_bits(x):
    return lax.bitcast_convert_type(x.astype(BF16).astype(F32), jnp.int32)


def _store_packed_rows(ref, read_cols, t):
    for s in range(ROW_SLABS):
        lo = (_bf16_bits(read_cols(s * LANES)) >> 16) & 0xFFFF
        hi = _bf16_bits(read_cols(HALF_D + s * LANES)) & _HIGH_HALF
        ref[pl.ds(s, t, stride=ROW_SLABS), :] = lo | hi


def _load_packed_slab(ref, s, t):
    w = ref[pl.ds(s, t, stride=ROW_SLABS), :]
    lo = lax.bitcast_convert_type(w << 16, F32)
    hi = lax.bitcast_convert_type(w & _HIGH_HALF, F32)
    return lo, hi


def _row_proj_kernel(a_ref, w_ref, r_ref, g_ref, o_ref, h_ref, *, pack_rows):
    x = r_ref[...] + jnp.dot(a_ref[...], w_ref[...], preferred_element_type=F32)
    ms = jnp.mean(x * x, axis=-1, keepdims=True)
    o_ref[...] = x
    h = x * lax.rsqrt(ms + EPS) * g_ref[...]
    if pack_rows:
        _store_packed_rows(h_ref, lambda c0: h[:, c0:c0 + LANES], x.shape[0])
    else:
        h_ref[...] = h.astype(h_ref.dtype)


def out_proj(a, w, resid, norm_gain, pack_rows=False, tm=512):
    n, k = a.shape
    d = w.shape[1]
    row = lambda width: pl.BlockSpec((tm, width), lambda i: (i, 0))
    if pack_rows:
        h_shape = jax.ShapeDtypeStruct((n * ROW_SLABS, LANES), jnp.int32)
        h_spec = pl.BlockSpec((tm * ROW_SLABS, LANES), lambda i: (i, 0))
    else:
        h_shape, h_spec = jax.ShapeDtypeStruct((n, d), BF16), row(d)
    return pl.pallas_call(
        functools.partial(_row_proj_kernel, pack_rows=pack_rows),
        out_shape=(jax.ShapeDtypeStruct((n, d), F32), h_shape),
        grid=(n // tm,),
        in_specs=[row(k), pl.BlockSpec((k, d), lambda i: (0, 0), pipeline_mode=pl.Buffered(1)),
                  row(d), pl.BlockSpec((1, d), lambda i: (0, 0))],
        out_specs=(row(d), h_spec),
        compiler_params=_cparams(("parallel",)),
        name="out_proj",
    )(a, w, resid, norm_gain.reshape(1, d))


def _router_kernel(x_ref, g_ref, rhi_ref, rlo_ref, info_ref, cnt_ref, carry_ref):
    x = x_ref[...]
    ms = jnp.mean(x * x, axis=-1, keepdims=True)
    h = x * lax.rsqrt(ms + EPS) * g_ref[...]
    _route_top2(h, rhi_ref, rlo_ref, info_ref, cnt_ref, carry_ref)


def route_tokens(x, g, router, tm=512):
    n, d = x.shape
    r_pad = jnp.zeros((d, LANES), F32).at[:, :N_EXPERTS].set(router)
    fixed = lambda shape: pl.BlockSpec(shape, lambda i: (0, 0))
    return pl.pallas_call(
        _router_kernel,
        out_shape=(jax.ShapeDtypeStruct((n, LANES), F32), jax.ShapeDtypeStruct((8, LANES), F32)),
        grid=(n // tm,),
        in_specs=[pl.BlockSpec((tm, d), lambda i: (i, 0)), fixed((1, d)),
                  fixed((d, LANES)), fixed((d, LANES))],
        out_specs=(pl.BlockSpec((tm, LANES), lambda i: (i, 0)), fixed((8, LANES))),
        scratch_shapes=[pltpu.VMEM((8, LANES), F32)],
        compiler_params=_cparams(("arbitrary",)),
        name="route_tokens",
    )(x, g.reshape(1, d), *_split_bf16(r_pad))


POOL_HALO = 128


def _pool_kernel(u_ref, w_ref, s_ref, o_ref, prev_ref, *, t):
    sb = pl.program_id(1)

    @pl.when(sb == 0)
    def _():
        prev_ref[...] = jnp.zeros_like(prev_ref)

    row = lax.broadcasted_iota(jnp.int32, (t, t), 0)
    col = lax.broadcasted_iota(jnp.int32, (t, t), 1)
    prow = lax.broadcasted_iota(jnp.int32, (t, POOL_HALO), 0)
    pcol = lax.broadcasted_iota(jnp.int32, (t, POOL_HALO), 1) - POOL_HALO
    tg = sb * t + lax.broadcasted_iota(jnp.int32, (t, 1), 0)
    for gi, w in enumerate(POOL_WINDOWS):
        cs = slice(gi * POOL_GROUP_W, (gi + 1) * POOL_GROUP_W)
        u = u_ref[:, cs]
        band_cur = ((col <= row) & (col > row - w)).astype(BF16)
        band_prev = ((pcol > prow - w) & (pcol + sb * t >= 0)).astype(BF16)
        win = jnp.dot(band_cur, u, preferred_element_type=F32)
        win = win + jnp.dot(band_prev, prev_ref[:, cs], preferred_element_type=F32)
        count = jnp.minimum(tg + 1, w).astype(F32)
        pooled = win / count - u.astype(F32)
        mixed = jnp.dot(pooled.astype(BF16), w_ref[gi], preferred_element_type=F32)
        o_ref[:, cs] = (mixed * s_ref[:, cs]).astype(o_ref.dtype)
    prev_ref[...] = u_ref[t - POOL_HALO:, :]


def pool_mixer(z3, pool_w, scale, t=256):
    b, s, _ = z3.shape
    return pl.pallas_call(
        functools.partial(_pool_kernel, t=t),
        out_shape=jax.ShapeDtypeStruct((b, s, POOL_W), BF16),
        grid=(b, s // t),
        in_specs=[pl.BlockSpec((None, t, POOL_W), lambda bi, si: (bi, si, COL_POOL // POOL_W)),
                  pl.BlockSpec((POOL_GROUPS, POOL_GROUP_W, POOL_GROUP_W), lambda bi, si: (0, 0, 0)),
                  pl.BlockSpec((1, POOL_W), lambda bi, si: (0, 0))],
        out_specs=pl.BlockSpec((None, t, POOL_W), lambda bi, si: (bi, si, 0)),
        scratch_shapes=[pltpu.VMEM((POOL_HALO, POOL_W), BF16)],
        compiler_params=_cparams(("parallel", "arbitrary")),
        name="pool_mixer",
    )(z3, pool_w, scale.reshape(1, POOL_W))


def _head_rmsnorm(x, g):
    ms = jnp.mean(x * x, axis=-1, keepdims=True)
    return x * lax.rsqrt(ms + EPS) * g


def _sb_kernel(q_ref, k_ref, v_ref, o_ref, acc_ref, c_ref, *, t, nh):
    i = pl.program_id(2)
    hd = SB_HEAD_DIM
    row = lax.broadcasted_iota(jnp.int32, (t, t), 0)
    col = lax.broadcasted_iota(jnp.int32, (t, t), 1)
    diag_mask = col < row
    suffix = (row > col).astype(BF16)
    suffix2 = jnp.concatenate([suffix, suffix], axis=0)
    has_prev = jnp.broadcast_to(i > 0, (t, t))

    def block(h, start, mask, c):
        hs = slice(h * hd, (h + 1) * hd)
        k = k_ref[pl.ds(start, t), hs]
        z = lax.dot_general(q_ref[:, hs], k, (((1,), (1,)), ((), ())),
                            preferred_element_type=F32)
        soft = jnp.log(1.0 + jnp.exp(-jnp.abs(z)))
        neg_part = jnp.minimum(z, 0.0)
        log_beta = neg_part - soft
        log_keep = (neg_part - z) - soft
        if mask is not None:
            log_keep = jnp.where(mask, log_keep, 0.0)
        hi, lo = _split_bf16(log_keep)
        later = jnp.dot(jnp.concatenate([hi, lo], axis=1), suffix2, preferred_element_type=F32)
        a = jnp.exp(log_beta + later + c)
        if mask is not None:
            a = jnp.where(mask, a, 0.0)
        out = jnp.dot(a.astype(BF16), v_ref[pl.ds(start, t), hs], preferred_element_type=F32)
        return out, c + later[:, 0:1] + log_keep[:, 0:1]

    diag_start = pl.multiple_of(i * t, t)
    prev_start = pl.multiple_of(jnp.maximum(i - 1, 0) * t, t)
    live = jnp.int32(0)
    for h in range(nh):
        out_a, c = block(h, diag_start, diag_mask, jnp.zeros((t, 1), F32))
        out_b, c = block(h, prev_start, has_prev, c)
        acc_ref[:, h * hd:(h + 1) * hd] = out_a + out_b
        c_ref[h] = c
        live = jnp.maximum(live, (jnp.max(c) > SB_F32_ZERO_LOG).astype(jnp.int32))

    def body(carry):
        j, _ = carry
        start = pl.multiple_of(j * t, t)
        live = jnp.int32(0)
        for h in range(nh):
            out, c = block(h, start, None, c_ref[h])
            acc_ref[:, h * hd:(h + 1) * hd] += out
            c_ref[h] = c
            live = jnp.maximum(live, (jnp.max(c) > SB_F32_ZERO_LOG).astype(jnp.int32))
        return j - 1, live

    def cond(carry):
        j, live = carry
        return jnp.logical_and(j >= 0, live > 0)

    lax.while_loop(cond, body, (i - 2, live))
    o_ref[...] = acc_ref[...].astype(o_ref.dtype)


def sb_attention(z3, t=256, nh=8):
    b, s, _ = z3.shape
    w = nh * SB_HEAD_DIM
    kv_spec = lambda col: pl.BlockSpec((None, s, w), lambda bi, h, i: (bi, 0, col // w + h),
                                       pipeline_mode=pl.Buffered(1))
    return pl.pallas_call(
        functools.partial(_sb_kernel, t=t, nh=nh),
        out_shape=jax.ShapeDtypeStruct((b, s, SB_W), BF16),
        grid=(b, SB_HEADS // nh, s // t),
        in_specs=[pl.BlockSpec((None, t, w), lambda bi, h, i: (bi, i, COL_SQ // w + h)),
                  kv_spec(COL_SK), kv_spec(COL_SV)],
        out_specs=pl.BlockSpec((None, t, w), lambda bi, h, i: (bi, i, h)),
        scratch_shapes=[pltpu.VMEM((t, w), F32), pltpu.VMEM((nh, t, 1), F32)],
        compiler_params=_cparams(("parallel", "parallel", "arbitrary")),
        name="sb_attention",
    )(z3, z3, z3)


def _gla_gate_kernel(h_ref, wg_ref, wlr_hi_ref, wlr_lo_ref, blr_ref, o_ref, *, tm):
    g = jnp.dot(h_ref[...], wg_ref[...], preferred_element_type=F32)
    g_hi, g_lo = _split_bf16(g)
    pre = (jnp.dot(g_hi, wlr_hi_ref[...], preferred_element_type=F32)
           + jnp.dot(g_lo, wlr_hi_ref[...], preferred_element_type=F32)
           + jnp.dot(g_hi, wlr_lo_ref[...], preferred_element_type=F32)) + blr_ref[...]
    log_a = _log_sigmoid(pre) * (1.0 / GLA_TAU)
    row = lax.broadcasted_iota(jnp.int32, (tm, tm), 0)
    col = lax.broadcasted_iota(jnp.int32, (tm, tm), 1)
    shift = GLA_CHUNK.bit_length() - 1
    tri = ((col <= row) & ((col >> shift) == (row >> shift))).astype(BF16)
    hi, lo = _split_bf16(log_a)
    o_ref[...] = (jnp.dot(tri, hi, preferred_element_type=F32)
                  + jnp.dot(tri, lo, preferred_element_type=F32))


def gla_gate(h, w_glr, w_lr, b_lr, tm=256):
    n, d = h.shape
    return pl.pallas_call(
        functools.partial(_gla_gate_kernel, tm=tm),
        out_shape=jax.ShapeDtypeStruct((n, GLA_K_W), F32),
        grid=(n // tm,),
        in_specs=[pl.BlockSpec((tm, d), lambda i: (i, 0)),
                  pl.BlockSpec((d, LANES), lambda i: (0, 0)),
                  pl.BlockSpec((LANES, GLA_K_W), lambda i: (0, 0)),
                  pl.BlockSpec((LANES, GLA_K_W), lambda i: (0, 0)),
                  pl.BlockSpec((1, GLA_K_W), lambda i: (0, 0))],
        out_specs=pl.BlockSpec((tm, GLA_K_W), lambda i: (i, 0)),
        compiler_params=_cparams(("parallel",)),
        name="gla_gate",
    )(h, w_glr, *_split_bf16(w_lr), b_lr.reshape(1, GLA_K_W))


def _gla_kernel(q_ref, k_ref, v_ref, r_ref, b_ref, g_ref, o_ref, st_ref, *, tc):
    @pl.when(pl.program_id(2) == 0)
    def _():
        st_ref[...] = jnp.zeros_like(st_ref)

    c = GLA_CHUNK
    sub = GLA_SUB
    causal = (lax.broadcasted_iota(jnp.int32, (c, c), 1)
              <= lax.broadcasted_iota(jnp.int32, (c, c), 0))
    g_out = g_ref[...]

    def chunk(ci, carry):
        base = pl.multiple_of(ci * c, c)
        rows = pl.ds(base, c)
        q = q_ref[rows, :].astype(F32) * (GLA_DK ** -0.5)
        k = k_ref[rows, :].astype(F32)
        v = v_ref[rows, :]
        b = b_ref[rows, :]
        b_end = b_ref[pl.ds(base + c - 1, 1), :]
        st = st_ref[...]
        inter = lax.dot_general((q * jnp.exp(b)).astype(BF16), st.astype(BF16),
                                (((1,), (1,)), ((), ())), preferred_element_type=F32)
        scores = []
        for si in range(c // sub):
            lo, hi = si * sub, (si + 1) * sub
            if si > 0:
                ref = b_ref[pl.ds(base + lo - 1, 1), :]
            else:
                ref = jnp.zeros((1, GLA_DK), F32)
            q_t = (q[lo:hi] * jnp.exp(b[lo:hi] - ref)).astype(BF16)
            k_t = (k * jnp.exp(jnp.minimum(ref - b, GLA_EXP_CAP))).astype(BF16)
            scores.append(lax.dot_general(q_t, k_t, (((1,), (1,)), ((), ())),
                                          preferred_element_type=F32))
        sc = jnp.where(causal, jnp.concatenate(scores, axis=0), 0.0)
        o = inter + jnp.dot(sc.astype(BF16), v, preferred_element_type=F32)
        k_e = (k * jnp.exp(b_end - b)).astype(BF16)
        v_t = v.astype(F32).T.astype(BF16)
        st_ref[...] = st * jnp.exp(b_end) + jnp.dot(v_t, k_e, preferred_element_type=F32)
        ms = jnp.mean(o * o, axis=-1, keepdims=True)
        o = o * lax.rsqrt(ms + EPS) * g_out
        r = r_ref[rows, :].astype(F32)
        o_ref[rows, :] = (o * (r * _sigmoid(r))).astype(o_ref.dtype)
        return carry

    lax.fori_loop(0, tc // c, chunk, 0, unroll=True)


def gla_mixer(z3, bcum3, g_out, tc=1024):
    b, s, _ = z3.shape
    dk, dv = GLA_DK, GLA_DV
    return pl.pallas_call(
        functools.partial(_gla_kernel, tc=tc),
        out_shape=jax.ShapeDtypeStruct((b, s, GLA_V_W), BF16),
        grid=(b, GLA_HEADS, s // tc),
        in_specs=[pl.BlockSpec((None, tc, dk), lambda bi, h, i: (bi, i, COL_GQ // dk + h)),
                  pl.BlockSpec((None, tc, dk), lambda bi, h, i: (bi, i, COL_GK // dk + h)),
                  pl.BlockSpec((None, tc, dv), lambda bi, h, i: (bi, i, COL_GV // dv + h)),
                  pl.BlockSpec((None, tc, dv), lambda bi, h, i: (bi, i, COL_GR // dv + h)),
                  pl.BlockSpec((None, tc, dk), lambda bi, h, i: (bi, i, h)),
                  pl.BlockSpec((1, dv), lambda bi, h, i: (0, 0))],
        out_specs=pl.BlockSpec((None, tc, dv), lambda bi, h, i: (bi, i, h)),
        scratch_shapes=[pltpu.VMEM((dv, dk), F32)],
        compiler_params=_cparams(("parallel", "parallel", "arbitrary")),
        name="gla_mixer",
    )(z3, z3, z3, z3, bcum3, g_out.reshape(1, dv))


def _merge_kernel(h_ref, wga_ref, wgb_ref, wgc_ref, ba_ref, bb_ref, bc_ref,
                  yp_ref, ys_ref, yg_ref, wp_ref, ws_ref, wg_ref, o_ref):
    h = h_ref[...]
    m = None
    for wgate_ref, b_ref, y_ref, wup_ref in ((wga_ref, ba_ref, yp_ref, wp_ref),
                                             (wgb_ref, bb_ref, ys_ref, ws_ref),
                                             (wgc_ref, bc_ref, yg_ref, wg_ref)):
        gate = _sigmoid(jnp.dot(h, wgate_ref[...], preferred_element_type=F32) + b_ref[...])
        term = gate * jnp.dot(y_ref[...], wup_ref[...], preferred_element_type=F32)
        m = term if m is None else m + term
    o_ref[...] = m.astype(o_ref.dtype)


def merge_branches(h, w_gate, b_gate, y_pool, y_sb, y_gla, w_p, w_s, w_g, tm=1024, tn=512):
    n, d = h.shape
    nj = d // tn
    gate_w_spec = lambda br: pl.BlockSpec((d, tn), lambda i, j: (0, br * nj + j))
    gate_b_spec = lambda br: pl.BlockSpec((1, tn), lambda i, j: (0, br * nj + j))
    y_spec = lambda w: pl.BlockSpec((tm, w), lambda i, j: (i, 0))
    w_spec = lambda w: pl.BlockSpec((w, tn), lambda i, j: (0, j))
    b_gate = b_gate.reshape(1, N_BRANCH * d)
    return pl.pallas_call(
        _merge_kernel,
        out_shape=jax.ShapeDtypeStruct((n, d), BF16),
        grid=(n // tm, nj),
        in_specs=[y_spec(d), gate_w_spec(0), gate_w_spec(1), gate_w_spec(2),
                  gate_b_spec(0), gate_b_spec(1), gate_b_spec(2),
                  y_spec(POOL_W), y_spec(SB_W), y_spec(GLA_V_W),
                  w_spec(POOL_W), w_spec(SB_W), w_spec(GLA_V_W)],
        out_specs=pl.BlockSpec((tm, tn), lambda i, j: (i, j)),
        compiler_params=_cparams(("parallel", "arbitrary")),
        name="merge_branches",
    )(h, w_gate, w_gate, w_gate, b_gate, b_gate, b_gate, y_pool, y_sb, y_gla, w_p, w_s, w_g)


def _silu(a):
    return a * _sigmoid(a)


def _ffn_kernel(h_ref, w1_ref, w3_ref, w2_ref, o_ref):
    @pl.when(pl.program_id(1) == 0)
    def _():
        o_ref[...] = jnp.zeros_like(o_ref)

    h = h_ref[...]
    a = jnp.dot(h, w1_ref[...], preferred_element_type=F32)
    b = jnp.dot(h, w3_ref[...], preferred_element_type=F32)
    g = (_silu(a) * b).astype(BF16)
    o_ref[...] += jnp.dot(g, w2_ref[...], preferred_element_type=F32)


def ffn_dense(h, w1, w3, w2, tm=1024, tf=512):
    n, d = h.shape
    ff = w1.shape[1]
    return pl.pallas_call(
        _ffn_kernel,
        out_shape=jax.ShapeDtypeStruct((n, d), F32),
        grid=(n // tm, ff // tf),
        in_specs=[pl.BlockSpec((tm, d), lambda i, f: (i, 0)),
                  pl.BlockSpec((d, tf), lambda i, f: (0, f)),
                  pl.BlockSpec((d, tf), lambda i, f: (0, f)),
                  pl.BlockSpec((tf, d), lambda i, f: (f, 0))],
        out_specs=pl.BlockSpec((tm, d), lambda i, f: (i, 0)),
        compiler_params=_cparams(("parallel", "arbitrary")),
        name="ffn_dense",
    )(h, w1, w3, w2)


MOE_TM = 1024
TOP_K = 2


def _row_copy(src_ref, src_idx, dst_ref, dst_idx, sem):
    return pltpu.make_async_copy(src_ref.at[src_idx], dst_ref.at[dst_idx], sem)


def _dispatch_kernel(dest_ref, h_ref, init_ref, o_ref, sem, *, tm):
    del init_ref
    base = pl.program_id(0) * tm

    def copies(t):
        return [_row_copy(h_ref, t, o_ref, dest_ref[TOP_K * (base + t) + kk], sem)
                for kk in range(TOP_K)]

    def start(t, carry):
        for cp in copies(t):
            cp.start()
        return carry

    def wait(t, carry):
        for cp in copies(t):
            cp.wait()
        return carry

    lax.fori_loop(0, tm, start, 0)
    lax.fori_loop(0, tm, wait, 0)


def moe_dispatch(dest, h_rows, n_rows, tm=1024):
    n = h_rows.shape[0]
    grid_spec = pltpu.PrefetchScalarGridSpec(
        num_scalar_prefetch=1,
        grid=(n // tm,),
        in_specs=[pl.BlockSpec((tm, ROW_SLABS, LANES), lambda i, dest: (i, 0, 0)),
                  pl.BlockSpec(memory_space=pl.ANY)],
        out_specs=pl.BlockSpec(memory_space=pl.ANY),
        scratch_shapes=[pltpu.SemaphoreType.DMA],
    )
    return pl.pallas_call(
        functools.partial(_dispatch_kernel, tm=tm),
        out_shape=jax.ShapeDtypeStruct((n_rows, ROW_SLABS, LANES), h_rows.dtype),
        grid_spec=grid_spec,
        input_output_aliases={2: 0},
        compiler_params=_cparams(("arbitrary",)),
        name="moe_dispatch",
    )(dest, h_rows, jnp.zeros((n_rows, ROW_SLABS, LANES), h_rows.dtype))


def _moe_ffn_kernel(te_ref, nu_ref, h_ref, w1_ref, w3_ref, w2_ref, o_ref, hb_ref, acc_ref, *, tm):
    del te_ref
    f = pl.program_id(1)
    last = pl.num_programs(1) - 1
    used = pl.program_id(0) < nu_ref[0]

    @pl.when(used & (f == 0))
    def _():
        acc_ref[...] = jnp.zeros_like(acc_ref)
        for s in range(ROW_SLABS):
            lo, hi = _load_packed_slab(h_ref, s, tm)
            hb_ref[:, s * LANES:(s + 1) * LANES] = lo.astype(BF16)
            hb_ref[:, HALF_D + s * LANES:HALF_D + (s + 1) * LANES] = hi.astype(BF16)

    @pl.when(used)
    def _():
        h = hb_ref[...]
        a = jnp.dot(h, w1_ref[...], preferred_element_type=F32)
        b = jnp.dot(h, w3_ref[...], preferred_element_type=F32)
        g = (_silu(a) * b).astype(BF16)
        acc_ref[...] += jnp.dot(g, w2_ref[...], preferred_element_type=F32)

    @pl.when(used & (f == last))
    def _():
        _store_packed_rows(o_ref, lambda c0: acc_ref[:, c0:c0 + LANES], tm)

    @pl.when(jnp.logical_not(used) & (f == last))
    def _():
        o_ref[...] = jnp.zeros_like(o_ref)


def moe_grouped_ffn(tile_expert, n_used, h_sorted, w1, w3, w2, tm=MOE_TM, tf=512):
    r = h_sorted.shape[0] // ROW_SLABS
    d, ff = w1.shape[1], w1.shape[2]
    nf = ff // tf

    def f_idx(i, f, nu):
        return jnp.where(i < nu[0], f, nf - 1)

    rows = pl.BlockSpec((tm * ROW_SLABS, LANES), lambda i, f, te, nu: (i, 0))
    grid_spec = pltpu.PrefetchScalarGridSpec(
        num_scalar_prefetch=2,
        grid=(r // tm, nf),
        in_specs=[rows,
                  pl.BlockSpec((None, d, tf), lambda i, f, te, nu: (te[i], 0, f_idx(i, f, nu))),
                  pl.BlockSpec((None, d, tf), lambda i, f, te, nu: (te[i], 0, f_idx(i, f, nu))),
                  pl.BlockSpec((None, tf, d), lambda i, f, te, nu: (te[i], f_idx(i, f, nu), 0))],
        out_specs=rows,
        scratch_shapes=[pltpu.VMEM((tm, d), BF16), pltpu.VMEM((tm, d), F32)],
    )
    return pl.pallas_call(
        functools.partial(_moe_ffn_kernel, tm=tm),
        out_shape=jax.ShapeDtypeStruct(h_sorted.shape, jnp.int32),
        grid_spec=grid_spec,
        compiler_params=_cparams(("arbitrary", "arbitrary")),
        name="moe_grouped_ffn",
    )(tile_expert, n_used, h_sorted, w1, w3, w2)


def _combine_kernel(dest_ref, w_ref, y_ref, o_ref, buf_ref, sems, *, tm):
    base = pl.program_id(0) * tm

    def copies(t):
        return [_row_copy(y_ref, dest_ref[TOP_K * (base + t) + kk], buf_ref.at[kk], t, sems.at[kk])
                for kk in range(TOP_K)]

    def start(t, carry):
        for cp in copies(t):
            cp.start()
        return carry

    def wait(t, carry):
        for cp in copies(t):
            cp.wait()
        return carry

    lax.fori_loop(0, tm, start, 0)
    lax.fori_loop(0, tm, wait, 0)
    lo = hi = None
    for kk in range(TOP_K):
        words = buf_ref[kk]
        w = w_ref[kk]
        lo_k = lax.bitcast_convert_type(words << 16, F32) * w
        hi_k = lax.bitcast_convert_type(words & _HIGH_HALF, F32) * w
        lo = lo_k if lo is None else lo + lo_k
        hi = hi_k if hi is None else hi + hi_k
    o_ref[0] = lo
    o_ref[1] = hi


def moe_combine(dest, w_rows, y_rows, n, tm=512):
    grid_spec = pltpu.PrefetchScalarGridSpec(
        num_scalar_prefetch=1,
        grid=(n // tm,),
        in_specs=[pl.BlockSpec((TOP_K, tm, 1, LANES), lambda i, dest: (0, i, 0, 0)),
                  pl.BlockSpec(memory_space=pl.ANY)],
        out_specs=pl.BlockSpec((2, tm, ROW_SLABS, LANES), lambda i, dest: (0, i, 0, 0)),
        scratch_shapes=[pltpu.VMEM((TOP_K, tm, ROW_SLABS, LANES), jnp.int32),
                        pltpu.SemaphoreType.DMA((TOP_K,))],
    )
    return pl.pallas_call(
        functools.partial(_combine_kernel, tm=tm),
        out_shape=jax.ShapeDtypeStruct((2, n, ROW_SLABS, LANES), F32),
        grid_spec=grid_spec,
        compiler_params=_cparams(("arbitrary",)),
        name="moe_combine",
    )(dest, w_rows, y_rows)


def moe_sparse(h_rows, info, counts, w1, w3, w2):
    n = h_rows.shape[0] // ROW_SLABS
    tm = MOE_TM
    n_tiles = (TOP_K * n) // tm + N_EXPERTS
    n_rows = n_tiles * tm
    cnt = counts[0, :N_EXPERTS].astype(jnp.int32)
    tiles_per = (cnt + tm - 1) // tm
    tile_end = jnp.cumsum(tiles_per)
    group_start = (tile_end - tiles_per) * tm
    n_used = tile_end[-1:]
    tile_expert = jnp.minimum(
        jnp.searchsorted(tile_end, jnp.arange(n_tiles, dtype=jnp.int32), side="right"),
        N_EXPERTS - 1).astype(jnp.int32)
    tile_expert = jnp.where(jnp.arange(n_tiles) < n_used[0], tile_expert,
                            tile_expert[jnp.maximum(n_used[0] - 1, 0)])
    experts = info[:, INFO_E1:INFO_E2 + 1].astype(jnp.int32)
    ranks = info[:, INFO_R1:INFO_R2 + 1].astype(jnp.int32)
    dest = (group_start[experts] + ranks).reshape(TOP_K * n)
    w_rows = jnp.broadcast_to(info[:, INFO_W1:INFO_W2 + 1].T[:, :, None, None],
                              (TOP_K, n, 1, LANES))

    h_sorted = moe_dispatch(dest, h_rows.reshape(n, ROW_SLABS, LANES), n_rows)
    y_sorted = moe_grouped_ffn(tile_expert, n_used,
                               h_sorted.reshape(n_rows * ROW_SLABS, LANES), w1, w3, w2)
    y = moe_combine(dest, w_rows, y_sorted.reshape(n_rows, ROW_SLABS, LANES), n)
    return y.reshape(2, n * ROW_SLABS, LANES)


def _ple_kernel(x_ref, y_ref, p_ref, wg_ref, wp_ref, *rest, y_slabs, with_norm):
    if with_norm:
        g_ref, o_ref, h_ref = rest
    else:
        (o_ref,) = rest
    if y_slabs:
        t = x_ref.shape[0]
        parts = []
        for half in range(2):
            for s in range(ROW_SLABS):
                parts.append(y_ref[half, pl.ds(s, t, stride=ROW_SLABS), :])
        x = x_ref[...] + jnp.concatenate(parts, axis=1)
    else:
        x = x_ref[...] + y_ref[...]
    xb = x.astype(BF16)
    pb = p_ref[...].astype(BF16)
    d = x.shape[1]
    sum_sq = jnp.zeros((x.shape[0], 1), F32)
    for c0 in range(0, d, PLE_COL_CHUNK):
        cs = slice(c0, c0 + PLE_COL_CHUNK)
        gate = _sigmoid(jnp.dot(xb, wg_ref[:, cs], preferred_element_type=F32))
        proj = jnp.dot(pb, wp_ref[:, cs], preferred_element_type=F32)
        xc = x[:, cs] + gate * proj
        o_ref[:, cs] = xc
        sum_sq += jnp.sum(xc * xc, axis=-1, keepdims=True)
    if with_norm:
        inv = lax.rsqrt(sum_sq * (1.0 / d) + EPS)
        h_ref[...] = (o_ref[...] * inv * g_ref[...]).astype(h_ref.dtype)


PLE_COL_CHUNK = 512


def ple_update(x, y, p, w_gate, w_proj, norm_gain=None, tm=512):
    n, d = x.shape
    pd = p.shape[1]
    y_slabs = y.ndim == 3
    with_norm = norm_gain is not None
    row = lambda width: pl.BlockSpec((tm, width), lambda i: (i, 0))
    resident = lambda shape: pl.BlockSpec(shape, lambda i: (0, 0), pipeline_mode=pl.Buffered(1))
    y_spec = (pl.BlockSpec((2, tm * ROW_SLABS, LANES), lambda i: (0, i, 0)) if y_slabs else row(d))
    in_specs = [row(d), y_spec, row(pd), resident((d, d)), resident((pd, d))]
    args = [x, y, p, w_gate, w_proj]
    out_shape = [jax.ShapeDtypeStruct((n, d), F32)]
    out_specs = [row(d)]
    if with_norm:
        in_specs.append(pl.BlockSpec((1, d), lambda i: (0, 0)))
        args.append(norm_gain.reshape(1, d))
        out_shape.append(jax.ShapeDtypeStruct((n, d), BF16))
        out_specs.append(row(d))
    res = pl.pallas_call(
        functools.partial(_ple_kernel, y_slabs=y_slabs, with_norm=with_norm),
        out_shape=tuple(out_shape),
        grid=(n // tm,),
        in_specs=in_specs,
        out_specs=tuple(out_specs),
        compiler_params=_cparams(("parallel",)),
        name="ple_update",
    )(*args)
    return res if with_norm else res[0]


def _split_w_in(w_in_i):
    c_lr = COL_GR
    main = jnp.concatenate([w_in_i[:, :c_lr], w_in_i[:, c_lr + GLA_RANK:]], axis=1).astype(BF16)
    glr = jnp.zeros((D_MODEL, LANES), BF16).at[:, :GLA_RANK].set(
        w_in_i[:, c_lr:c_lr + GLA_RANK].astype(BF16))
    return main, glr


def kernel(x, p, g_mix, w_in, w_branch_gate, b_branch_gate, pool_w, pool_scale, sb_gq, sb_gk,
           gla_w_lr, gla_b_lr, gla_g_out, w_up_pool, w_up_sb, w_up_gla, w_o, g_ffn,
           ffn_w1, ffn_w3, ffn_w2, moe_router, moe_w1, moe_w3, moe_w2, ple_w_proj, ple_w_gate):
    bsz, seq, d = x.shape
    n = bsz * seq
    depth = w_in.shape[0]
    xf = x.reshape(n, d)
    h = rmsnorm(xf, g_mix[0])
    for i in range(depth):
        w_main, w_glr = _split_w_in(w_in[i])
        w_lr_pad = jnp.zeros((LANES, GLA_K_W), F32).at[:GLA_RANK].set(gla_w_lr[i])
        qk_gain = jnp.ones((1, Z_COLS), F32)
        qk_gain = qk_gain.at[0, COL_SQ:COL_SK].set(jnp.tile(sb_gq[i] * SB_HEAD_DIM ** -0.5, SB_HEADS))
        qk_gain = qk_gain.at[0, COL_SK:COL_SV].set(jnp.tile(sb_gk[i], SB_HEADS))

        z = in_proj(h, w_main, qk_gain)
        bcum = gla_gate(h, w_glr, w_lr_pad, gla_b_lr[i])

        z3 = z.reshape(bsz, seq, Z_COLS)
        y_pool = pool_mixer(z3, pool_w[i].astype(BF16), pool_scale[i])
        y_sb = sb_attention(z3)
        y_gla = gla_mixer(z3, bcum.reshape(bsz, seq, GLA_K_W), gla_g_out[i])

        merged = merge_branches(h, w_branch_gate[i].astype(BF16), b_branch_gate[i],
                                y_pool.reshape(n, POOL_W), y_sb.reshape(n, SB_W),
                                y_gla.reshape(n, GLA_V_W), w_up_pool[i].astype(BF16),
                                w_up_sb[i].astype(BF16), w_up_gla[i].astype(BF16))

        j = i // 2
        if i % 2 == 0:
            xf, h2 = out_proj(merged, w_o[i].astype(BF16), xf, g_ffn[i])
            y = ffn_dense(h2, ffn_w1[j].astype(BF16), ffn_w3[j].astype(BF16),
                          ffn_w2[j].astype(BF16))
        else:
            xf, h2_rows = out_proj(merged, w_o[i].astype(BF16), xf, g_ffn[i], pack_rows=True)
            info, counts = route_tokens(xf, g_ffn[i], moe_router[j])
            y = moe_sparse(h2_rows, info, counts, moe_w1[j].astype(BF16),
                           moe_w3[j].astype(BF16), moe_w2[j].astype(BF16))

        if i + 1 < depth:
            xf, h = ple_update(xf, y, p[i].reshape(n, PLE_DIM), ple_w_gate[i].astype(BF16),
                               ple_w_proj[i].astype(BF16), norm_gain=g_mix[i + 1])
        else:
            xf = ple_update(xf, y, p[i].reshape(n, PLE_DIM), ple_w_gate[i].astype(BF16),
                            ple_w_proj[i].astype(BF16))
    return xf.reshape(bsz, seq, d)
```

```python
import functools

import jax
import jax.numpy as jnp
from jax import lax
from jax.experimental import pallas as pl
from jax.experimental.pallas import tpu as pltpu

F32 = jnp.float32
BF16 = jnp.bfloat16

EPS = 1e-6
D_MODEL = 2048
PLE_DIM = 256
POOL_GROUPS = 4
POOL_GROUP_W = 256
POOL_W = POOL_GROUPS * POOL_GROUP_W
POOL_WINDOWS = (2, 4, 8, 16)
SB_HEADS = 8
SB_HEAD_DIM = 128
SB_W = SB_HEADS * SB_HEAD_DIM
GLA_HEADS = 4
GLA_DK = 128
GLA_DV = 256
GLA_K_W = GLA_HEADS * GLA_DK
GLA_V_W = GLA_HEADS * GLA_DV
GLA_RANK = 16
GLA_TAU = 16.0
D_FF = 5632
N_EXPERTS = 8
N_BRANCH = 3

LANES = 128
VMEM_LIMIT = 56 * 1024 * 1024

COL_POOL = 0
COL_SQ = COL_POOL + POOL_W
COL_SK = COL_SQ + SB_W
COL_SV = COL_SK + SB_W
COL_GQ = COL_SV + SB_W
COL_GK = COL_GQ + GLA_K_W
COL_GV = COL_GK + GLA_K_W
COL_GR = COL_GV + GLA_V_W
Z_COLS = COL_GR + GLA_V_W

GLA_CHUNK = 64
GLA_SUB = 16
GLA_EXP_CAP = 80.0
SB_F32_ZERO_LOG = -104.0


def _cparams(sem):
    return pltpu.CompilerParams(dimension_semantics=sem, vmem_limit_bytes=VMEM_LIMIT)


def _log_sigmoid(z):
    return jnp.minimum(z, 0.0) - jnp.log(1.0 + jnp.exp(-jnp.abs(z)))


def _sigmoid(z):
    return 1.0 / (1.0 + jnp.exp(-z))


def _split_bf16(x):
    hi = x.astype(BF16)
    lo = (x - hi.astype(F32)).astype(BF16)
    return hi, lo


def _rmsnorm_kernel(x_ref, g_ref, o_ref):
    x = x_ref[...]
    ms = jnp.mean(x * x, axis=-1, keepdims=True)
    o_ref[...] = (x * lax.rsqrt(ms + EPS) * g_ref[...]).astype(o_ref.dtype)


def rmsnorm(x, g, tm=512):
    n, d = x.shape
    return pl.pallas_call(
        _rmsnorm_kernel,
        out_shape=jax.ShapeDtypeStruct((n, d), BF16),
        grid=(n // tm,),
        in_specs=[pl.BlockSpec((tm, d), lambda i: (i, 0)),
                  pl.BlockSpec((1, d), lambda i: (0, 0))],
        out_specs=pl.BlockSpec((tm, d), lambda i: (i, 0)),
        compiler_params=_cparams(("parallel",)),
        name="rmsnorm",
    )(x, g.reshape(1, d))


def _route_top2(h, rhi_ref, rlo_ref, info_ref, cnt_ref, carry_ref):
    tm = h.shape[0]

    @pl.when(pl.program_id(0) == 0)
    def _():
        carry_ref[...] = jnp.zeros_like(carry_ref)

    h_hi, h_lo = _split_bf16(h)
    logits = (jnp.dot(h_hi, rhi_ref[...], preferred_element_type=F32)
              + jnp.dot(h_lo, rhi_ref[...], preferred_element_type=F32)
              + jnp.dot(h_hi, rlo_ref[...], preferred_element_type=F32))
    lane = lax.broadcasted_iota(jnp.int32, logits.shape, 1).astype(F32)
    neg = jnp.float32(-jnp.inf)
    logits = jnp.where(lane < N_EXPERTS, logits, neg)
    m1 = jnp.max(logits, axis=-1, keepdims=True)
    i1 = jnp.min(jnp.where(logits == m1, lane, float(LANES)), axis=-1, keepdims=True)
    sel1 = lane == i1
    rest = jnp.where(sel1, neg, logits)
    m2 = jnp.max(rest, axis=-1, keepdims=True)
    i2 = jnp.min(jnp.where(rest == m2, lane, float(LANES)), axis=-1, keepdims=True)
    sel2 = lane == i2
    e2 = jnp.exp(m2 - m1)
    den = 1.0 + e2
    sel = (sel1 | sel2).astype(BF16)
    before = (lax.broadcasted_iota(jnp.int32, (tm, tm), 1)
              < lax.broadcasted_iota(jnp.int32, (tm, tm), 0)).astype(BF16)
    carry = carry_ref[...]
    rank = jnp.dot(before, sel, preferred_element_type=F32) + carry[0:1, :]
    r1 = jnp.sum(jnp.where(sel1, rank, 0.0), axis=-1, keepdims=True)
    r2 = jnp.sum(jnp.where(sel2, rank, 0.0), axis=-1, keepdims=True)
    fields = (i1, i2, 1.0 / den, e2 / den, r1, r2)
    info = jnp.zeros_like(logits)
    for li, val in enumerate(fields):
        info = jnp.where(lane == li, val, info)
    info_ref[...] = info
    carry = carry + jnp.sum(sel.astype(F32), axis=0, keepdims=True)
    carry_ref[...] = carry
    cnt_ref[...] = carry


INFO_E1, INFO_E2, INFO_W1, INFO_W2, INFO_R1, INFO_R2 = range(6)


def _in_proj_kernel(a_ref, w_ref, g_ref, o_ref, *, tn):
    j = pl.program_id(1)
    acc = jnp.dot(a_ref[...], w_ref[...], preferred_element_type=F32)
    is_qk = (j >= COL_SQ // tn) & (j < COL_SV // tn)
    for hh in range(tn // SB_HEAD_DIM):
        cs = slice(hh * SB_HEAD_DIM, (hh + 1) * SB_HEAD_DIM)
        blk = acc[:, cs]
        o_ref[:, cs] = jnp.where(is_qk, _head_rmsnorm(blk, g_ref[:, cs]), blk).astype(o_ref.dtype)


def in_proj(h, w, qk_gain, tm=1024, tn=1024):
    n, k = h.shape
    m = w.shape[1]
    assert COL_SQ % tn == 0 and COL_SV % tn == 0 and tn % SB_HEAD_DIM == 0
    return pl.pallas_call(
        functools.partial(_in_proj_kernel, tn=tn),
        out_shape=jax.ShapeDtypeStruct((n, m), BF16),
        grid=(n // tm, m // tn),
        in_specs=[pl.BlockSpec((tm, k), lambda i, j: (i, 0)),
                  pl.BlockSpec((k, tn), lambda i, j: (0, j)),
                  pl.BlockSpec((1, tn), lambda i, j: (0, j))],
        out_specs=pl.BlockSpec((tm, tn), lambda i, j: (i, j)),
        compiler_params=_cparams(("parallel", "arbitrary")),
        name="in_proj",
    )(h, w, qk_gain)


ROW_SLABS = D_MODEL // (2 * LANES)
HALF_D = D_MODEL // 2
_HIGH_HALF = -65536


def _bf16_bits(x):
    return lax.bitcast_convert_type(x.astype(BF16).astype(F32), jnp.int32)


def _store_packed_rows(ref, read_cols, t):
    for s in range(ROW_SLABS):
        lo = (_bf16_bits(read_cols(s * LANES)) >> 16) & 0xFFFF
        hi = _bf16_bits(read_cols(HALF_D + s * LANES)) & _HIGH_HALF
        ref[pl.ds(s, t, stride=ROW_SLABS), :] = lo | hi


def _load_packed_slab(ref, s, t):
    w = ref[pl.ds(s, t, stride=ROW_SLABS), :]
    lo = lax.bitcast_convert_type(w << 16, F32)
    hi = lax.bitcast_convert_type(w & _HIGH_HALF, F32)
    return lo, hi


def _row_proj_kernel(a_ref, w_ref, r_ref, g_ref, o_ref, h_ref, *, pack_rows):
    x = r_ref[...] + jnp.dot(a_ref[...], w_ref[...], preferred_element_type=F32)
    ms = jnp.mean(x * x, axis=-1, keepdims=True)
    o_ref[...] = x
    h = x * lax.rsqrt(ms + EPS) * g_ref[...]
    if pack_rows:
        _store_packed_rows(h_ref, lambda c0: h[:, c0:c0 + LANES], x.shape[0])
    else:
        h_ref[...] = h.astype(h_ref.dtype)


def out_proj(a, w, resid, norm_gain, pack_rows=False, tm=512):
    n, k = a.shape
    d = w.shape[1]
    row = lambda width: pl.BlockSpec((tm, width), lambda i: (i, 0))
    if pack_rows:
        h_shape = jax.ShapeDtypeStruct((n * ROW_SLABS, LANES), jnp.int32)
        h_spec = pl.BlockSpec((tm * ROW_SLABS, LANES), lambda i: (i, 0))
    else:
        h_shape, h_spec = jax.ShapeDtypeStruct((n, d), BF16), row(d)
    return pl.pallas_call(
        functools.partial(_row_proj_kernel, pack_rows=pack_rows),
        out_shape=(jax.ShapeDtypeStruct((n, d), F32), h_shape),
        grid=(n // tm,),
        in_specs=[row(k), pl.BlockSpec((k, d), lambda i: (0, 0), pipeline_mode=pl.Buffered(1)),
                  row(d), pl.BlockSpec((1, d), lambda i: (0, 0))],
        out_specs=(row(d), h_spec),
        compiler_params=_cparams(("parallel",)),
        name="out_proj",
    )(a, w, resid, norm_gain.reshape(1, d))


def _router_kernel(x_ref, g_ref, rhi_ref, rlo_ref, info_ref, cnt_ref, carry_ref):
    x = x_ref[...]
    ms = jnp.mean(x * x, axis=-1, keepdims=True)
    h = x * lax.rsqrt(ms + EPS) * g_ref[...]
    _route_top2(h, rhi_ref, rlo_ref, info_ref, cnt_ref, carry_ref)


def route_tokens(x, g, router, tm=512):
    n, d = x.shape
    r_pad = jnp.zeros((d, LANES), F32).at[:, :N_EXPERTS].set(router)
    fixed = lambda shape: pl.BlockSpec(shape, lambda i: (0, 0))
    return pl.pallas_call(
        _router_kernel,
        out_shape=(jax.ShapeDtypeStruct((n, LANES), F32), jax.ShapeDtypeStruct((8, LANES), F32)),
        grid=(n // tm,),
        in_specs=[pl.BlockSpec((tm, d), lambda i: (i, 0)), fixed((1, d)),
                  fixed((d, LANES)), fixed((d, LANES))],
        out_specs=(pl.BlockSpec((tm, LANES), lambda i: (i, 0)), fixed((8, LANES))),
        scratch_shapes=[pltpu.VMEM((8, LANES), F32)],
        compiler_params=_cparams(("arbitrary",)),
        name="route_tokens",
    )(x, g.reshape(1, d), *_split_bf16(r_pad))


POOL_HALO = 128


def _pool_kernel(u_ref, w_ref, s_ref, o_ref, prev_ref, *, t):
    sb = pl.program_id(1)

    @pl.when(sb == 0)
    def _():
        prev_ref[...] = jnp.zeros_like(prev_ref)

    row = lax.broadcasted_iota(jnp.int32, (t, t), 0)
    col = lax.broadcasted_iota(jnp.int32, (t, t), 1)
    prow = lax.broadcasted_iota(jnp.int32, (t, POOL_HALO), 0)
    pcol = lax.broadcasted_iota(jnp.int32, (t, POOL_HALO), 1) - POOL_HALO
    tg = sb * t + lax.broadcasted_iota(jnp.int32, (t, 1), 0)
    for gi, w in enumerate(POOL_WINDOWS):
        cs = slice(gi * POOL_GROUP_W, (gi + 1) * POOL_GROUP_W)
        u = u_ref[:, cs]
        band_cur = ((col <= row) & (col > row - w)).astype(BF16)
        band_prev = ((pcol > prow - w) & (pcol + sb * t >= 0)).astype(BF16)
        win = jnp.dot(band_cur, u, preferred_element_type=F32)
        win = win + jnp.dot(band_prev, prev_ref[:, cs], preferred_element_type=F32)
        count = jnp.minimum(tg + 1, w).astype(F32)
        pooled = win / count - u.astype(F32)
        mixed = jnp.dot(pooled.astype(BF16), w_ref[gi], preferred_element_type=F32)
        o_ref[:, cs] = (mixed * s_ref[:, cs]).astype(o_ref.dtype)
    prev_ref[...] = u_ref[t - POOL_HALO:, :]


def pool_mixer(z3, pool_w, scale, t=256):
    b, s, _ = z3.shape
    return pl.pallas_call(
        functools.partial(_pool_kernel, t=t),
        out_shape=jax.ShapeDtypeStruct((b, s, POOL_W), BF16),
        grid=(b, s // t),
        in_specs=[pl.BlockSpec((None, t, POOL_W), lambda bi, si: (bi, si, COL_POOL // POOL_W)),
                  pl.BlockSpec((POOL_GROUPS, POOL_GROUP_W, POOL_GROUP_W), lambda bi, si: (0, 0, 0)),
                  pl.BlockSpec((1, POOL_W), lambda bi, si: (0, 0))],
        out_specs=pl.BlockSpec((None, t, POOL_W), lambda bi, si: (bi, si, 0)),
        scratch_shapes=[pltpu.VMEM((POOL_HALO, POOL_W), BF16)],
        compiler_params=_cparams(("parallel", "arbitrary")),
        name="pool_mixer",
    )(z3, pool_w, scale.reshape(1, POOL_W))


def _head_rmsnorm(x, g):
    ms = jnp.mean(x * x, axis=-1, keepdims=True)
    return x * lax.rsqrt(ms + EPS) * g


def _sb_kernel(q_ref, k_ref, v_ref, o_ref, acc_ref, c_ref, *, t, nh):
    i = pl.program_id(2)
    hd = SB_HEAD_DIM
    row = lax.broadcasted_iota(jnp.int32, (t, t), 0)
    col = lax.broadcasted_iota(jnp.int32, (t, t), 1)
    diag_mask = col < row
    suffix = (row > col).astype(BF16)
    suffix2 = jnp.concatenate([suffix, suffix], axis=0)
    has_prev = jnp.broadcast_to(i > 0, (t, t))

    def block(h, start, mask, c):
        hs = slice(h * hd, (h + 1) * hd)
        k = k_ref[pl.ds(start, t), hs]
        z = lax.dot_general(q_ref[:, hs], k, (((1,), (1,)), ((), ())),
                            preferred_element_type=F32)
        soft = jnp.log(1.0 + jnp.exp(-jnp.abs(z)))
        neg_part = jnp.minimum(z, 0.0)
        log_beta = neg_part - soft
        log_keep = (neg_part - z) - soft
        if mask is not None:
            log_keep = jnp.where(mask, log_keep, 0.0)
        hi, lo = _split_bf16(log_keep)
        later = jnp.dot(jnp.concatenate([hi, lo], axis=1), suffix2, preferred_element_type=F32)
        a = jnp.exp(log_beta + later + c)
        if mask is not None:
            a = jnp.where(mask, a, 0.0)
        out = jnp.dot(a.astype(BF16), v_ref[pl.ds(start, t), hs], preferred_element_type=F32)
        return out, c + later[:, 0:1] + log_keep[:, 0:1]

    diag_start = pl.multiple_of(i * t, t)
    prev_start = pl.multiple_of(jnp.maximum(i - 1, 0) * t, t)
    live = jnp.int32(0)
    for h in range(nh):
        out_a, c = block(h, diag_start, diag_mask, jnp.zeros((t, 1), F32))
        out_b, c = block(h, prev_start, has_prev, c)
        acc_ref[:, h * hd:(h + 1) * hd] = out_a + out_b
        c_ref[h] = c
        live = jnp.maximum(live, (jnp.max(c) > SB_F32_ZERO_LOG).astype(jnp.int32))

    def body(carry):
        j, _ = carry
        start = pl.multiple_of(j * t, t)
        live = jnp.int32(0)
        for h in range(nh):
            out, c = block(h, start, None, c_ref[h])
            acc_ref[:, h * hd:(h + 1) * hd] += out
            c_ref[h] = c
            live = jnp.maximum(live, (jnp.max(c) > SB_F32_ZERO_LOG).astype(jnp.int32))
        return j - 1, live

    def cond(carry):
        j, live = carry
        return jnp.logical_and(j >= 0, live > 0)

    lax.while_loop(cond, body, (i - 2, live))
    o_ref[...] = acc_ref[...].astype(o_ref.dtype)


def sb_attention(z3, t=256, nh=8):
    b, s, _ = z3.shape
    w = nh * SB_HEAD_DIM
    kv_spec = lambda col: pl.BlockSpec((None, s, w), lambda bi, h, i: (bi, 0, col // w + h),
                                       pipeline_mode=pl.Buffered(1))
    return pl.pallas_call(
        functools.partial(_sb_kernel, t=t, nh=nh),
        out_shape=jax.ShapeDtypeStruct((b, s, SB_W), BF16),
        grid=(b, SB_HEADS // nh, s // t),
        in_specs=[pl.BlockSpec((None, t, w), lambda bi, h, i: (bi, i, COL_SQ // w + h)),
                  kv_spec(COL_SK), kv_spec(COL_SV)],
        out_specs=pl.BlockSpec((None, t, w), lambda bi, h, i: (bi, i, h)),
        scratch_shapes=[pltpu.VMEM((t, w), F32), pltpu.VMEM((nh, t, 1), F32)],
        compiler_params=_cparams(("parallel", "parallel", "arbitrary")),
        name="sb_attention",
    )(z3, z3, z3)


def _gla_gate_kernel(h_ref, wg_ref, wlr_hi_ref, wlr_lo_ref, blr_ref, o_ref, *, tm):
    g = jnp.dot(h_ref[...], wg_ref[...], preferred_element_type=F32)
    g_hi, g_lo = _split_bf16(g)
    pre = (jnp.dot(g_hi, wlr_hi_ref[...], preferred_element_type=F32)
           + jnp.dot(g_lo, wlr_hi_ref[...], preferred_element_type=F32)
           + jnp.dot(g_hi, wlr_lo_ref[...], preferred_element_type=F32)) + blr_ref[...]
    log_a = _log_sigmoid(pre) * (1.0 / GLA_TAU)
    row = lax.broadcasted_iota(jnp.int32, (tm, tm), 0)
    col = lax.broadcasted_iota(jnp.int32, (tm, tm), 1)
    shift = GLA_CHUNK.bit_length() - 1
    tri = ((col <= row) & ((col >> shift) == (row >> shift))).astype(BF16)
    hi, lo = _split_bf16(log_a)
    o_ref[...] = (jnp.dot(tri, hi, preferred_element_type=F32)
                  + jnp.dot(tri, lo, preferred_element_type=F32))


def gla_gate(h, w_glr, w_lr, b_lr, tm=256):
    n, d = h.shape
    return pl.pallas_call(
        functools.partial(_gla_gate_kernel, tm=tm),
        out_shape=jax.ShapeDtypeStruct((n, GLA_K_W), F32),
        grid=(n // tm,),
        in_specs=[pl.BlockSpec((tm, d), lambda i: (i, 0)),
                  pl.BlockSpec((d, LANES), lambda i: (0, 0)),
                  pl.BlockSpec((LANES, GLA_K_W), lambda i: (0, 0)),
                  pl.BlockSpec((LANES, GLA_K_W), lambda i: (0, 0)),
                  pl.BlockSpec((1, GLA_K_W), lambda i: (0, 0))],
        out_specs=pl.BlockSpec((tm, GLA_K_W), lambda i: (i, 0)),
        compiler_params=_cparams(("parallel",)),
        name="gla_gate",
    )(h, w_glr, *_split_bf16(w_lr), b_lr.reshape(1, GLA_K_W))


def _gla_kernel(q_ref, k_ref, v_ref, r_ref, b_ref, g_ref, o_ref, st_ref, *, tc):
    @pl.when(pl.program_id(2) == 0)
    def _():
        st_ref[...] = jnp.zeros_like(st_ref)

    c = GLA_CHUNK
    sub = GLA_SUB
    causal = (lax.broadcasted_iota(jnp.int32, (c, c), 1)
              <= lax.broadcasted_iota(jnp.int32, (c, c), 0))
    g_out = g_ref[...]

    def chunk(ci, carry):
        base = pl.multiple_of(ci * c, c)
        rows = pl.ds(base, c)
        q = q_ref[rows, :].astype(F32) * (GLA_DK ** -0.5)
        k = k_ref[rows, :].astype(F32)
        v = v_ref[rows, :]
        b = b_ref[rows, :]
        b_end = b_ref[pl.ds(base + c - 1, 1), :]
        st = st_ref[...]
        inter = lax.dot_general((q * jnp.exp(b)).astype(BF16), st.astype(BF16),
                                (((1,), (1,)), ((), ())), preferred_element_type=F32)
        scores = []
        for si in range(c // sub):
            lo, hi = si * sub, (si + 1) * sub
            if si > 0:
                ref = b_ref[pl.ds(base + lo - 1, 1), :]
            else:
                ref = jnp.zeros((1, GLA_DK), F32)
            q_t = (q[lo:hi] * jnp.exp(b[lo:hi] - ref)).astype(BF16)
            k_t = (k * jnp.exp(jnp.minimum(ref - b, GLA_EXP_CAP))).astype(BF16)
            scores.append(lax.dot_general(q_t, k_t, (((1,), (1,)), ((), ())),
                                          preferred_element_type=F32))
        sc = jnp.where(causal, jnp.concatenate(scores, axis=0), 0.0)
        o = inter + jnp.dot(sc.astype(BF16), v, preferred_element_type=F32)
        k_e = (k * jnp.exp(b_end - b)).astype(BF16)
        v_t = v.astype(F32).T.astype(BF16)
        st_ref[...] = st * jnp.exp(b_end) + jnp.dot(v_t, k_e, preferred_element_type=F32)
        ms = jnp.mean(o * o, axis=-1, keepdims=True)
        o = o * lax.rsqrt(ms + EPS) * g_out
        r = r_ref[rows, :].astype(F32)
        o_ref[rows, :] = (o * (r * _sigmoid(r))).astype(o_ref.dtype)
        return carry

    lax.fori_loop(0, tc // c, chunk, 0, unroll=True)


def gla_mixer(z3, bcum3, g_out, tc=1024):
    b, s, _ = z3.shape
    dk, dv = GLA_DK, GLA_DV
    return pl.pallas_call(
        functools.partial(_gla_kernel, tc=tc),
        out_shape=jax.ShapeDtypeStruct((b, s, GLA_V_W), BF16),
        grid=(b, GLA_HEADS, s // tc),
        in_specs=[pl.BlockSpec((None, tc, dk), lambda bi, h, i: (bi, i, COL_GQ // dk + h)),
                  pl.BlockSpec((None, tc, dk), lambda bi, h, i: (bi, i, COL_GK // dk + h)),
                  pl.BlockSpec((None, tc, dv), lambda bi, h, i: (bi, i, COL_GV // dv + h)),
                  pl.BlockSpec((None, tc, dv), lambda bi, h, i: (bi, i, COL_GR // dv + h)),
                  pl.BlockSpec((None, tc, dk), lambda bi, h, i: (bi, i, h)),
                  pl.BlockSpec((1, dv), lambda bi, h, i: (0, 0))],
        out_specs=pl.BlockSpec((None, tc, dv), lambda bi, h, i: (bi, i, h)),
        scratch_shapes=[pltpu.VMEM((dv, dk), F32)],
        compiler_params=_cparams(("parallel", "parallel", "arbitrary")),
        name="gla_mixer",
    )(z3, z3, z3, z3, bcum3, g_out.reshape(1, dv))


def _merge_kernel(h_ref, wga_ref, wgb_ref, wgc_ref, ba_ref, bb_ref, bc_ref,
                  yp_ref, ys_ref, yg_ref, wp_ref, ws_ref, wg_ref, o_ref):
    h = h_ref[...]
    m = None
    for wgate_ref, b_ref, y_ref, wup_ref in ((wga_ref, ba_ref, yp_ref, wp_ref),
                                             (wgb_ref, bb_ref, ys_ref, ws_ref),
                                             (wgc_ref, bc_ref, yg_ref, wg_ref)):
        gate = _sigmoid(jnp.dot(h, wgate_ref[...], preferred_element_type=F32) + b_ref[...])
        term = gate * jnp.dot(y_ref[...], wup_ref[...], preferred_element_type=F32)
        m = term if m is None else m + term
    o_ref[...] = m.astype(o_ref.dtype)


def merge_branches(h, w_gate, b_gate, y_pool, y_sb, y_gla, w_p, w_s, w_g, tm=1024, tn=512):
    n, d = h.shape
    nj = d // tn
    gate_w_spec = lambda br: pl.BlockSpec((d, tn), lambda i, j: (0, br * nj + j))
    gate_b_spec = lambda br: pl.BlockSpec((1, tn), lambda i, j: (0, br * nj + j))
    y_spec = lambda w: pl.BlockSpec((tm, w), lambda i, j: (i, 0))
    w_spec = lambda w: pl.BlockSpec((w, tn), lambda i, j: (0, j))
    b_gate = b_gate.reshape(1, N_BRANCH * d)
    return pl.pallas_call(
        _merge_kernel,
        out_shape=jax.ShapeDtypeStruct((n, d), BF16),
        grid=(n // tm, nj),
        in_specs=[y_spec(d), gate_w_spec(0), gate_w_spec(1), gate_w_spec(2),
                  gate_b_spec(0), gate_b_spec(1), gate_b_spec(2),
                  y_spec(POOL_W), y_spec(SB_W), y_spec(GLA_V_W),
                  w_spec(POOL_W), w_spec(SB_W), w_spec(GLA_V_W)],
        out_specs=pl.BlockSpec((tm, tn), lambda i, j: (i, j)),
        compiler_params=_cparams(("parallel", "arbitrary")),
        name="merge_branches",
    )(h, w_gate, w_gate, w_gate, b_gate, b_gate, b_gate, y_pool, y_sb, y_gla, w_p, w_s, w_g)


def _silu(a):
    return a * _sigmoid(a)


def _ffn_kernel(h_ref, w1_ref, w3_ref, w2_ref, o_ref):
    @pl.when(pl.program_id(1) == 0)
    def _():
        o_ref[...] = jnp.zeros_like(o_ref)

    h = h_ref[...]
    a = jnp.dot(h, w1_ref[...], preferred_element_type=F32)
    b = jnp.dot(h, w3_ref[...], preferred_element_type=F32)
    g = (_silu(a) * b).astype(BF16)
    o_ref[...] += jnp.dot(g, w2_ref[...], preferred_element_type=F32)


def ffn_dense(h, w1, w3, w2, tm=1024, tf=512):
    n, d = h.shape
    ff = w1.shape[1]
    return pl.pallas_call(
        _ffn_kernel,
        out_shape=jax.ShapeDtypeStruct((n, d), F32),
        grid=(n // tm, ff // tf),
        in_specs=[pl.BlockSpec((tm, d), lambda i, f: (i, 0)),
                  pl.BlockSpec((d, tf), lambda i, f: (0, f)),
                  pl.BlockSpec((d, tf), lambda i, f: (0, f)),
                  pl.BlockSpec((tf, d), lambda i, f: (f, 0))],
        out_specs=pl.BlockSpec((tm, d), lambda i, f: (i, 0)),
        compiler_params=_cparams(("parallel", "arbitrary")),
        name="ffn_dense",
    )(h, w1, w3, w2)


MOE_TM = 1024
MOE_FEW_ROWS = 256
TOP_K = 2


def _row_copy(src_ref, src_idx, dst_ref, dst_idx, sem):
    return pltpu.make_async_copy(src_ref.at[src_idx], dst_ref.at[dst_idx], sem)


def _dispatch_kernel(dest_ref, h_ref, init_ref, o_ref, sem, *, tm):
    del init_ref
    base = pl.program_id(0) * tm

    def copies(t):
        return [_row_copy(h_ref, t, o_ref, dest_ref[TOP_K * (base + t) + kk], sem)
                for kk in range(TOP_K)]

    def start(t, carry):
        for cp in copies(t):
            cp.start()
        return carry

    def wait(t, carry):
        for cp in copies(t):
            cp.wait()
        return carry

    lax.fori_loop(0, tm, start, 0)
    lax.fori_loop(0, tm, wait, 0)


def moe_dispatch(dest, h_rows, n_rows, tm=256):
    n = h_rows.shape[0]
    grid_spec = pltpu.PrefetchScalarGridSpec(
        num_scalar_prefetch=1,
        grid=(n // tm,),
        in_specs=[pl.BlockSpec((tm, ROW_SLABS, LANES), lambda i, dest: (i, 0, 0)),
                  pl.BlockSpec(memory_space=pl.ANY)],
        out_specs=pl.BlockSpec(memory_space=pl.ANY),
        scratch_shapes=[pltpu.SemaphoreType.DMA],
    )
    return pl.pallas_call(
        functools.partial(_dispatch_kernel, tm=tm),
        out_shape=jax.ShapeDtypeStruct((n_rows, ROW_SLABS, LANES), h_rows.dtype),
        grid_spec=grid_spec,
        input_output_aliases={2: 0},
        compiler_params=_cparams(("arbitrary",)),
        name="moe_dispatch",
    )(dest, h_rows, jnp.zeros((n_rows, ROW_SLABS, LANES), h_rows.dtype))


def _moe_ffn_kernel(te_ref, tr_ref, h_ref, w1_ref, w3_ref, w2_ref, o_ref, hb_ref, acc_ref, *, tm):
    del te_ref
    f = pl.program_id(1)
    last = pl.num_programs(1) - 1
    n_real = tr_ref[pl.program_id(0)]
    used = n_real > 0
    few = n_real <= MOE_FEW_ROWS

    @pl.when(used & (f == 0))
    def _():
        acc_ref[...] = jnp.zeros_like(acc_ref)
        for s in range(ROW_SLABS):
            lo, hi = _load_packed_slab(h_ref, s, tm)
            hb_ref[:, s * LANES:(s + 1) * LANES] = lo.astype(BF16)
            hb_ref[:, HALF_D + s * LANES:HALF_D + (s + 1) * LANES] = hi.astype(BF16)

    def swiglu_rows(rows):
        h = hb_ref[rows, :]
        a = jnp.dot(h, w1_ref[...], preferred_element_type=F32)
        b = jnp.dot(h, w3_ref[...], preferred_element_type=F32)
        g = (_silu(a) * b).astype(BF16)
        acc_ref[rows, :] += jnp.dot(g, w2_ref[...], preferred_element_type=F32)

    @pl.when(used & few)
    def _():
        swiglu_rows(slice(0, MOE_FEW_ROWS))

    @pl.when(used & jnp.logical_not(few))
    def _():
        swiglu_rows(slice(None))

    @pl.when(used & (f == last))
    def _():
        _store_packed_rows(o_ref, lambda c0: acc_ref[:, c0:c0 + LANES], tm)

    @pl.when(jnp.logical_not(used) & (f == last))
    def _():
        o_ref[...] = jnp.zeros_like(o_ref)


def moe_grouped_ffn(tile_expert, tile_rows, h_sorted, w1, w3, w2, tm=MOE_TM, tf=512):
    r = h_sorted.shape[0] // ROW_SLABS
    d, ff = w1.shape[1], w1.shape[2]
    nf = ff // tf

    def f_idx(i, f, tr):
        return jnp.where(tr[i] > 0, f, nf - 1)

    rows = pl.BlockSpec((tm * ROW_SLABS, LANES), lambda i, f, te, tr: (i, 0))
    grid_spec = pltpu.PrefetchScalarGridSpec(
        num_scalar_prefetch=2,
        grid=(r // tm, nf),
        in_specs=[rows,
                  pl.BlockSpec((None, d, tf), lambda i, f, te, tr: (te[i], 0, f_idx(i, f, tr))),
                  pl.BlockSpec((None, d, tf), lambda i, f, te, tr: (te[i], 0, f_idx(i, f, tr))),
                  pl.BlockSpec((None, tf, d), lambda i, f, te, tr: (te[i], f_idx(i, f, tr), 0))],
        out_specs=rows,
        scratch_shapes=[pltpu.VMEM((tm, d), BF16), pltpu.VMEM((tm, d), F32)],
    )
    return pl.pallas_call(
        functools.partial(_moe_ffn_kernel, tm=tm),
        out_shape=jax.ShapeDtypeStruct(h_sorted.shape, jnp.int32),
        grid_spec=grid_spec,
        compiler_params=_cparams(("arbitrary", "arbitrary")),
        name="moe_grouped_ffn",
    )(tile_expert, tile_rows, h_sorted, w1, w3, w2)


def _combine_kernel(dest_ref, w_ref, y_ref, o_ref, buf_ref, sems, *, tm):
    base = pl.program_id(0) * tm

    def copies(t):
        return [_row_copy(y_ref, dest_ref[TOP_K * (base + t) + kk], buf_ref.at[kk], t, sems.at[kk])
                for kk in range(TOP_K)]

    def start(t, carry):
        for cp in copies(t):
            cp.start()
        return carry

    def wait(t, carry):
        for cp in copies(t):
            cp.wait()
        return carry

    lax.fori_loop(0, tm, start, 0)
    lax.fori_loop(0, tm, wait, 0)
    lo = hi = None
    for kk in range(TOP_K):
        words = buf_ref[kk]
        w = w_ref[kk]
        lo_k = lax.bitcast_convert_type(words << 16, F32) * w
        hi_k = lax.bitcast_convert_type(words & _HIGH_HALF, F32) * w
        lo = lo_k if lo is None else lo + lo_k
        hi = hi_k if hi is None else hi + hi_k
    o_ref[0] = lo
    o_ref[1] = hi


def moe_combine(dest, w_rows, y_rows, n, tm=256):
    grid_spec = pltpu.PrefetchScalarGridSpec(
        num_scalar_prefetch=1,
        grid=(n // tm,),
        in_specs=[pl.BlockSpec((TOP_K, tm, 1, LANES), lambda i, dest: (0, i, 0, 0)),
                  pl.BlockSpec(memory_space=pl.ANY)],
        out_specs=pl.BlockSpec((2, tm, ROW_SLABS, LANES), lambda i, dest: (0, i, 0, 0)),
        scratch_shapes=[pltpu.VMEM((TOP_K, tm, ROW_SLABS, LANES), jnp.int32),
                        pltpu.SemaphoreType.DMA((TOP_K,))],
    )
    return pl.pallas_call(
        functools.partial(_combine_kernel, tm=tm),
        out_shape=jax.ShapeDtypeStruct((2, n, ROW_SLABS, LANES), F32),
        grid_spec=grid_spec,
        compiler_params=_cparams(("arbitrary",)),
        name="moe_combine",
    )(dest, w_rows, y_rows)


def moe_sparse(h_rows, info, counts, w1, w3, w2):
    n = h_rows.shape[0] // ROW_SLABS
    tm = MOE_TM
    n_tiles = (TOP_K * n) // tm + N_EXPERTS
    n_rows = n_tiles * tm
    cnt = counts[0, :N_EXPERTS].astype(jnp.int32)
    tiles_per = (cnt + tm - 1) // tm
    tile_end = jnp.cumsum(tiles_per)
    group_start = (tile_end - tiles_per) * tm
    n_used = tile_end[-1]
    tile_ids = jnp.arange(n_tiles, dtype=jnp.int32)
    tile_expert = jnp.minimum(jnp.searchsorted(tile_end, tile_ids, side="right"),
                              N_EXPERTS - 1).astype(jnp.int32)
    tile_rows = jnp.clip(cnt[tile_expert] - (tile_ids * tm - group_start[tile_expert]), 0, tm)
    tile_rows = jnp.where(tile_ids < n_used, tile_rows, 0).astype(jnp.int32)
    tile_expert = jnp.where(tile_ids < n_used, tile_expert,
                            tile_expert[jnp.maximum(n_used - 1, 0)])
    experts = info[:, INFO_E1:INFO_E2 + 1].astype(jnp.int32)
    ranks = info[:, INFO_R1:INFO_R2 + 1].astype(jnp.int32)
    dest = (group_start[experts] + ranks).reshape(TOP_K * n)
    w_rows = jnp.broadcast_to(info[:, INFO_W1:INFO_W2 + 1].T[:, :, None, None],
                              (TOP_K, n, 1, LANES))

    h_sorted = moe_dispatch(dest, h_rows.reshape(n, ROW_SLABS, LANES), n_rows)
    y_sorted = moe_grouped_ffn(tile_expert, tile_rows,
                               h_sorted.reshape(n_rows * ROW_SLABS, LANES), w1, w3, w2)
    y = moe_combine(dest, w_rows, y_sorted.reshape(n_rows, ROW_SLABS, LANES), n)
    return y.reshape(2, n * ROW_SLABS, LANES)


def _ple_kernel(x_ref, y_ref, p_ref, wg_ref, wp_ref, *rest, y_slabs, with_norm):
    if with_norm:
        g_ref, o_ref, h_ref = rest
    else:
        (o_ref,) = rest
    if y_slabs:
        t = x_ref.shape[0]
        parts = []
        for half in range(2):
            for s in range(ROW_SLABS):
                parts.append(y_ref[half, pl.ds(s, t, stride=ROW_SLABS), :])
        x = x_ref[...] + jnp.concatenate(parts, axis=1)
    else:
        x = x_ref[...] + y_ref[...]
    xb = x.astype(BF16)
    pb = p_ref[...].astype(BF16)
    d = x.shape[1]
    sum_sq = jnp.zeros((x.shape[0], 1), F32)
    for c0 in range(0, d, PLE_COL_CHUNK):
        cs = slice(c0, c0 + PLE_COL_CHUNK)
        gate = _sigmoid(jnp.dot(xb, wg_ref[:, cs], preferred_element_type=F32))
        proj = jnp.dot(pb, wp_ref[:, cs], preferred_element_type=F32)
        xc = x[:, cs] + gate * proj
        o_ref[:, cs] = xc
        sum_sq += jnp.sum(xc * xc, axis=-1, keepdims=True)
    if with_norm:
        inv = lax.rsqrt(sum_sq * (1.0 / d) + EPS)
        h_ref[...] = (o_ref[...] * inv * g_ref[...]).astype(h_ref.dtype)


PLE_COL_CHUNK = 512


def ple_update(x, y, p, w_gate, w_proj, norm_gain=None, tm=512):
    n, d = x.shape
    pd = p.shape[1]
    y_slabs = y.ndim == 3
    with_norm = norm_gain is not None
    row = lambda width: pl.BlockSpec((tm, width), lambda i: (i, 0))
    resident = lambda shape: pl.BlockSpec(shape, lambda i: (0, 0), pipeline_mode=pl.Buffered(1))
    y_spec = (pl.BlockSpec((2, tm * ROW_SLABS, LANES), lambda i: (0, i, 0)) if y_slabs else row(d))
    in_specs = [row(d), y_spec, row(pd), resident((d, d)), resident((pd, d))]
    args = [x, y, p, w_gate, w_proj]
    out_shape = [jax.ShapeDtypeStruct((n, d), F32)]
    out_specs = [row(d)]
    if with_norm:
        in_specs.append(pl.BlockSpec((1, d), lambda i: (0, 0)))
        args.append(norm_gain.reshape(1, d))
        out_shape.append(jax.ShapeDtypeStruct((n, d), BF16))
        out_specs.append(row(d))
    res = pl.pallas_call(
        functools.partial(_ple_kernel, y_slabs=y_slabs, with_norm=with_norm),
        out_shape=tuple(out_shape),
        grid=(n // tm,),
        in_specs=in_specs,
        out_specs=tuple(out_specs),
        compiler_params=_cparams(("parallel",)),
        name="ple_update",
    )(*args)
    return res if with_norm else res[0]


def _split_w_in(w_in_i):
    c_lr = COL_GR
    main = jnp.concatenate([w_in_i[:, :c_lr], w_in_i[:, c_lr + GLA_RANK:]], axis=1).astype(BF16)
    glr = jnp.zeros((D_MODEL, LANES), BF16).at[:, :GLA_RANK].set(
        w_in_i[:, c_lr:c_lr + GLA_RANK].astype(BF16))
    return main, glr


def kernel(x, p, g_mix, w_in, w_branch_gate, b_branch_gate, pool_w, pool_scale, sb_gq, sb_gk,
           gla_w_lr, gla_b_lr, gla_g_out, w_up_pool, w_up_sb, w_up_gla, w_o, g_ffn,
           ffn_w1, ffn_w3, ffn_w2, moe_router, moe_w1, moe_w3, moe_w2, ple_w_proj, ple_w_gate):
    bsz, seq, d = x.shape
    n = bsz * seq
    depth = w_in.shape[0]
    xf = x.reshape(n, d)
    h = rmsnorm(xf, g_mix[0])
    for i in range(depth):
        w_main, w_glr = _split_w_in(w_in[i])
        w_lr_pad = jnp.zeros((LANES, GLA_K_W), F32).at[:GLA_RANK].set(gla_w_lr[i])
        qk_gain = jnp.ones((1, Z_COLS), F32)
        qk_gain = qk_gain.at[0, COL_SQ:COL_SK].set(jnp.tile(sb_gq[i] * SB_HEAD_DIM ** -0.5, SB_HEADS))
        qk_gain = qk_gain.at[0, COL_SK:COL_SV].set(jnp.tile(sb_gk[i], SB_HEADS))

        z = in_proj(h, w_main, qk_gain)
        bcum = gla_gate(h, w_glr, w_lr_pad, gla_b_lr[i])

        z3 = z.reshape(bsz, seq, Z_COLS)
        y_pool = pool_mixer(z3, pool_w[i].astype(BF16), pool_scale[i])
        y_sb = sb_attention(z3)
        y_gla = gla_mixer(z3, bcum.reshape(bsz, seq, GLA_K_W), gla_g_out[i])

        merged = merge_branches(h, w_branch_gate[i].astype(BF16), b_branch_gate[i],
                                y_pool.reshape(n, POOL_W), y_sb.reshape(n, SB_W),
                                y_gla.reshape(n, GLA_V_W), w_up_pool[i].astype(BF16),
                                w_up_sb[i].astype(BF16), w_up_gla[i].astype(BF16))

        j = i // 2
        if i % 2 == 0:
            xf, h2 = out_proj(merged, w_o[i].astype(BF16), xf, g_ffn[i])
            y = ffn_dense(h2, ffn_w1[j].astype(BF16), ffn_w3[j].astype(BF16),
                          ffn_w2[j].astype(BF16))
        else:
            xf, h2_rows = out_proj(merged, w_o[i].astype(BF16), xf, g_ffn[i], pack_rows=True)
            info, counts = route_tokens(xf, g_ffn[i], moe_router[j])
            y = moe_sparse(h2_rows, info, counts, moe_w1[j].astype(BF16),
                           moe_w3[j].astype(BF16), moe_w2[j].astype(BF16))

        if i + 1 < depth:
            xf, h = ple_update(xf, y, p[i].reshape(n, PLE_DIM), ple_w_gate[i].astype(BF16),
                               ple_w_proj[i].astype(BF16), norm_gain=g_mix[i + 1])
        else:
            xf = ple_update(xf, y, p[i].reshape(n, PLE_DIM), ple_w_gate[i].astype(BF16),
                            ple_w_proj[i].astype(BF16))
    return xf.reshape(bsz, seq, d)
```

```python
import functools

import jax
import jax.numpy as jnp
from jax import lax
from jax.experimental import pallas as pl
from jax.experimental.pallas import tpu as pltpu

F32 = jnp.float32
BF16 = jnp.bfloat16

EPS = 1e-6
D_MODEL = 2048
PLE_DIM = 256
POOL_GROUPS = 4
POOL_GROUP_W = 256
POOL_W = POOL_GROUPS * POOL_GROUP_W
POOL_WINDOWS = (2, 4, 8, 16)
SB_HEADS = 8
SB_HEAD_DIM = 128
SB_W = SB_HEADS * SB_HEAD_DIM
GLA_HEADS = 4
GLA_DK = 128
GLA_DV = 256
GLA_K_W = GLA_HEADS * GLA_DK
GLA_V_W = GLA_HEADS * GLA_DV
GLA_RANK = 16
GLA_TAU = 16.0
D_FF = 5632
N_EXPERTS = 8
N_BRANCH = 3

LANES = 128
VMEM_LIMIT = 56 * 1024 * 1024

COL_POOL = 0
COL_SQ = COL_POOL + POOL_W
COL_SK = COL_SQ + SB_W
COL_SV = COL_SK + SB_W
COL_GQ = COL_SV + SB_W
COL_GK = COL_GQ + GLA_K_W
COL_GV = COL_GK + GLA_K_W
COL_GR = COL_GV + GLA_V_W
Z_COLS = COL_GR + GLA_V_W

GLA_CHUNK = 64
GLA_SUB = 16
GLA_EXP_CAP = 80.0
SB_F32_ZERO_LOG = -104.0


def _cparams(sem):
    return pltpu.CompilerParams(dimension_semantics=sem, vmem_limit_bytes=VMEM_LIMIT)


def _log_sigmoid(z):
    return jnp.minimum(z, 0.0) - jnp.log(1.0 + jnp.exp(-jnp.abs(z)))


def _sigmoid(z):
    return 1.0 / (1.0 + jnp.exp(-z))


def _split_bf16(x):
    hi = x.astype(BF16)
    lo = (x - hi.astype(F32)).astype(BF16)
    return hi, lo


def _rmsnorm_kernel(x_ref, g_ref, o_ref):
    x = x_ref[...]
    ms = jnp.mean(x * x, axis=-1, keepdims=True)
    o_ref[...] = (x * lax.rsqrt(ms + EPS) * g_ref[...]).astype(o_ref.dtype)


def rmsnorm(x, g, tm=512):
    n, d = x.shape
    return pl.pallas_call(
        _rmsnorm_kernel,
        out_shape=jax.ShapeDtypeStruct((n, d), BF16),
        grid=(n // tm,),
        in_specs=[pl.BlockSpec((tm, d), lambda i: (i, 0)),
                  pl.BlockSpec((1, d), lambda i: (0, 0))],
        out_specs=pl.BlockSpec((tm, d), lambda i: (i, 0)),
        compiler_params=_cparams(("parallel",)),
        name="rmsnorm",
    )(x, g.reshape(1, d))


def _route_top2(h, rhi_ref, rlo_ref, info_ref, cnt_ref, carry_ref):
    tm = h.shape[0]

    @pl.when(pl.program_id(0) == 0)
    def _():
        carry_ref[...] = jnp.zeros_like(carry_ref)

    h_hi, h_lo = _split_bf16(h)
    logits = (jnp.dot(h_hi, rhi_ref[...], preferred_element_type=F32)
              + jnp.dot(h_lo, rhi_ref[...], preferred_element_type=F32)
              + jnp.dot(h_hi, rlo_ref[...], preferred_element_type=F32))
    lane = lax.broadcasted_iota(jnp.int32, logits.shape, 1).astype(F32)
    neg = jnp.float32(-jnp.inf)
    logits = jnp.where(lane < N_EXPERTS, logits, neg)
    m1 = jnp.max(logits, axis=-1, keepdims=True)
    i1 = jnp.min(jnp.where(logits == m1, lane, float(LANES)), axis=-1, keepdims=True)
    sel1 = lane == i1
    rest = jnp.where(sel1, neg, logits)
    m2 = jnp.max(rest, axis=-1, keepdims=True)
    i2 = jnp.min(jnp.where(rest == m2, lane, float(LANES)), axis=-1, keepdims=True)
    sel2 = lane == i2
    e2 = jnp.exp(m2 - m1)
    den = 1.0 + e2
    sel = (sel1 | sel2).astype(BF16)
    before = (lax.broadcasted_iota(jnp.int32, (tm, tm), 1)
              < lax.broadcasted_iota(jnp.int32, (tm, tm), 0)).astype(BF16)
    carry = carry_ref[...]
    rank = jnp.dot(before, sel, preferred_element_type=F32) + carry[0:1, :]
    r1 = jnp.sum(jnp.where(sel1, rank, 0.0), axis=-1, keepdims=True)
    r2 = jnp.sum(jnp.where(sel2, rank, 0.0), axis=-1, keepdims=True)
    fields = (i1, i2, 1.0 / den, e2 / den, r1, r2)
    info = jnp.zeros_like(logits)
    for li, val in enumerate(fields):
        info = jnp.where(lane == li, val, info)
    info_ref[...] = info
    carry = carry + jnp.sum(sel.astype(F32), axis=0, keepdims=True)
    carry_ref[...] = carry
    cnt_ref[...] = carry


INFO_E1, INFO_E2, INFO_W1, INFO_W2, INFO_R1, INFO_R2 = range(6)


def _in_proj_kernel(a_ref, w_ref, g_ref, o_ref, *, tn):
    j = pl.program_id(1)
    acc = jnp.dot(a_ref[...], w_ref[...], preferred_element_type=F32)
    is_qk = (j >= COL_SQ // tn) & (j < COL_SV // tn)
    for hh in range(tn // SB_HEAD_DIM):
        cs = slice(hh * SB_HEAD_DIM, (hh + 1) * SB_HEAD_DIM)
        blk = acc[:, cs]
        o_ref[:, cs] = jnp.where(is_qk, _head_rmsnorm(blk, g_ref[:, cs]), blk).astype(o_ref.dtype)


def in_proj(h, w, qk_gain, tm=1024, tn=1024):
    n, k = h.shape
    m = w.shape[1]
    assert COL_SQ % tn == 0 and COL_SV % tn == 0 and tn % SB_HEAD_DIM == 0
    return pl.pallas_call(
        functools.partial(_in_proj_kernel, tn=tn),
        out_shape=jax.ShapeDtypeStruct((n, m), BF16),
        grid=(n // tm, m // tn),
        in_specs=[pl.BlockSpec((tm, k), lambda i, j: (i, 0)),
                  pl.BlockSpec((k, tn), lambda i, j: (0, j)),
                  pl.BlockSpec((1, tn), lambda i, j: (0, j))],
        out_specs=pl.BlockSpec((tm, tn), lambda i, j: (i, j)),
        compiler_params=_cparams(("parallel", "arbitrary")),
        name="in_proj",
    )(h, w, qk_gain)


ROW_SLABS = D_MODEL // (2 * LANES)
HALF_D = D_MODEL // 2
_HIGH_HALF = -65536


def _bf16_bits(x):
    return lax.bitcast_convert_type(x.astype(BF16).astype(F32), jnp.int32)


def _store_packed_rows(ref, read_cols, t):
    for s in range(ROW_SLABS):
        lo = (_bf16_bits(read_cols(s * LANES)) >> 16) & 0xFFFF
        hi = _bf16_bits(read_cols(HALF_D + s * LANES)) & _HIGH_HALF
        ref[pl.ds(s, t, stride=ROW_SLABS), :] = lo | hi


def _load_packed_slab(ref, s, t):
    w = ref[pl.ds(s, t, stride=ROW_SLABS), :]
    lo = lax.bitcast_convert_type(w << 16, F32)
    hi = lax.bitcast_convert_type(w & _HIGH_HALF, F32)
    return lo, hi


def _row_proj_kernel(a_ref, w_ref, r_ref, g_ref, o_ref, h_ref, *, pack_rows):
    x = r_ref[...] + jnp.dot(a_ref[...], w_ref[...], preferred_element_type=F32)
    ms = jnp.mean(x * x, axis=-1, keepdims=True)
    o_ref[...] = x
    h = x * lax.rsqrt(ms + EPS) * g_ref[...]
    if pack_rows:
        _store_packed_rows(h_ref, lambda c0: h[:, c0:c0 + LANES], x.shape[0])
    else:
        h_ref[...] = h.astype(h_ref.dtype)


def out_proj(a, w, resid, norm_gain, pack_rows=False, tm=512):
    n, k = a.shape
    d = w.shape[1]
    row = lambda width: pl.BlockSpec((tm, width), lambda i: (i, 0))
    if pack_rows:
        h_shape = jax.ShapeDtypeStruct((n * ROW_SLABS, LANES), jnp.int32)
        h_spec = pl.BlockSpec((tm * ROW_SLABS, LANES), lambda i: (i, 0))
    else:
        h_shape, h_spec = jax.ShapeDtypeStruct((n, d), BF16), row(d)
    return pl.pallas_call(
        functools.partial(_row_proj_kernel, pack_rows=pack_rows),
        out_shape=(jax.ShapeDtypeStruct((n, d), F32), h_shape),
        grid=(n // tm,),
        in_specs=[row(k), pl.BlockSpec((k, d), lambda i: (0, 0), pipeline_mode=pl.Buffered(1)),
                  row(d), pl.BlockSpec((1, d), lambda i: (0, 0))],
        out_specs=(row(d), h_spec),
        compiler_params=_cparams(("parallel",)),
        name="out_proj",
    )(a, w, resid, norm_gain.reshape(1, d))


def _router_kernel(x_ref, g_ref, rhi_ref, rlo_ref, info_ref, cnt_ref, carry_ref):
    x = x_ref[...]
    ms = jnp.mean(x * x, axis=-1, keepdims=True)
    h = x * lax.rsqrt(ms + EPS) * g_ref[...]
    _route_top2(h, rhi_ref, rlo_ref, info_ref, cnt_ref, carry_ref)


def route_tokens(x, g, router, tm=512):
    n, d = x.shape
    r_pad = jnp.zeros((d, LANES), F32).at[:, :N_EXPERTS].set(router)
    fixed = lambda shape: pl.BlockSpec(shape, lambda i: (0, 0))
    return pl.pallas_call(
        _router_kernel,
        out_shape=(jax.ShapeDtypeStruct((n, LANES), F32), jax.ShapeDtypeStruct((8, LANES), F32)),
        grid=(n // tm,),
        in_specs=[pl.BlockSpec((tm, d), lambda i: (i, 0)), fixed((1, d)),
                  fixed((d, LANES)), fixed((d, LANES))],
        out_specs=(pl.BlockSpec((tm, LANES), lambda i: (i, 0)), fixed((8, LANES))),
        scratch_shapes=[pltpu.VMEM((8, LANES), F32)],
        compiler_params=_cparams(("arbitrary",)),
        name="route_tokens",
    )(x, g.reshape(1, d), *_split_bf16(r_pad))


POOL_HALO = 128


def _pool_kernel(u_ref, w_ref, s_ref, o_ref, prev_ref, *, t):
    sb = pl.program_id(1)

    @pl.when(sb == 0)
    def _():
        prev_ref[...] = jnp.zeros_like(prev_ref)

    row = lax.broadcasted_iota(jnp.int32, (t, t), 0)
    col = lax.broadcasted_iota(jnp.int32, (t, t), 1)
    prow = lax.broadcasted_iota(jnp.int32, (t, POOL_HALO), 0)
    pcol = lax.broadcasted_iota(jnp.int32, (t, POOL_HALO), 1) - POOL_HALO
    tg = sb * t + lax.broadcasted_iota(jnp.int32, (t, 1), 0)
    for gi, w in enumerate(POOL_WINDOWS):
        cs = slice(gi * POOL_GROUP_W, (gi + 1) * POOL_GROUP_W)
        u = u_ref[:, cs]
        band_cur = ((col <= row) & (col > row - w)).astype(BF16)
        band_prev = ((pcol > prow - w) & (pcol + sb * t >= 0)).astype(BF16)
        win = jnp.dot(band_cur, u, preferred_element_type=F32)
        win = win + jnp.dot(band_prev, prev_ref[:, cs], preferred_element_type=F32)
        count = jnp.minimum(tg + 1, w).astype(F32)
        pooled = win / count - u.astype(F32)
        mixed = jnp.dot(pooled.astype(BF16), w_ref[gi], preferred_element_type=F32)
        o_ref[:, cs] = (mixed * s_ref[:, cs]).astype(o_ref.dtype)
    prev_ref[...] = u_ref[t - POOL_HALO:, :]


def pool_mixer(z3, pool_w, scale, t=256):
    b, s, _ = z3.shape
    return pl.pallas_call(
        functools.partial(_pool_kernel, t=t),
        out_shape=jax.ShapeDtypeStruct((b, s, POOL_W), BF16),
        grid=(b, s // t),
        in_specs=[pl.BlockSpec((None, t, POOL_W), lambda bi, si: (bi, si, COL_POOL // POOL_W)),
                  pl.BlockSpec((POOL_GROUPS, POOL_GROUP_W, POOL_GROUP_W), lambda bi, si: (0, 0, 0)),
                  pl.BlockSpec((1, POOL_W), lambda bi, si: (0, 0))],
        out_specs=pl.BlockSpec((None, t, POOL_W), lambda bi, si: (bi, si, 0)),
        scratch_shapes=[pltpu.VMEM((POOL_HALO, POOL_W), BF16)],
        compiler_params=_cparams(("parallel", "arbitrary")),
        name="pool_mixer",
    )(z3, pool_w, scale.reshape(1, POOL_W))


def _head_rmsnorm(x, g):
    ms = jnp.mean(x * x, axis=-1, keepdims=True)
    return x * lax.rsqrt(ms + EPS) * g


def _sb_kernel(q_ref, k_ref, v_ref, o_ref, acc_ref, c_ref, *, t, nh):
    i = pl.program_id(2)
    hd = SB_HEAD_DIM
    row = lax.broadcasted_iota(jnp.int32, (t, t), 0)
    col = lax.broadcasted_iota(jnp.int32, (t, t), 1)
    diag_mask = col < row
    suffix = (row > col).astype(BF16)
    suffix2 = jnp.concatenate([suffix, suffix], axis=0)
    has_prev = jnp.broadcast_to(i > 0, (t, t))

    def block(h, start, mask, c):
        hs = slice(h * hd, (h + 1) * hd)
        k = k_ref[pl.ds(start, t), hs]
        z = lax.dot_general(q_ref[:, hs], k, (((1,), (1,)), ((), ())),
                            preferred_element_type=F32)
        soft = jnp.log(1.0 + jnp.exp(-jnp.abs(z)))
        neg_part = jnp.minimum(z, 0.0)
        log_beta = neg_part - soft
        log_keep = (neg_part - z) - soft
        if mask is not None:
            log_keep = jnp.where(mask, log_keep, 0.0)
        hi, lo = _split_bf16(log_keep)
        later = jnp.dot(jnp.concatenate([hi, lo], axis=1), suffix2, preferred_element_type=F32)
        a = jnp.exp(log_beta + later + c)
        if mask is not None:
            a = jnp.where(mask, a, 0.0)
        out = jnp.dot(a.astype(BF16), v_ref[pl.ds(start, t), hs], preferred_element_type=F32)
        return out, c + later[:, 0:1] + log_keep[:, 0:1]

    diag_start = pl.multiple_of(i * t, t)
    prev_start = pl.multiple_of(jnp.maximum(i - 1, 0) * t, t)
    live = jnp.int32(0)
    for h in range(nh):
        out_a, c = block(h, diag_start, diag_mask, jnp.zeros((t, 1), F32))
        out_b, c = block(h, prev_start, has_prev, c)
        acc_ref[:, h * hd:(h + 1) * hd] = out_a + out_b
        c_ref[h] = c
        live = jnp.maximum(live, (jnp.max(c) > SB_F32_ZERO_LOG).astype(jnp.int32))

    def body(carry):
        j, _ = carry
        start = pl.multiple_of(j * t, t)
        live = jnp.int32(0)
        for h in range(nh):
            out, c = block(h, start, None, c_ref[h])
            acc_ref[:, h * hd:(h + 1) * hd] += out
            c_ref[h] = c
            live = jnp.maximum(live, (jnp.max(c) > SB_F32_ZERO_LOG).astype(jnp.int32))
        return j - 1, live

    def cond(carry):
        j, live = carry
        return jnp.logical_and(j >= 0, live > 0)

    lax.while_loop(cond, body, (i - 2, live))
    o_ref[...] = acc_ref[...].astype(o_ref.dtype)


def sb_attention(z3, t=256, nh=8):
    b, s, _ = z3.shape
    w = nh * SB_HEAD_DIM
    kv_spec = lambda col: pl.BlockSpec((None, s, w), lambda bi, h, i: (bi, 0, col // w + h),
                                       pipeline_mode=pl.Buffered(1))
    return pl.pallas_call(
        functools.partial(_sb_kernel, t=t, nh=nh),
        out_shape=jax.ShapeDtypeStruct((b, s, SB_W), BF16),
        grid=(b, SB_HEADS // nh, s // t),
        in_specs=[pl.BlockSpec((None, t, w), lambda bi, h, i: (bi, i, COL_SQ // w + h)),
                  kv_spec(COL_SK), kv_spec(COL_SV)],
        out_specs=pl.BlockSpec((None, t, w), lambda bi, h, i: (bi, i, h)),
        scratch_shapes=[pltpu.VMEM((t, w), F32), pltpu.VMEM((nh, t, 1), F32)],
        compiler_params=_cparams(("parallel", "parallel", "arbitrary")),
        name="sb_attention",
    )(z3, z3, z3)


def _gla_gate_kernel(h_ref, wg_ref, wlr_hi_ref, wlr_lo_ref, blr_ref, o_ref, *, tm):
    g = jnp.dot(h_ref[...], wg_ref[...], preferred_element_type=F32)
    g_hi, g_lo = _split_bf16(g)
    pre = (jnp.dot(g_hi, wlr_hi_ref[...], preferred_element_type=F32)
           + jnp.dot(g_lo, wlr_hi_ref[...], preferred_element_type=F32)
           + jnp.dot(g_hi, wlr_lo_ref[...], preferred_element_type=F32)) + blr_ref[...]
    log_a = _log_sigmoid(pre) * (1.0 / GLA_TAU)
    row = lax.broadcasted_iota(jnp.int32, (tm, tm), 0)
    col = lax.broadcasted_iota(jnp.int32, (tm, tm), 1)
    shift = GLA_CHUNK.bit_length() - 1
    tri = ((col <= row) & ((col >> shift) == (row >> shift))).astype(BF16)
    hi, lo = _split_bf16(log_a)
    o_ref[...] = (jnp.dot(tri, hi, preferred_element_type=F32)
                  + jnp.dot(tri, lo, preferred_element_type=F32))


def gla_gate(h, w_glr, w_lr, b_lr, tm=256):
    n, d = h.shape
    return pl.pallas_call(
        functools.partial(_gla_gate_kernel, tm=tm),
        out_shape=jax.ShapeDtypeStruct((n, GLA_K_W), F32),
        grid=(n // tm,),
        in_specs=[pl.BlockSpec((tm, d), lambda i: (i, 0)),
                  pl.BlockSpec((d, LANES), lambda i: (0, 0)),
                  pl.BlockSpec((LANES, GLA_K_W), lambda i: (0, 0)),
                  pl.BlockSpec((LANES, GLA_K_W), lambda i: (0, 0)),
                  pl.BlockSpec((1, GLA_K_W), lambda i: (0, 0))],
        out_specs=pl.BlockSpec((tm, GLA_K_W), lambda i: (i, 0)),
        compiler_params=_cparams(("parallel",)),
        name="gla_gate",
    )(h, w_glr, *_split_bf16(w_lr), b_lr.reshape(1, GLA_K_W))


def _gla_kernel(q_ref, k_ref, v_ref, r_ref, b_ref, g_ref, o_ref, st_ref, *, tc):
    @pl.when(pl.program_id(2) == 0)
    def _():
        st_ref[...] = jnp.zeros_like(st_ref)

    c = GLA_CHUNK
    sub = GLA_SUB
    causal = (lax.broadcasted_iota(jnp.int32, (c, c), 1)
              <= lax.broadcasted_iota(jnp.int32, (c, c), 0))
    g_out = g_ref[...]

    def chunk(ci, carry):
        base = pl.multiple_of(ci * c, c)
        rows = pl.ds(base, c)
        q = q_ref[rows, :].astype(F32) * (GLA_DK ** -0.5)
        k = k_ref[rows, :].astype(F32)
        v = v_ref[rows, :]
        b = b_ref[rows, :]
        b_end = b_ref[pl.ds(base + c - 1, 1), :]
        st = st_ref[...]
        inter = lax.dot_general((q * jnp.exp(b)).astype(BF16), st.astype(BF16),
                                (((1,), (1,)), ((), ())), preferred_element_type=F32)
        scores = []
        for si in range(c // sub):
            lo, hi = si * sub, (si + 1) * sub
            if si > 0:
                ref = b_ref[pl.ds(base + lo - 1, 1), :]
            else:
                ref = jnp.zeros((1, GLA_DK), F32)
            q_t = (q[lo:hi] * jnp.exp(b[lo:hi] - ref)).astype(BF16)
            k_t = (k * jnp.exp(jnp.minimum(ref - b, GLA_EXP_CAP))).astype(BF16)
            scores.append(lax.dot_general(q_t, k_t, (((1,), (1,)), ((), ())),
                                          preferred_element_type=F32))
        sc = jnp.where(causal, jnp.concatenate(scores, axis=0), 0.0)
        o = inter + jnp.dot(sc.astype(BF16), v, preferred_element_type=F32)
        k_e = (k * jnp.exp(b_end - b)).astype(BF16)
        v_t = v.astype(F32).T.astype(BF16)
        st_ref[...] = st * jnp.exp(b_end) + jnp.dot(v_t, k_e, preferred_element_type=F32)
        ms = jnp.mean(o * o, axis=-1, keepdims=True)
        o = o * lax.rsqrt(ms + EPS) * g_out
        r = r_ref[rows, :].astype(F32)
        o_ref[rows, :] = (o * (r * _sigmoid(r))).astype(o_ref.dtype)
        return carry

    lax.fori_loop(0, tc // c, chunk, 0, unroll=True)


def gla_mixer(z3, bcum3, g_out, tc=1024):
    b, s, _ = z3.shape
    dk, dv = GLA_DK, GLA_DV
    return pl.pallas_call(
        functools.partial(_gla_kernel, tc=tc),
        out_shape=jax.ShapeDtypeStruct((b, s, GLA_V_W), BF16),
        grid=(b, GLA_HEADS, s // tc),
        in_specs=[pl.BlockSpec((None, tc, dk), lambda bi, h, i: (bi, i, COL_GQ // dk + h)),
                  pl.BlockSpec((None, tc, dk), lambda bi, h, i: (bi, i, COL_GK // dk + h)),
                  pl.BlockSpec((None, tc, dv), lambda bi, h, i: (bi, i, COL_GV // dv + h)),
                  pl.BlockSpec((None, tc, dv), lambda bi, h, i: (bi, i, COL_GR // dv + h)),
                  pl.BlockSpec((None, tc, dk), lambda bi, h, i: (bi, i, h)),
                  pl.BlockSpec((1, dv), lambda bi, h, i: (0, 0))],
        out_specs=pl.BlockSpec((None, tc, dv), lambda bi, h, i: (bi, i, h)),
        scratch_shapes=[pltpu.VMEM((dv, dk), F32)],
        compiler_params=_cparams(("parallel", "parallel", "arbitrary")),
        name="gla_mixer",
    )(z3, z3, z3, z3, bcum3, g_out.reshape(1, dv))


def _merge_kernel(h_ref, wga_ref, wgb_ref, wgc_ref, ba_ref, bb_ref, bc_ref,
                  yp_ref, ys_ref, yg_ref, wp_ref, ws_ref, wg_ref, o_ref):
    h = h_ref[...]
    m = None
    for wgate_ref, b_ref, y_ref, wup_ref in ((wga_ref, ba_ref, yp_ref, wp_ref),
                                             (wgb_ref, bb_ref, ys_ref, ws_ref),
                                             (wgc_ref, bc_ref, yg_ref, wg_ref)):
        gate = _sigmoid(jnp.dot(h, wgate_ref[...], preferred_element_type=F32) + b_ref[...])
        term = gate * jnp.dot(y_ref[...], wup_ref[...], preferred_element_type=F32)
        m = term if m is None else m + term
    o_ref[...] = m.astype(o_ref.dtype)


def merge_branches(h, w_gate, b_gate, y_pool, y_sb, y_gla, w_p, w_s, w_g, tm=1024, tn=512):
    n, d = h.shape
    nj = d // tn
    gate_w_spec = lambda br: pl.BlockSpec((d, tn), lambda i, j: (0, br * nj + j))
    gate_b_spec = lambda br: pl.BlockSpec((1, tn), lambda i, j: (0, br * nj + j))
    y_spec = lambda w: pl.BlockSpec((tm, w), lambda i, j: (i, 0))
    w_spec = lambda w: pl.BlockSpec((w, tn), lambda i, j: (0, j))
    b_gate = b_gate.reshape(1, N_BRANCH * d)
    return pl.pallas_call(
        _merge_kernel,
        out_shape=jax.ShapeDtypeStruct((n, d), BF16),
        grid=(n // tm, nj),
        in_specs=[y_spec(d), gate_w_spec(0), gate_w_spec(1), gate_w_spec(2),
                  gate_b_spec(0), gate_b_spec(1), gate_b_spec(2),
                  y_spec(POOL_W), y_spec(SB_W), y_spec(GLA_V_W),
                  w_spec(POOL_W), w_spec(SB_W), w_spec(GLA_V_W)],
        out_specs=pl.BlockSpec((tm, tn), lambda i, j: (i, j)),
        compiler_params=_cparams(("parallel", "arbitrary")),
        name="merge_branches",
    )(h, w_gate, w_gate, w_gate, b_gate, b_gate, b_gate, y_pool, y_sb, y_gla, w_p, w_s, w_g)


def _silu(a):
    return a * _sigmoid(a)


def _ffn_kernel(h_ref, w1_ref, w3_ref, w2_ref, o_ref):
    @pl.when(pl.program_id(1) == 0)
    def _():
        o_ref[...] = jnp.zeros_like(o_ref)

    h = h_ref[...]
    a = jnp.dot(h, w1_ref[...], preferred_element_type=F32)
    b = jnp.dot(h, w3_ref[...], preferred_element_type=F32)
    g = (_silu(a) * b).astype(BF16)
    o_ref[...] += jnp.dot(g, w2_ref[...], preferred_element_type=F32)


def ffn_dense(h, w1, w3, w2, tm=1024, tf=512):
    n, d = h.shape
    ff = w1.shape[1]
    return pl.pallas_call(
        _ffn_kernel,
        out_shape=jax.ShapeDtypeStruct((n, d), F32),
        grid=(n // tm, ff // tf),
        in_specs=[pl.BlockSpec((tm, d), lambda i, f: (i, 0)),
                  pl.BlockSpec((d, tf), lambda i, f: (0, f)),
                  pl.BlockSpec((d, tf), lambda i, f: (0, f)),
                  pl.BlockSpec((tf, d), lambda i, f: (f, 0))],
        out_specs=pl.BlockSpec((tm, d), lambda i, f: (i, 0)),
        compiler_params=_cparams(("parallel", "arbitrary")),
        name="ffn_dense",
    )(h, w1, w3, w2)


MOE_TM = 1024
MOE_FEW_ROWS = 256
TOP_K = 2


def _row_copy(src_ref, src_idx, dst_ref, dst_idx, sem):
    return pltpu.make_async_copy(src_ref.at[src_idx], dst_ref.at[dst_idx], sem)


def _dispatch_kernel(dest_ref, h_ref, init_ref, o_ref, sem, *, tm):
    del init_ref
    base = pl.program_id(0) * tm

    def copies(t):
        return [_row_copy(h_ref, t, o_ref, dest_ref[TOP_K * (base + t) + kk], sem)
                for kk in range(TOP_K)]

    def start(t, carry):
        for cp in copies(t):
            cp.start()
        return carry

    def wait(t, carry):
        for cp in copies(t):
            cp.wait()
        return carry

    lax.fori_loop(0, tm, start, 0)
    lax.fori_loop(0, tm, wait, 0)


def moe_dispatch(dest, h_rows, n_rows, tm=256):
    n = h_rows.shape[0]
    grid_spec = pltpu.PrefetchScalarGridSpec(
        num_scalar_prefetch=1,
        grid=(n // tm,),
        in_specs=[pl.BlockSpec((tm, ROW_SLABS, LANES), lambda i, dest: (i, 0, 0)),
                  pl.BlockSpec(memory_space=pl.ANY)],
        out_specs=pl.BlockSpec(memory_space=pl.ANY),
        scratch_shapes=[pltpu.SemaphoreType.DMA],
    )
    return pl.pallas_call(
        functools.partial(_dispatch_kernel, tm=tm),
        out_shape=jax.ShapeDtypeStruct((n_rows, ROW_SLABS, LANES), h_rows.dtype),
        grid_spec=grid_spec,
        input_output_aliases={2: 0},
        compiler_params=_cparams(("arbitrary",)),
        name="moe_dispatch",
    )(dest, h_rows, jnp.zeros((n_rows, ROW_SLABS, LANES), h_rows.dtype))


def _moe_ffn_kernel(te_ref, tr_ref, h_ref, w1_ref, w3_ref, w2_ref, o_ref, hb_ref, acc_ref, *, tm):
    del te_ref
    f = pl.program_id(1)
    last = pl.num_programs(1) - 1
    n_real = tr_ref[pl.program_id(0)]
    used = n_real > 0
    few = n_real <= MOE_FEW_ROWS

    @pl.when(used & (f == 0))
    def _():
        acc_ref[...] = jnp.zeros_like(acc_ref)
        for s in range(ROW_SLABS):
            lo, hi = _load_packed_slab(h_ref, s, tm)
            hb_ref[:, s * LANES:(s + 1) * LANES] = lo.astype(BF16)
            hb_ref[:, HALF_D + s * LANES:HALF_D + (s + 1) * LANES] = hi.astype(BF16)

    def swiglu_rows(rows):
        h = hb_ref[rows, :]
        a = jnp.dot(h, w1_ref[...].astype(BF16), preferred_element_type=F32)
        b = jnp.dot(h, w3_ref[...].astype(BF16), preferred_element_type=F32)
        g = (_silu(a) * b).astype(BF16)
        acc_ref[rows, :] += jnp.dot(g, w2_ref[...].astype(BF16), preferred_element_type=F32)

    @pl.when(used & few)
    def _():
        swiglu_rows(slice(0, MOE_FEW_ROWS))

    @pl.when(used & jnp.logical_not(few))
    def _():
        swiglu_rows(slice(None))

    @pl.when(used & (f == last))
    def _():
        _store_packed_rows(o_ref, lambda c0: acc_ref[:, c0:c0 + LANES], tm)

    @pl.when(jnp.logical_not(used) & (f == last))
    def _():
        o_ref[...] = jnp.zeros_like(o_ref)


def moe_grouped_ffn(tile_expert, tile_rows, h_sorted, w1, w3, w2, tm=MOE_TM, tf=512):
    r = h_sorted.shape[0] // ROW_SLABS
    d, ff = w1.shape[1], w1.shape[2]
    nf = ff // tf

    def f_idx(i, f, tr):
        return jnp.where(tr[i] > 0, f, nf - 1)

    rows = pl.BlockSpec((tm * ROW_SLABS, LANES), lambda i, f, te, tr: (i, 0),
                        pipeline_mode=pl.Buffered(1))
    grid_spec = pltpu.PrefetchScalarGridSpec(
        num_scalar_prefetch=2,
        grid=(r // tm, nf),
        in_specs=[rows,
                  pl.BlockSpec((None, d, tf), lambda i, f, te, tr: (te[i], 0, f_idx(i, f, tr))),
                  pl.BlockSpec((None, d, tf), lambda i, f, te, tr: (te[i], 0, f_idx(i, f, tr))),
                  pl.BlockSpec((None, tf, d), lambda i, f, te, tr: (te[i], f_idx(i, f, tr), 0))],
        out_specs=rows,
        scratch_shapes=[pltpu.VMEM((tm, d), BF16), pltpu.VMEM((tm, d), F32)],
    )
    return pl.pallas_call(
        functools.partial(_moe_ffn_kernel, tm=tm),
        out_shape=jax.ShapeDtypeStruct(h_sorted.shape, jnp.int32),
        grid_spec=grid_spec,
        compiler_params=_cparams(("arbitrary", "arbitrary")),
        name="moe_grouped_ffn",
    )(tile_expert, tile_rows, h_sorted, w1, w3, w2)


def _combine_kernel(dest_ref, w_ref, y_ref, o_ref, buf_ref, sems, *, tm):
    base = pl.program_id(0) * tm

    def copies(t):
        return [_row_copy(y_ref, dest_ref[TOP_K * (base + t) + kk], buf_ref.at[kk], t, sems.at[kk])
                for kk in range(TOP_K)]

    def start(t, carry):
        for cp in copies(t):
            cp.start()
        return carry

    def wait(t, carry):
        for cp in copies(t):
            cp.wait()
        return carry

    lax.fori_loop(0, tm, start, 0)
    lax.fori_loop(0, tm, wait, 0)
    lo = hi = None
    for kk in range(TOP_K):
        words = buf_ref[kk]
        w = w_ref[kk]
        lo_k = lax.bitcast_convert_type(words << 16, F32) * w
        hi_k = lax.bitcast_convert_type(words & _HIGH_HALF, F32) * w
        lo = lo_k if lo is None else lo + lo_k
        hi = hi_k if hi is None else hi + hi_k
    o_ref[0] = lo
    o_ref[1] = hi


def moe_combine(dest, w_rows, y_rows, n, tm=256):
    grid_spec = pltpu.PrefetchScalarGridSpec(
        num_scalar_prefetch=1,
        grid=(n // tm,),
        in_specs=[pl.BlockSpec((TOP_K, tm, 1, LANES), lambda i, dest: (0, i, 0, 0)),
                  pl.BlockSpec(memory_space=pl.ANY)],
        out_specs=pl.BlockSpec((2, tm, ROW_SLABS, LANES), lambda i, dest: (0, i, 0, 0)),
        scratch_shapes=[pltpu.VMEM((TOP_K, tm, ROW_SLABS, LANES), jnp.int32),
                        pltpu.SemaphoreType.DMA((TOP_K,))],
    )
    return pl.pallas_call(
        functools.partial(_combine_kernel, tm=tm),
        out_shape=jax.ShapeDtypeStruct((2, n, ROW_SLABS, LANES), F32),
        grid_spec=grid_spec,
        compiler_params=_cparams(("arbitrary",)),
        name="moe_combine",
    )(dest, w_rows, y_rows)


def moe_sparse(h_rows, info, counts, w1, w3, w2):
    n = h_rows.shape[0] // ROW_SLABS
    tm = MOE_TM
    n_tiles = (TOP_K * n) // tm + N_EXPERTS
    n_rows = n_tiles * tm
    cnt = counts[0, :N_EXPERTS].astype(jnp.int32)
    tiles_per = (cnt + tm - 1) // tm
    tile_end = jnp.cumsum(tiles_per)
    group_start = (tile_end - tiles_per) * tm
    n_used = tile_end[-1]
    tile_ids = jnp.arange(n_tiles, dtype=jnp.int32)
    tile_expert = jnp.minimum(jnp.searchsorted(tile_end, tile_ids, side="right"),
                              N_EXPERTS - 1).astype(jnp.int32)
    tile_rows = jnp.clip(cnt[tile_expert] - (tile_ids * tm - group_start[tile_expert]), 0, tm)
    tile_rows = jnp.where(tile_ids < n_used, tile_rows, 0).astype(jnp.int32)
    tile_expert = jnp.where(tile_ids < n_used, tile_expert,
                            tile_expert[jnp.maximum(n_used - 1, 0)])
    experts = info[:, INFO_E1:INFO_E2 + 1].astype(jnp.int32)
    ranks = info[:, INFO_R1:INFO_R2 + 1].astype(jnp.int32)
    dest = (group_start[experts] + ranks).reshape(TOP_K * n)
    w_rows = jnp.broadcast_to(info[:, INFO_W1:INFO_W2 + 1].T[:, :, None, None],
                              (TOP_K, n, 1, LANES))

    h_sorted = moe_dispatch(dest, h_rows.reshape(n, ROW_SLABS, LANES), n_rows)
    y_sorted = moe_grouped_ffn(tile_expert, tile_rows,
                               h_sorted.reshape(n_rows * ROW_SLABS, LANES), w1, w3, w2)
    y = moe_combine(dest, w_rows, y_sorted.reshape(n_rows, ROW_SLABS, LANES), n)
    return y.reshape(2, n * ROW_SLABS, LANES)


def _ple_kernel(x_ref, y_ref, p_ref, wg_ref, wp_ref, *rest, y_slabs, with_norm):
    if with_norm:
        g_ref, o_ref, h_ref = rest
    else:
        (o_ref,) = rest
    if y_slabs:
        t = x_ref.shape[0]
        parts = []
        for half in range(2):
            for s in range(ROW_SLABS):
                parts.append(y_ref[half, pl.ds(s, t, stride=ROW_SLABS), :])
        x = x_ref[...] + jnp.concatenate(parts, axis=1)
    else:
        x = x_ref[...] + y_ref[...]
    xb = x.astype(BF16)
    pb = p_ref[...].astype(BF16)
    d = x.shape[1]
    sum_sq = jnp.zeros((x.shape[0], 1), F32)
    for c0 in range(0, d, PLE_COL_CHUNK):
        cs = slice(c0, c0 + PLE_COL_CHUNK)
        gate = _sigmoid(jnp.dot(xb, wg_ref[:, cs], preferred_element_type=F32))
        proj = jnp.dot(pb, wp_ref[:, cs], preferred_element_type=F32)
        xc = x[:, cs] + gate * proj
        o_ref[:, cs] = xc
        sum_sq += jnp.sum(xc * xc, axis=-1, keepdims=True)
    if with_norm:
        inv = lax.rsqrt(sum_sq * (1.0 / d) + EPS)
        h_ref[...] = (o_ref[...] * inv * g_ref[...]).astype(h_ref.dtype)


PLE_COL_CHUNK = 512


def ple_update(x, y, p, w_gate, w_proj, norm_gain=None, tm=512):
    n, d = x.shape
    pd = p.shape[1]
    y_slabs = y.ndim == 3
    with_norm = norm_gain is not None
    row = lambda width: pl.BlockSpec((tm, width), lambda i: (i, 0))
    resident = lambda shape: pl.BlockSpec(shape, lambda i: (0, 0), pipeline_mode=pl.Buffered(1))
    y_spec = (pl.BlockSpec((2, tm * ROW_SLABS, LANES), lambda i: (0, i, 0)) if y_slabs else row(d))
    in_specs = [row(d), y_spec, row(pd), resident((d, d)), resident((pd, d))]
    args = [x, y, p, w_gate, w_proj]
    out_shape = [jax.ShapeDtypeStruct((n, d), F32)]
    out_specs = [row(d)]
    if with_norm:
        in_specs.append(pl.BlockSpec((1, d), lambda i: (0, 0)))
        args.append(norm_gain.reshape(1, d))
        out_shape.append(jax.ShapeDtypeStruct((n, d), BF16))
        out_specs.append(row(d))
    res = pl.pallas_call(
        functools.partial(_ple_kernel, y_slabs=y_slabs, with_norm=with_norm),
        out_shape=tuple(out_shape),
        grid=(n // tm,),
        in_specs=in_specs,
        out_specs=tuple(out_specs),
        compiler_params=_cparams(("parallel",)),
        name="ple_update",
    )(*args)
    return res if with_norm else res[0]


def _split_w_in(w_in_i):
    c_lr = COL_GR
    main = jnp.concatenate([w_in_i[:, :c_lr], w_in_i[:, c_lr + GLA_RANK:]], axis=1).astype(BF16)
    glr = jnp.zeros((D_MODEL, LANES), BF16).at[:, :GLA_RANK].set(
        w_in_i[:, c_lr:c_lr + GLA_RANK].astype(BF16))
    return main, glr


def kernel(x, p, g_mix, w_in, w_branch_gate, b_branch_gate, pool_w, pool_scale, sb_gq, sb_gk,
           gla_w_lr, gla_b_lr, gla_g_out, w_up_pool, w_up_sb, w_up_gla, w_o, g_ffn,
           ffn_w1, ffn_w3, ffn_w2, moe_router, moe_w1, moe_w3, moe_w2, ple_w_proj, ple_w_gate):
    bsz, seq, d = x.shape
    n = bsz * seq
    depth = w_in.shape[0]
    xf = x.reshape(n, d)
    h = rmsnorm(xf, g_mix[0])
    for i in range(depth):
        w_main, w_glr = _split_w_in(w_in[i])
        w_lr_pad = jnp.zeros((LANES, GLA_K_W), F32).at[:GLA_RANK].set(gla_w_lr[i])
        qk_gain = jnp.ones((1, Z_COLS), F32)
        qk_gain = qk_gain.at[0, COL_SQ:COL_SK].set(jnp.tile(sb_gq[i] * SB_HEAD_DIM ** -0.5, SB_HEADS))
        qk_gain = qk_gain.at[0, COL_SK:COL_SV].set(jnp.tile(sb_gk[i], SB_HEADS))

        z = in_proj(h, w_main, qk_gain)
        bcum = gla_gate(h, w_glr, w_lr_pad, gla_b_lr[i])

        z3 = z.reshape(bsz, seq, Z_COLS)
        y_pool = pool_mixer(z3, pool_w[i].astype(BF16), pool_scale[i])
        y_sb = sb_attention(z3)
        y_gla = gla_mixer(z3, bcum.reshape(bsz, seq, GLA_K_W), gla_g_out[i])

        merged = merge_branches(h, w_branch_gate[i].astype(BF16), b_branch_gate[i],
                                y_pool.reshape(n, POOL_W), y_sb.reshape(n, SB_W),
                                y_gla.reshape(n, GLA_V_W), w_up_pool[i].astype(BF16),
                                w_up_sb[i].astype(BF16), w_up_gla[i].astype(BF16))

        j = i // 2
        if i % 2 == 0:
            xf, h2 = out_proj(merged, w_o[i].astype(BF16), xf, g_ffn[i])
            y = ffn_dense(h2, ffn_w1[j].astype(BF16), ffn_w3[j].astype(BF16),
                          ffn_w2[j].astype(BF16))
        else:
            xf, h2_rows = out_proj(merged, w_o[i].astype(BF16), xf, g_ffn[i], pack_rows=True)
            info, counts = route_tokens(xf, g_ffn[i], moe_router[j])
            y = moe_sparse(h2_rows, info, counts, moe_w1[j], moe_w3[j], moe_w2[j])

        if i + 1 < depth:
            xf, h = ple_update(xf, y, p[i].reshape(n, PLE_DIM), ple_w_gate[i].astype(BF16),
                               ple_w_proj[i].astype(BF16), norm_gain=g_mix[i + 1])
        else:
            xf = ple_update(xf, y, p[i].reshape(n, PLE_DIM), ple_w_gate[i].astype(BF16),
                            ple_w_proj[i].astype(BF16))
    return xf.reshape(bsz, seq, d)
```

```python
import functools

import jax
import jax.numpy as jnp
from jax import lax
from jax.experimental import pallas as pl
from jax.experimental.pallas import tpu as pltpu

F32 = jnp.float32
BF16 = jnp.bfloat16

EPS = 1e-6
D_MODEL = 2048
PLE_DIM = 256
POOL_GROUPS = 4
POOL_GROUP_W = 256
POOL_W = POOL_GROUPS * POOL_GROUP_W
POOL_WINDOWS = (2, 4, 8, 16)
SB_HEADS = 8
SB_HEAD_DIM = 128
SB_W = SB_HEADS * SB_HEAD_DIM
GLA_HEADS = 4
GLA_DK = 128
GLA_DV = 256
GLA_K_W = GLA_HEADS * GLA_DK
GLA_V_W = GLA_HEADS * GLA_DV
GLA_RANK = 16
GLA_TAU = 16.0
D_FF = 5632
N_EXPERTS = 8
N_BRANCH = 3

LANES = 128
VMEM_LIMIT = 56 * 1024 * 1024

COL_POOL = 0
COL_SQ = COL_POOL + POOL_W
COL_SK = COL_SQ + SB_W
COL_SV = COL_SK + SB_W
COL_GQ = COL_SV + SB_W
COL_GK = COL_GQ + GLA_K_W
COL_GV = COL_GK + GLA_K_W
COL_GR = COL_GV + GLA_V_W
Z_COLS = COL_GR + GLA_V_W

GLA_CHUNK = 64
GLA_SUB = 16
GLA_EXP_CAP = 80.0
SB_F32_ZERO_LOG = -104.0


def _cparams(sem):
    return pltpu.CompilerParams(dimension_semantics=sem, vmem_limit_bytes=VMEM_LIMIT)


def _log_sigmoid(z):
    return jnp.minimum(z, 0.0) - jnp.log(1.0 + jnp.exp(-jnp.abs(z)))


def _sigmoid(z):
    return 1.0 / (1.0 + jnp.exp(-z))


def _split_bf16(x):
    hi = x.astype(BF16)
    lo = (x - hi.astype(F32)).astype(BF16)
    return hi, lo


def _route_top2(h, rhi_ref, rlo_ref, info_ref, cnt_ref, carry_ref):
    tm = h.shape[0]

    @pl.when(pl.program_id(0) == 0)
    def _():
        carry_ref[...] = jnp.zeros_like(carry_ref)

    h_hi, h_lo = _split_bf16(h)
    logits = (jnp.dot(h_hi, rhi_ref[...], preferred_element_type=F32)
              + jnp.dot(h_lo, rhi_ref[...], preferred_element_type=F32)
              + jnp.dot(h_hi, rlo_ref[...], preferred_element_type=F32))
    lane = lax.broadcasted_iota(jnp.int32, logits.shape, 1).astype(F32)
    neg = jnp.float32(-jnp.inf)
    logits = jnp.where(lane < N_EXPERTS, logits, neg)
    m1 = jnp.max(logits, axis=-1, keepdims=True)
    i1 = jnp.min(jnp.where(logits == m1, lane, float(LANES)), axis=-1, keepdims=True)
    sel1 = lane == i1
    rest = jnp.where(sel1, neg, logits)
    m2 = jnp.max(rest, axis=-1, keepdims=True)
    i2 = jnp.min(jnp.where(rest == m2, lane, float(LANES)), axis=-1, keepdims=True)
    sel2 = lane == i2
    e2 = jnp.exp(m2 - m1)
    den = 1.0 + e2
    sel = (sel1 | sel2).astype(BF16)
    before = (lax.broadcasted_iota(jnp.int32, (tm, tm), 1)
              < lax.broadcasted_iota(jnp.int32, (tm, tm), 0)).astype(BF16)
    carry = carry_ref[...]
    rank = jnp.dot(before, sel, preferred_element_type=F32) + carry[0:1, :]
    r1 = jnp.sum(jnp.where(sel1, rank, 0.0), axis=-1, keepdims=True)
    r2 = jnp.sum(jnp.where(sel2, rank, 0.0), axis=-1, keepdims=True)
    fields = (i1, i2, 1.0 / den, e2 / den, r1, r2)
    info = jnp.zeros_like(logits)
    for li, val in enumerate(fields):
        info = jnp.where(lane == li, val, info)
    info_ref[...] = info
    carry = carry + jnp.sum(sel.astype(F32), axis=0, keepdims=True)
    carry_ref[...] = carry
    cnt_ref[...] = carry


INFO_E1, INFO_E2, INFO_W1, INFO_W2, INFO_R1, INFO_R2 = range(6)


def _in_proj_kernel(a_ref, w_ref, g_ref, o_ref, *, tn):
    j = pl.program_id(1)
    acc = jnp.dot(a_ref[...], w_ref[...], preferred_element_type=F32)
    is_qk = (j >= COL_SQ // tn) & (j < COL_SV // tn)
    for hh in range(tn // SB_HEAD_DIM):
        cs = slice(hh * SB_HEAD_DIM, (hh + 1) * SB_HEAD_DIM)
        blk = acc[:, cs]
        o_ref[:, cs] = jnp.where(is_qk, _head_rmsnorm(blk, g_ref[:, cs]), blk).astype(o_ref.dtype)


def _norm_in_proj_kernel(x_ref, gm_ref, w_ref, g_ref, o_ref, h_ref, hb_ref, *, tn):
    @pl.when(pl.program_id(1) == 0)
    def _():
        x = x_ref[...]
        ms = jnp.mean(x * x, axis=-1, keepdims=True)
        hb_ref[...] = (x * lax.rsqrt(ms + EPS) * gm_ref[...]).astype(BF16)
        h_ref[...] = hb_ref[...]

    _in_proj_kernel(hb_ref, w_ref, g_ref, o_ref, tn=tn)


def norm_in_proj(x, g_mix, w, qk_gain, tm=1024, tn=1024):
    n, k = x.shape
    m = w.shape[1]
    assert COL_SQ % tn == 0 and COL_SV % tn == 0 and tn % SB_HEAD_DIM == 0
    return pl.pallas_call(
        functools.partial(_norm_in_proj_kernel, tn=tn),
        out_shape=(jax.ShapeDtypeStruct((n, m), BF16), jax.ShapeDtypeStruct((n, k), BF16)),
        grid=(n // tm, m // tn),
        in_specs=[pl.BlockSpec((tm, k), lambda i, j: (i, 0)),
                  pl.BlockSpec((1, k), lambda i, j: (0, 0)),
                  pl.BlockSpec((k, tn), lambda i, j: (0, j)),
                  pl.BlockSpec((1, tn), lambda i, j: (0, j))],
        out_specs=(pl.BlockSpec((tm, tn), lambda i, j: (i, j)),
                   pl.BlockSpec((tm, k), lambda i, j: (i, 0))),
        scratch_shapes=[pltpu.VMEM((tm, k), BF16)],
        compiler_params=_cparams(("parallel", "arbitrary")),
        name="norm_in_proj",
    )(x, g_mix.reshape(1, k), w, qk_gain)


def in_proj(h, w, qk_gain, tm=1024, tn=1024):
    n, k = h.shape
    m = w.shape[1]
    assert COL_SQ % tn == 0 and COL_SV % tn == 0 and tn % SB_HEAD_DIM == 0
    return pl.pallas_call(
        functools.partial(_in_proj_kernel, tn=tn),
        out_shape=jax.ShapeDtypeStruct((n, m), BF16),
        grid=(n // tm, m // tn),
        in_specs=[pl.BlockSpec((tm, k), lambda i, j: (i, 0)),
                  pl.BlockSpec((k, tn), lambda i, j: (0, j)),
                  pl.BlockSpec((1, tn), lambda i, j: (0, j))],
        out_specs=pl.BlockSpec((tm, tn), lambda i, j: (i, j)),
        compiler_params=_cparams(("parallel", "arbitrary")),
        name="in_proj",
    )(h, w, qk_gain)


ROW_SLABS = D_MODEL // (2 * LANES)
HALF_D = D_MODEL // 2
_HIGH_HALF = -65536


def _bf16_bits(x):
    return lax.bitcast_convert_type(x.astype(BF16).astype(F32), jnp.int32)


def _store_packed_rows(ref, read_cols, t):
    for s in range(ROW_SLABS):
        lo = (_bf16_bits(read_cols(s * LANES)) >> 16) & 0xFFFF
        hi = _bf16_bits(read_cols(HALF_D + s * LANES)) & _HIGH_HALF
        ref[pl.ds(s, t, stride=ROW_SLABS), :] = lo | hi


def _load_packed_slab(ref, s, t):
    w = ref[pl.ds(s, t, stride=ROW_SLABS), :]
    lo = lax.bitcast_convert_type(w << 16, F32)
    hi = lax.bitcast_convert_type(w & _HIGH_HALF, F32)
    return lo, hi


def _row_proj_kernel(a_ref, w_ref, r_ref, g_ref, o_ref, h_ref, *, pack_rows):
    x = r_ref[...] + jnp.dot(a_ref[...], w_ref[...], preferred_element_type=F32)
    ms = jnp.mean(x * x, axis=-1, keepdims=True)
    o_ref[...] = x
    h = x * lax.rsqrt(ms + EPS) * g_ref[...]
    if pack_rows:
        _store_packed_rows(h_ref, lambda c0: h[:, c0:c0 + LANES], x.shape[0])
    else:
        h_ref[...] = h.astype(h_ref.dtype)


def out_proj(a, w, resid, norm_gain, pack_rows=False, tm=512):
    n, k = a.shape
    d = w.shape[1]
    row = lambda width: pl.BlockSpec((tm, width), lambda i: (i, 0))
    if pack_rows:
        h_shape = jax.ShapeDtypeStruct((n * ROW_SLABS, LANES), jnp.int32)
        h_spec = pl.BlockSpec((tm * ROW_SLABS, LANES), lambda i: (i, 0))
    else:
        h_shape, h_spec = jax.ShapeDtypeStruct((n, d), BF16), row(d)
    return pl.pallas_call(
        functools.partial(_row_proj_kernel, pack_rows=pack_rows),
        out_shape=(jax.ShapeDtypeStruct((n, d), F32), h_shape),
        grid=(n // tm,),
        in_specs=[row(k), pl.BlockSpec((k, d), lambda i: (0, 0), pipeline_mode=pl.Buffered(1)),
                  row(d), pl.BlockSpec((1, d), lambda i: (0, 0))],
        out_specs=(row(d), h_spec),
        compiler_params=_cparams(("parallel",)),
        name="out_proj",
    )(a, w, resid, norm_gain.reshape(1, d))


def _router_kernel(x_ref, g_ref, rhi_ref, rlo_ref, info_ref, cnt_ref, carry_ref):
    x = x_ref[...]
    ms = jnp.mean(x * x, axis=-1, keepdims=True)
    h = x * lax.rsqrt(ms + EPS) * g_ref[...]
    _route_top2(h, rhi_ref, rlo_ref, info_ref, cnt_ref, carry_ref)


def route_tokens(x, g, router, tm=512):
    n, d = x.shape
    r_pad = jnp.zeros((d, LANES), F32).at[:, :N_EXPERTS].set(router)
    fixed = lambda shape: pl.BlockSpec(shape, lambda i: (0, 0))
    return pl.pallas_call(
        _router_kernel,
        out_shape=(jax.ShapeDtypeStruct((n, LANES), F32), jax.ShapeDtypeStruct((8, LANES), F32)),
        grid=(n // tm,),
        in_specs=[pl.BlockSpec((tm, d), lambda i: (i, 0)), fixed((1, d)),
                  fixed((d, LANES)), fixed((d, LANES))],
        out_specs=(pl.BlockSpec((tm, LANES), lambda i: (i, 0)), fixed((8, LANES))),
        scratch_shapes=[pltpu.VMEM((8, LANES), F32)],
        compiler_params=_cparams(("arbitrary",)),
        name="route_tokens",
    )(x, g.reshape(1, d), *_split_bf16(r_pad))


POOL_HALO = 128


def _pool_kernel(u_ref, w_ref, s_ref, o_ref, prev_ref, *, t):
    sb = pl.program_id(1)

    @pl.when(sb == 0)
    def _():
        prev_ref[...] = jnp.zeros_like(prev_ref)

    row = lax.broadcasted_iota(jnp.int32, (t, t), 0)
    col = lax.broadcasted_iota(jnp.int32, (t, t), 1)
    prow = lax.broadcasted_iota(jnp.int32, (t, POOL_HALO), 0)
    pcol = lax.broadcasted_iota(jnp.int32, (t, POOL_HALO), 1) - POOL_HALO
    tg = sb * t + lax.broadcasted_iota(jnp.int32, (t, 1), 0)
    for gi, w in enumerate(POOL_WINDOWS):
        cs = slice(gi * POOL_GROUP_W, (gi + 1) * POOL_GROUP_W)
        u = u_ref[:, cs]
        band_cur = ((col <= row) & (col > row - w)).astype(BF16)
        band_prev = ((pcol > prow - w) & (pcol + sb * t >= 0)).astype(BF16)
        win = jnp.dot(band_cur, u, preferred_element_type=F32)
        win = win + jnp.dot(band_prev, prev_ref[:, cs], preferred_element_type=F32)
        count = jnp.minimum(tg + 1, w).astype(F32)
        pooled = win / count - u.astype(F32)
        mixed = jnp.dot(pooled.astype(BF16), w_ref[gi], preferred_element_type=F32)
        o_ref[:, cs] = (mixed * s_ref[:, cs]).astype(o_ref.dtype)
    prev_ref[...] = u_ref[t - POOL_HALO:, :]


def pool_mixer(z3, pool_w, scale, t=256):
    b, s, _ = z3.shape
    return pl.pallas_call(
        functools.partial(_pool_kernel, t=t),
        out_shape=jax.ShapeDtypeStruct((b, s, POOL_W), BF16),
        grid=(b, s // t),
        in_specs=[pl.BlockSpec((None, t, POOL_W), lambda bi, si: (bi, si, COL_POOL // POOL_W)),
                  pl.BlockSpec((POOL_GROUPS, POOL_GROUP_W, POOL_GROUP_W), lambda bi, si: (0, 0, 0)),
                  pl.BlockSpec((1, POOL_W), lambda bi, si: (0, 0))],
        out_specs=pl.BlockSpec((None, t, POOL_W), lambda bi, si: (bi, si, 0)),
        scratch_shapes=[pltpu.VMEM((POOL_HALO, POOL_W), BF16)],
        compiler_params=_cparams(("parallel", "arbitrary")),
        name="pool_mixer",
    )(z3, pool_w, scale.reshape(1, POOL_W))


def _head_rmsnorm(x, g):
    ms = jnp.mean(x * x, axis=-1, keepdims=True)
    return x * lax.rsqrt(ms + EPS) * g


def _sb_kernel(q_ref, k_ref, v_ref, o_ref, acc_ref, c_ref, *, t, nh):
    i = pl.program_id(2)
    hd = SB_HEAD_DIM
    row = lax.broadcasted_iota(jnp.int32, (t, t), 0)
    col = lax.broadcasted_iota(jnp.int32, (t, t), 1)
    diag_mask = col < row
    suffix = (row > col).astype(BF16)
    suffix2 = jnp.concatenate([suffix, suffix], axis=0)
    has_prev = jnp.broadcast_to(i > 0, (t, t))

    def block(h, start, mask, c):
        hs = slice(h * hd, (h + 1) * hd)
        k = k_ref[pl.ds(start, t), hs]
        z = lax.dot_general(q_ref[:, hs], k, (((1,), (1,)), ((), ())),
                            preferred_element_type=F32)
        soft = jnp.log(1.0 + jnp.exp(-jnp.abs(z)))
        neg_part = jnp.minimum(z, 0.0)
        log_beta = neg_part - soft
        log_keep = (neg_part - z) - soft
        if mask is not None:
            log_keep = jnp.where(mask, log_keep, 0.0)
        hi, lo = _split_bf16(log_keep)
        later = jnp.dot(jnp.concatenate([hi, lo], axis=1), suffix2, preferred_element_type=F32)
        a = jnp.exp(log_beta + later + c)
        if mask is not None:
            a = jnp.where(mask, a, 0.0)
        out = jnp.dot(a.astype(BF16), v_ref[pl.ds(start, t), hs], preferred_element_type=F32)
        return out, c + later[:, 0:1] + log_keep[:, 0:1]

    diag_start = pl.multiple_of(i * t, t)
    prev_start = pl.multiple_of(jnp.maximum(i - 1, 0) * t, t)
    live = jnp.int32(0)
    for h in range(nh):
        out_a, c = block(h, diag_start, diag_mask, jnp.zeros((t, 1), F32))
        out_b, c = block(h, prev_start, has_prev, c)
        acc_ref[:, h * hd:(h + 1) * hd] = out_a + out_b
        c_ref[h] = c
        live = jnp.maximum(live, (jnp.max(c) > SB_F32_ZERO_LOG).astype(jnp.int32))

    def body(carry):
        j, _ = carry
        start = pl.multiple_of(j * t, t)
        live = jnp.int32(0)
        for h in range(nh):
            out, c = block(h, start, None, c_ref[h])
            acc_ref[:, h * hd:(h + 1) * hd] += out
            c_ref[h] = c
            live = jnp.maximum(live, (jnp.max(c) > SB_F32_ZERO_LOG).astype(jnp.int32))
        return j - 1, live

    def cond(carry):
        j, live = carry
        return jnp.logical_and(j >= 0, live > 0)

    lax.while_loop(cond, body, (i - 2, live))
    o_ref[...] = acc_ref[...].astype(o_ref.dtype)


def sb_attention(z3, t=256, nh=8):
    b, s, _ = z3.shape
    w = nh * SB_HEAD_DIM
    kv_spec = lambda col: pl.BlockSpec((None, s, w), lambda bi, h, i: (bi, 0, col // w + h),
                                       pipeline_mode=pl.Buffered(1))
    return pl.pallas_call(
        functools.partial(_sb_kernel, t=t, nh=nh),
        out_shape=jax.ShapeDtypeStruct((b, s, SB_W), BF16),
        grid=(b, SB_HEADS // nh, s // t),
        in_specs=[pl.BlockSpec((None, t, w), lambda bi, h, i: (bi, i, COL_SQ // w + h)),
                  kv_spec(COL_SK), kv_spec(COL_SV)],
        out_specs=pl.BlockSpec((None, t, w), lambda bi, h, i: (bi, i, h)),
        scratch_shapes=[pltpu.VMEM((t, w), F32), pltpu.VMEM((nh, t, 1), F32)],
        compiler_params=_cparams(("parallel", "parallel", "arbitrary")),
        name="sb_attention",
    )(z3, z3, z3)


def _gla_gate_kernel(h_ref, wg_ref, wlr_hi_ref, wlr_lo_ref, blr_ref, o_ref, *, tm):
    g = jnp.dot(h_ref[...], wg_ref[...], preferred_element_type=F32)
    g_hi, g_lo = _split_bf16(g)
    pre = (jnp.dot(g_hi, wlr_hi_ref[...], preferred_element_type=F32)
           + jnp.dot(g_lo, wlr_hi_ref[...], preferred_element_type=F32)
           + jnp.dot(g_hi, wlr_lo_ref[...], preferred_element_type=F32)) + blr_ref[...]
    log_a = _log_sigmoid(pre) * (1.0 / GLA_TAU)
    row = lax.broadcasted_iota(jnp.int32, (tm, tm), 0)
    col = lax.broadcasted_iota(jnp.int32, (tm, tm), 1)
    shift = GLA_CHUNK.bit_length() - 1
    tri = ((col <= row) & ((col >> shift) == (row >> shift))).astype(BF16)
    hi, lo = _split_bf16(log_a)
    o_ref[...] = (jnp.dot(tri, hi, preferred_element_type=F32)
                  + jnp.dot(tri, lo, preferred_element_type=F32))


def gla_gate(h, w_glr, w_lr, b_lr, tm=256):
    n, d = h.shape
    return pl.pallas_call(
        functools.partial(_gla_gate_kernel, tm=tm),
        out_shape=jax.ShapeDtypeStruct((n, GLA_K_W), F32),
        grid=(n // tm,),
        in_specs=[pl.BlockSpec((tm, d), lambda i: (i, 0)),
                  pl.BlockSpec((d, LANES), lambda i: (0, 0)),
                  pl.BlockSpec((LANES, GLA_K_W), lambda i: (0, 0)),
                  pl.BlockSpec((LANES, GLA_K_W), lambda i: (0, 0)),
                  pl.BlockSpec((1, GLA_K_W), lambda i: (0, 0))],
        out_specs=pl.BlockSpec((tm, GLA_K_W), lambda i: (i, 0)),
        compiler_params=_cparams(("parallel",)),
        name="gla_gate",
    )(h, w_glr, *_split_bf16(w_lr), b_lr.reshape(1, GLA_K_W))


def _gla_kernel(q_ref, k_ref, v_ref, r_ref, b_ref, g_ref, o_ref, st_ref, *, tc):
    @pl.when(pl.program_id(2) == 0)
    def _():
        st_ref[...] = jnp.zeros_like(st_ref)

    c = GLA_CHUNK
    sub = GLA_SUB
    causal = (lax.broadcasted_iota(jnp.int32, (c, c), 1)
              <= lax.broadcasted_iota(jnp.int32, (c, c), 0))
    g_out = g_ref[...]

    def chunk(ci, carry):
        base = pl.multiple_of(ci * c, c)
        rows = pl.ds(base, c)
        q = q_ref[rows, :].astype(F32) * (GLA_DK ** -0.5)
        k = k_ref[rows, :].astype(F32)
        v = v_ref[rows, :]
        b = b_ref[rows, :]
        b_end = b_ref[pl.ds(base + c - 1, 1), :]
        st = st_ref[...]
        inter = lax.dot_general((q * jnp.exp(b)).astype(BF16), st.astype(BF16),
                                (((1,), (1,)), ((), ())), preferred_element_type=F32)
        scores = []
        for si in range(c // sub):
            lo, hi = si * sub, (si + 1) * sub
            if si > 0:
                ref = b_ref[pl.ds(base + lo - 1, 1), :]
            else:
                ref = jnp.zeros((1, GLA_DK), F32)
            q_t = (q[lo:hi] * jnp.exp(b[lo:hi] - ref)).astype(BF16)
            k_t = (k * jnp.exp(jnp.minimum(ref - b, GLA_EXP_CAP))).astype(BF16)
            scores.append(lax.dot_general(q_t, k_t, (((1,), (1,)), ((), ())),
                                          preferred_element_type=F32))
        sc = jnp.where(causal, jnp.concatenate(scores, axis=0), 0.0)
        o = inter + jnp.dot(sc.astype(BF16), v, preferred_element_type=F32)
        k_e = (k * jnp.exp(b_end - b)).astype(BF16)
        v_t = v.astype(F32).T.astype(BF16)
        st_ref[...] = st * jnp.exp(b_end) + jnp.dot(v_t, k_e, preferred_element_type=F32)
        ms = jnp.mean(o * o, axis=-1, keepdims=True)
        o = o * lax.rsqrt(ms + EPS) * g_out
        r = r_ref[rows, :].astype(F32)
        o_ref[rows, :] = (o * (r * _sigmoid(r))).astype(o_ref.dtype)
        return carry

    lax.fori_loop(0, tc // c, chunk, 0, unroll=True)


def gla_mixer(z3, bcum3, g_out, tc=1024):
    b, s, _ = z3.shape
    dk, dv = GLA_DK, GLA_DV
    return pl.pallas_call(
        functools.partial(_gla_kernel, tc=tc),
        out_shape=jax.ShapeDtypeStruct((b, s, GLA_V_W), BF16),
        grid=(b, GLA_HEADS, s // tc),
        in_specs=[pl.BlockSpec((None, tc, dk), lambda bi, h, i: (bi, i, COL_GQ // dk + h)),
                  pl.BlockSpec((None, tc, dk), lambda bi, h, i: (bi, i, COL_GK // dk + h)),
                  pl.BlockSpec((None, tc, dv), lambda bi, h, i: (bi, i, COL_GV // dv + h)),
                  pl.BlockSpec((None, tc, dv), lambda bi, h, i: (bi, i, COL_GR // dv + h)),
                  pl.BlockSpec((None, tc, dk), lambda bi, h, i: (bi, i, h)),
                  pl.BlockSpec((1, dv), lambda bi, h, i: (0, 0))],
        out_specs=pl.BlockSpec((None, tc, dv), lambda bi, h, i: (bi, i, h)),
        scratch_shapes=[pltpu.VMEM((dv, dk), F32)],
        compiler_params=_cparams(("parallel", "parallel", "arbitrary")),
        name="gla_mixer",
    )(z3, z3, z3, z3, bcum3, g_out.reshape(1, dv))


def _merge_kernel(h_ref, wga_ref, wgb_ref, wgc_ref, ba_ref, bb_ref, bc_ref,
                  yp_ref, ys_ref, yg_ref, wp_ref, ws_ref, wg_ref, o_ref):
    h = h_ref[...]
    m = None
    for wgate_ref, b_ref, y_ref, wup_ref in ((wga_ref, ba_ref, yp_ref, wp_ref),
                                             (wgb_ref, bb_ref, ys_ref, ws_ref),
                                             (wgc_ref, bc_ref, yg_ref, wg_ref)):
        gate = _sigmoid(jnp.dot(h, wgate_ref[...], preferred_element_type=F32) + b_ref[...])
        term = gate * jnp.dot(y_ref[...], wup_ref[...], preferred_element_type=F32)
        m = term if m is None else m + term
    o_ref[...] = m.astype(o_ref.dtype)


def merge_branches(h, w_gate, b_gate, y_pool, y_sb, y_gla, w_p, w_s, w_g, tm=1024, tn=512):
    n, d = h.shape
    nj = d // tn
    gate_w_spec = lambda br: pl.BlockSpec((d, tn), lambda i, j: (0, br * nj + j))
    gate_b_spec = lambda br: pl.BlockSpec((1, tn), lambda i, j: (0, br * nj + j))
    y_spec = lambda w: pl.BlockSpec((tm, w), lambda i, j: (i, 0))
    w_spec = lambda w: pl.BlockSpec((w, tn), lambda i, j: (0, j))
    b_gate = b_gate.reshape(1, N_BRANCH * d)
    return pl.pallas_call(
        _merge_kernel,
        out_shape=jax.ShapeDtypeStruct((n, d), BF16),
        grid=(n // tm, nj),
        in_specs=[y_spec(d), gate_w_spec(0), gate_w_spec(1), gate_w_spec(2),
                  gate_b_spec(0), gate_b_spec(1), gate_b_spec(2),
                  y_spec(POOL_W), y_spec(SB_W), y_spec(GLA_V_W),
                  w_spec(POOL_W), w_spec(SB_W), w_spec(GLA_V_W)],
        out_specs=pl.BlockSpec((tm, tn), lambda i, j: (i, j)),
        compiler_params=_cparams(("parallel", "arbitrary")),
        name="merge_branches",
    )(h, w_gate, w_gate, w_gate, b_gate, b_gate, b_gate, y_pool, y_sb, y_gla, w_p, w_s, w_g)


def _silu(a):
    return a * _sigmoid(a)


def _ffn_kernel(h_ref, w1_ref, w3_ref, w2_ref, o_ref):
    @pl.when(pl.program_id(1) == 0)
    def _():
        o_ref[...] = jnp.zeros_like(o_ref)

    h = h_ref[...]
    a = jnp.dot(h, w1_ref[...], preferred_element_type=F32)
    b = jnp.dot(h, w3_ref[...], preferred_element_type=F32)
    g = (_silu(a) * b).astype(BF16)
    o_ref[...] += jnp.dot(g, w2_ref[...], preferred_element_type=F32)


def ffn_dense(h, w1, w3, w2, tm=1024, tf=512):
    n, d = h.shape
    ff = w1.shape[1]
    return pl.pallas_call(
        _ffn_kernel,
        out_shape=jax.ShapeDtypeStruct((n, d), F32),
        grid=(n // tm, ff // tf),
        in_specs=[pl.BlockSpec((tm, d), lambda i, f: (i, 0)),
                  pl.BlockSpec((d, tf), lambda i, f: (0, f)),
                  pl.BlockSpec((d, tf), lambda i, f: (0, f)),
                  pl.BlockSpec((tf, d), lambda i, f: (f, 0))],
        out_specs=pl.BlockSpec((tm, d), lambda i, f: (i, 0)),
        compiler_params=_cparams(("parallel", "arbitrary")),
        name="ffn_dense",
    )(h, w1, w3, w2)


MOE_TM = 1024
MOE_FEW_ROWS = 256
TOP_K = 2


def _row_copy(src_ref, src_idx, dst_ref, dst_idx, sem):
    return pltpu.make_async_copy(src_ref.at[src_idx], dst_ref.at[dst_idx], sem)


def _dispatch_kernel(dest_ref, h_ref, init_ref, o_ref, sem, *, tm):
    del init_ref
    base = pl.program_id(0) * tm

    def copies(t):
        return [_row_copy(h_ref, t, o_ref, dest_ref[TOP_K * (base + t) + kk], sem)
                for kk in range(TOP_K)]

    def start(t, carry):
        for cp in copies(t):
            cp.start()
        return carry

    def wait(t, carry):
        for cp in copies(t):
            cp.wait()
        return carry

    lax.fori_loop(0, tm, start, 0)
    lax.fori_loop(0, tm, wait, 0)


def moe_dispatch(dest, h_rows, n_rows, tm=256):
    n = h_rows.shape[0]
    grid_spec = pltpu.PrefetchScalarGridSpec(
        num_scalar_prefetch=1,
        grid=(n // tm,),
        in_specs=[pl.BlockSpec((tm, ROW_SLABS, LANES), lambda i, dest: (i, 0, 0)),
                  pl.BlockSpec(memory_space=pl.ANY)],
        out_specs=pl.BlockSpec(memory_space=pl.ANY),
        scratch_shapes=[pltpu.SemaphoreType.DMA],
    )
    return pl.pallas_call(
        functools.partial(_dispatch_kernel, tm=tm),
        out_shape=jax.ShapeDtypeStruct((n_rows, ROW_SLABS, LANES), h_rows.dtype),
        grid_spec=grid_spec,
        input_output_aliases={2: 0},
        compiler_params=_cparams(("arbitrary",)),
        name="moe_dispatch",
    )(dest, h_rows, jnp.zeros((n_rows, ROW_SLABS, LANES), h_rows.dtype))


def _moe_ffn_kernel(te_ref, tr_ref, h_ref, w1_ref, w3_ref, w2_ref, o_ref, hb_ref, acc_ref, *, tm):
    del te_ref
    f = pl.program_id(1)
    last = pl.num_programs(1) - 1
    n_real = tr_ref[pl.program_id(0)]
    used = n_real > 0
    few = n_real <= MOE_FEW_ROWS

    @pl.when(used & (f == 0))
    def _():
        acc_ref[...] = jnp.zeros_like(acc_ref)
        for s in range(ROW_SLABS):
            lo, hi = _load_packed_slab(h_ref, s, tm)
            hb_ref[:, s * LANES:(s + 1) * LANES] = lo.astype(BF16)
            hb_ref[:, HALF_D + s * LANES:HALF_D + (s + 1) * LANES] = hi.astype(BF16)

    def swiglu_rows(rows):
        h = hb_ref[rows, :]
        a = jnp.dot(h, w1_ref[...], preferred_element_type=F32)
        b = jnp.dot(h, w3_ref[...], preferred_element_type=F32)
        g = (_silu(a) * b).astype(BF16)
        acc_ref[rows, :] += jnp.dot(g, w2_ref[...], preferred_element_type=F32)

    @pl.when(used & few)
    def _():
        swiglu_rows(slice(0, MOE_FEW_ROWS))

    @pl.when(used & jnp.logical_not(few))
    def _():
        swiglu_rows(slice(None))

    @pl.when(used & (f == last))
    def _():
        _store_packed_rows(o_ref, lambda c0: acc_ref[:, c0:c0 + LANES], tm)

    @pl.when(jnp.logical_not(used) & (f == last))
    def _():
        o_ref[...] = jnp.zeros_like(o_ref)


def moe_grouped_ffn(tile_expert, tile_rows, h_sorted, w1, w3, w2, tm=MOE_TM, tf=512):
    r = h_sorted.shape[0] // ROW_SLABS
    d, ff = w1.shape[1], w1.shape[2]
    nf = ff // tf

    def f_idx(i, f, tr):
        return jnp.where(tr[i] > 0, f, nf - 1)

    rows = pl.BlockSpec((tm * ROW_SLABS, LANES), lambda i, f, te, tr: (i, 0))
    grid_spec = pltpu.PrefetchScalarGridSpec(
        num_scalar_prefetch=2,
        grid=(r // tm, nf),
        in_specs=[rows,
                  pl.BlockSpec((None, d, tf), lambda i, f, te, tr: (te[i], 0, f_idx(i, f, tr))),
                  pl.BlockSpec((None, d, tf), lambda i, f, te, tr: (te[i], 0, f_idx(i, f, tr))),
                  pl.BlockSpec((None, tf, d), lambda i, f, te, tr: (te[i], f_idx(i, f, tr), 0))],
        out_specs=rows,
        scratch_shapes=[pltpu.VMEM((tm, d), BF16), pltpu.VMEM((tm, d), F32)],
    )
    return pl.pallas_call(
        functools.partial(_moe_ffn_kernel, tm=tm),
        out_shape=jax.ShapeDtypeStruct(h_sorted.shape, jnp.int32),
        grid_spec=grid_spec,
        compiler_params=_cparams(("arbitrary", "arbitrary")),
        name="moe_grouped_ffn",
    )(tile_expert, tile_rows, h_sorted, w1, w3, w2)


def _combine_kernel(dest_ref, w_ref, y_ref, o_ref, buf_ref, sems, *, tm):
    base = pl.program_id(0) * tm

    def copies(t):
        return [_row_copy(y_ref, dest_ref[TOP_K * (base + t) + kk], buf_ref.at[kk], t, sems.at[kk])
                for kk in range(TOP_K)]

    def start(t, carry):
        for cp in copies(t):
            cp.start()
        return carry

    def wait(t, carry):
        for cp in copies(t):
            cp.wait()
        return carry

    lax.fori_loop(0, tm, start, 0)
    lax.fori_loop(0, tm, wait, 0)
    lo = hi = None
    for kk in range(TOP_K):
        words = buf_ref[kk]
        w = w_ref[kk]
        lo_k = lax.bitcast_convert_type(words << 16, F32) * w
        hi_k = lax.bitcast_convert_type(words & _HIGH_HALF, F32) * w
        lo = lo_k if lo is None else lo + lo_k
        hi = hi_k if hi is None else hi + hi_k
    o_ref[0] = lo
    o_ref[1] = hi


def moe_combine(dest, w_rows, y_rows, n, tm=256):
    grid_spec = pltpu.PrefetchScalarGridSpec(
        num_scalar_prefetch=1,
        grid=(n // tm,),
        in_specs=[pl.BlockSpec((TOP_K, tm, 1, LANES), lambda i, dest: (0, i, 0, 0)),
                  pl.BlockSpec(memory_space=pl.ANY)],
        out_specs=pl.BlockSpec((2, tm, ROW_SLABS, LANES), lambda i, dest: (0, i, 0, 0)),
        scratch_shapes=[pltpu.VMEM((TOP_K, tm, ROW_SLABS, LANES), jnp.int32),
                        pltpu.SemaphoreType.DMA((TOP_K,))],
    )
    return pl.pallas_call(
        functools.partial(_combine_kernel, tm=tm),
        out_shape=jax.ShapeDtypeStruct((2, n, ROW_SLABS, LANES), F32),
        grid_spec=grid_spec,
        compiler_params=_cparams(("arbitrary",)),
        name="moe_combine",
    )(dest, w_rows, y_rows)


def moe_sparse(h_rows, info, counts, w1, w3, w2):
    n = h_rows.shape[0] // ROW_SLABS
    tm = MOE_TM
    n_tiles = (TOP_K * n) // tm + N_EXPERTS
    n_rows = n_tiles * tm
    cnt = counts[0, :N_EXPERTS].astype(jnp.int32)
    tiles_per = (cnt + tm - 1) // tm
    tile_end = jnp.cumsum(tiles_per)
    group_start = (tile_end - tiles_per) * tm
    n_used = tile_end[-1]
    tile_ids = jnp.arange(n_tiles, dtype=jnp.int32)
    tile_expert = jnp.minimum(jnp.searchsorted(tile_end, tile_ids, side="right"),
                              N_EXPERTS - 1).astype(jnp.int32)
    tile_rows = jnp.clip(cnt[tile_expert] - (tile_ids * tm - group_start[tile_expert]), 0, tm)
    tile_rows = jnp.where(tile_ids < n_used, tile_rows, 0).astype(jnp.int32)
    tile_expert = jnp.where(tile_ids < n_used, tile_expert,
                            tile_expert[jnp.maximum(n_used - 1, 0)])
    experts = info[:, INFO_E1:INFO_E2 + 1].astype(jnp.int32)
    ranks = info[:, INFO_R1:INFO_R2 + 1].astype(jnp.int32)
    dest = (group_start[experts] + ranks).reshape(TOP_K * n)
    w_rows = jnp.broadcast_to(info[:, INFO_W1:INFO_W2 + 1].T[:, :, None, None],
                              (TOP_K, n, 1, LANES))

    h_sorted = moe_dispatch(dest, h_rows.reshape(n, ROW_SLABS, LANES), n_rows)
    y_sorted = moe_grouped_ffn(tile_expert, tile_rows,
                               h_sorted.reshape(n_rows * ROW_SLABS, LANES), w1, w3, w2)
    y = moe_combine(dest, w_rows, y_sorted.reshape(n_rows, ROW_SLABS, LANES), n)
    return y.reshape(2, n * ROW_SLABS, LANES)


def _ple_kernel(x_ref, y_ref, p_ref, wg_ref, wp_ref, *rest, y_slabs, with_norm):
    if with_norm:
        g_ref, o_ref, h_ref = rest
    else:
        (o_ref,) = rest
    if y_slabs:
        t = x_ref.shape[0]
        parts = []
        for half in range(2):
            for s in range(ROW_SLABS):
                parts.append(y_ref[half, pl.ds(s, t, stride=ROW_SLABS), :])
        x = x_ref[...] + jnp.concatenate(parts, axis=1)
    else:
        x = x_ref[...] + y_ref[...]
    xb = x.astype(BF16)
    pb = p_ref[...].astype(BF16)
    d = x.shape[1]
    sum_sq = jnp.zeros((x.shape[0], 1), F32)
    for c0 in range(0, d, PLE_COL_CHUNK):
        cs = slice(c0, c0 + PLE_COL_CHUNK)
        gate = _sigmoid(jnp.dot(xb, wg_ref[:, cs], preferred_element_type=F32))
        proj = jnp.dot(pb, wp_ref[:, cs], preferred_element_type=F32)
        xc = x[:, cs] + gate * proj
        o_ref[:, cs] = xc
        sum_sq += jnp.sum(xc * xc, axis=-1, keepdims=True)
    if with_norm:
        inv = lax.rsqrt(sum_sq * (1.0 / d) + EPS)
        h_ref[...] = (o_ref[...] * inv * g_ref[...]).astype(h_ref.dtype)


PLE_COL_CHUNK = 512


def ple_update(x, y, p, w_gate, w_proj, norm_gain=None, tm=512):
    n, d = x.shape
    pd = p.shape[1]
    y_slabs = y.ndim == 3
    with_norm = norm_gain is not None
    row = lambda width: pl.BlockSpec((tm, width), lambda i: (i, 0))
    resident = lambda shape: pl.BlockSpec(shape, lambda i: (0, 0), pipeline_mode=pl.Buffered(1))
    y_spec = (pl.BlockSpec((2, tm * ROW_SLABS, LANES), lambda i: (0, i, 0)) if y_slabs else row(d))
    in_specs = [row(d), y_spec, row(pd), resident((d, d)), resident((pd, d))]
    args = [x, y, p, w_gate, w_proj]
    out_shape = [jax.ShapeDtypeStruct((n, d), F32)]
    out_specs = [row(d)]
    if with_norm:
        in_specs.append(pl.BlockSpec((1, d), lambda i: (0, 0)))
        args.append(norm_gain.reshape(1, d))
        out_shape.append(jax.ShapeDtypeStruct((n, d), BF16))
        out_specs.append(row(d))
    res = pl.pallas_call(
        functools.partial(_ple_kernel, y_slabs=y_slabs, with_norm=with_norm),
        out_shape=tuple(out_shape),
        grid=(n // tm,),
        in_specs=in_specs,
        out_specs=tuple(out_specs),
        compiler_params=_cparams(("parallel",)),
        name="ple_update",
    )(*args)
    return res if with_norm else res[0]


def _split_w_in(w_in_i):
    c_lr = COL_GR
    main = jnp.concatenate([w_in_i[:, :c_lr], w_in_i[:, c_lr + GLA_RANK:]], axis=1).astype(BF16)
    glr = jnp.zeros((D_MODEL, LANES), BF16).at[:, :GLA_RANK].set(
        w_in_i[:, c_lr:c_lr + GLA_RANK].astype(BF16))
    return main, glr


def kernel(x, p, g_mix, w_in, w_branch_gate, b_branch_gate, pool_w, pool_scale, sb_gq, sb_gk,
           gla_w_lr, gla_b_lr, gla_g_out, w_up_pool, w_up_sb, w_up_gla, w_o, g_ffn,
           ffn_w1, ffn_w3, ffn_w2, moe_router, moe_w1, moe_w3, moe_w2, ple_w_proj, ple_w_gate):
    bsz, seq, d = x.shape
    n = bsz * seq
    depth = w_in.shape[0]
    xf = x.reshape(n, d)
    h = None
    for i in range(depth):
        w_main, w_glr = _split_w_in(w_in[i])
        w_lr_pad = jnp.zeros((LANES, GLA_K_W), F32).at[:GLA_RANK].set(gla_w_lr[i])
        qk_gain = jnp.ones((1, Z_COLS), F32)
        qk_gain = qk_gain.at[0, COL_SQ:COL_SK].set(jnp.tile(sb_gq[i] * SB_HEAD_DIM ** -0.5, SB_HEADS))
        qk_gain = qk_gain.at[0, COL_SK:COL_SV].set(jnp.tile(sb_gk[i], SB_HEADS))

        if i == 0:
            z, h = norm_in_proj(xf, g_mix[0], w_main, qk_gain)
        else:
            z = in_proj(h, w_main, qk_gain)
        bcum = gla_gate(h, w_glr, w_lr_pad, gla_b_lr[i])

        z3 = z.reshape(bsz, seq, Z_COLS)
        y_pool = pool_mixer(z3, pool_w[i].astype(BF16), pool_scale[i])
        y_sb = sb_attention(z3)
        y_gla = gla_mixer(z3, bcum.reshape(bsz, seq, GLA_K_W), gla_g_out[i])

        merged = merge_branches(h, w_branch_gate[i].astype(BF16), b_branch_gate[i],
                                y_pool.reshape(n, POOL_W), y_sb.reshape(n, SB_W),
                                y_gla.reshape(n, GLA_V_W), w_up_pool[i].astype(BF16),
                                w_up_sb[i].astype(BF16), w_up_gla[i].astype(BF16))

        j = i // 2
        if i % 2 == 0:
            xf, h2 = out_proj(merged, w_o[i].astype(BF16), xf, g_ffn[i])
            y = ffn_dense(h2, ffn_w1[j].astype(BF16), ffn_w3[j].astype(BF16),
                          ffn_w2[j].astype(BF16))
        else:
            xf, h2_rows = out_proj(merged, w_o[i].astype(BF16), xf, g_ffn[i], pack_rows=True)
            info, counts = route_tokens(xf, g_ffn[i], moe_router[j])
            y = moe_sparse(h2_rows, info, counts, moe_w1[j].astype(BF16),
                           moe_w3[j].astype(BF16), moe_w2[j].astype(BF16))

        if i + 1 < depth:
            xf, h = ple_update(xf, y, p[i].reshape(n, PLE_DIM), ple_w_gate[i].astype(BF16),
                               ple_w_proj[i].astype(BF16), norm_gain=g_mix[i + 1])
        else:
            xf = ple_update(xf, y, p[i].reshape(n, PLE_DIM), ple_w_gate[i].astype(BF16),
                            ple_w_proj[i].astype(BF16))
    return xf.reshape(bsz, seq, d)
```
